```python
import numpy as np
import jax, jax.numpy as jnp
from jax import lax

D_MODEL = 2048
BATCH = 4
SEQ = 2048
DEPTH = 4

N_EVEN = (DEPTH + 1) // 2
N_ODD = DEPTH // 2

A_HEADS = 8
A_DK = 128
A_DV = 128
A_QK = A_HEADS * A_DK
A_WIDTH = A_HEADS * A_DV
B_HEADS = 8
B_DK = 128
B_DV = 128
B_QK = B_HEADS * B_DK
B_WIDTH = B_HEADS * B_DV
CONV_K = 4
CHUNK = 64
EVEN_IN = 2 * A_QK + 2 * A_WIDTH + 2 * B_QK + 2 * B_WIDTH + 2 * B_HEADS
EVEN_OUT = A_WIDTH + B_WIDTH

C_HEADS = 16
C_GROUPS = 4
C_HPG = C_HEADS // C_GROUPS
C_DK = 128
C_DV = 128
C_Q = C_HEADS * C_DK
C_KV = C_GROUPS * C_DK
CMP_LEN = 32
CMP_STRIDE = 16
SEL_BLOCK = 64
N_SEL = 8
WINDOW = 512
Q_BLOCK = 128
SEL_Q_BLOCK = 64
SEL_FORCE = 1e4
ODD_IN = C_Q + 6 * C_KV + 3 * C_HEADS
ODD_OUT = C_HEADS * C_DV

D_FF = -(-8 * D_MODEL // (3 * 256)) * 256
NORM_EPS = 1e-6
NEG_INF = -1e30

kernel_name = 'hybrid_hgrn2_gdn_nsa_sandwich'


def rmsnorm(x, g):
    xf = x.astype(jnp.float32)
    y = xf * lax.rsqrt(jnp.mean(xf * xf, axis=-1, keepdims=True) + NORM_EPS)
    return (y * g.astype(jnp.float32)).astype(x.dtype)


def l2norm(x):
    return x * lax.rsqrt(jnp.sum(x * x, axis=-1, keepdims=True) + NORM_EPS)


def masked_softmax(s, mask):
    return jax.nn.softmax(jnp.where(mask, s, NEG_INF), axis=-1) * mask


def alibi_slopes(n):
    return jnp.asarray(2.0 ** (-8.0 * np.arange(1, n + 1) / n), jnp.float32)


def _chunks(x):
    b, t, h = x.shape[:3]
    x = x.reshape((b, t // CHUNK, CHUNK, h) + x.shape[3:])
    return jnp.moveaxis(x, 3, 1)


def _unchunk(o):
    n, b, h, c, d = o.shape
    return o.transpose(1, 0, 3, 2, 4).reshape(b, n * c, h, d)


def _causal_conv(x, w):
    return lax.conv_general_dilated(x, w[:, None, :], window_strides=(1,), padding=[(CONV_K - 1, 0)],
                                    dimension_numbers=('NWC', 'WIO', 'NWC'), feature_group_count=x.shape[-1])


def _hgrn2(q, f_logit, i, lb):
    b_, t_, h_, dk = q.shape
    dv = i.shape[-1]
    f = lb + (1.0 - lb) * jax.nn.sigmoid(f_logit)
    log_f = jnp.log(f)
    k = 1.0 - f
    xs = tuple(jnp.moveaxis(_chunks(a), 2, 0) for a in (q, k, i, log_f))
    tril = jnp.tril(jnp.ones((CHUNK, CHUNK), bool))[:, :, None]

    def step(S, inp):
        qn, kn, vn, ln = inp
        b = jnp.cumsum(ln, axis=2)
        decay = jnp.exp(jnp.where(tril, b[:, :, :, None, :] - b[:, :, None, :, :], -jnp.inf))
        intra = jnp.einsum('bhtd,bhsd,bhtsd->bhts', qn, kn, decay)
        o = jnp.einsum('bhts,bhse->bhte', intra, vn) + jnp.einsum('bhtd,bhde->bhte', qn * jnp.exp(b), S)
        b_end = b[:, :, -1:, :]
        S = jnp.exp(b_end[:, :, 0, :, None]) * S + jnp.einsum('bhsd,bhse->bhde', kn * jnp.exp(b_end - b), vn)
        return S, o

    S0 = jnp.zeros((b_, h_, dk, dv), jnp.float32)
    _, o = lax.scan(step, S0, xs)
    return _unchunk(o)


def _gated_delta(q, k, v, beta, log_decay):
    b_, t_, h_, dk = q.shape
    dv = v.shape[-1]
    qc, kc, vc = _chunks(q), _chunks(k), _chunks(v)
    bc = _chunks(beta)
    gc = jnp.cumsum(_chunks(log_decay), axis=-1)
    eye = jnp.eye(CHUNK, dtype=bool)
    tril = jnp.tril(jnp.ones((CHUNK, CHUNK), bool))
    decay = jnp.exp(jnp.where(tril, gc[..., :, None] - gc[..., None, :], -jnp.inf))
    kb = kc * bc[..., None]
    L = jnp.where(tril & ~eye, jnp.einsum('bhncd,bhnsd->bhncs', kb, kc) * decay, 0.0)
    rhs = jnp.concatenate([vc * bc[..., None], kb * jnp.exp(gc)[..., None]], axis=-1)
    sol = lax.linalg.triangular_solve(L + eye, rhs, left_side=True, lower=True, unit_diagonal=True)
    u, w = sol[..., :dv], sol[..., dv:]
    qk = jnp.einsum('bhncd,bhnsd->bhncs', qc, kc) * decay
    xs = tuple(jnp.moveaxis(a, 2, 0) for a in (qc, kc, u, w, qk, gc))

    def step(S, inp):
        qn, kn, un, wn, qkn, gn = inp
        v_new = un - jnp.einsum('bhcd,bhde->bhce', wn, S)
        o = (jnp.einsum('bhcd,bhde->bhce', qn * jnp.exp(gn)[..., None], S)
             + jnp.einsum('bhcs,bhse->bhce', qkn, v_new))
        g_end = gn[..., -1:]
        S = (jnp.exp(g_end)[..., None] * S
             + jnp.einsum('bhcd,bhce->bhde', kn * jnp.exp(g_end - gn)[..., None], v_new))
        return S, o

    S0 = jnp.zeros((b_, h_, dk, dv), jnp.float32)
    _, o = lax.scan(step, S0, xs)
    return _unchunk(o)


def _even_mixer(u, w_in, lb, conv_w, a_log, dt_bias, hgrn_g, gdn_g, w_out):
    b_, t_, _ = u.shape
    z = (u @ w_in).astype(jnp.float32)
    cuts = [int(c) for c in np.cumsum([A_QK, A_QK, A_WIDTH, A_WIDTH, B_QK, B_QK, B_WIDTH, B_WIDTH, B_HEADS])]
    aq, af, ai, ag, bq, bk, bv, bg, ba, bb = jnp.split(z, cuts, axis=-1)
    o_a = _hgrn2(aq.reshape(b_, t_, A_HEADS, A_DK), af.reshape(b_, t_, A_HEADS, A_DK),
                 ai.reshape(b_, t_, A_HEADS, A_DV), lb)
    o_a = rmsnorm(o_a, hgrn_g) * jax.nn.silu(ag.reshape(b_, t_, A_HEADS, A_DV))
    qkv = jax.nn.silu(_causal_conv(jnp.concatenate([bq, bk, bv], axis=-1), conv_w.astype(jnp.float32)))
    bq, bk, bv = jnp.split(qkv, [B_QK, 2 * B_QK], axis=-1)
    q = l2norm(bq.reshape(b_, t_, B_HEADS, B_DK)) * (B_DK ** -0.5)
    k = l2norm(bk.reshape(b_, t_, B_HEADS, B_DK))
    beta = jax.nn.sigmoid(bb)
    log_decay = -jnp.exp(a_log.astype(jnp.float32)) * jax.nn.softplus(ba + dt_bias.astype(jnp.float32))
    o_b = _gated_delta(q, k, bv.reshape(b_, t_, B_HEADS, B_DV), beta, log_decay)
    o_b = rmsnorm(o_b, gdn_g) * jax.nn.silu(bg.reshape(b_, t_, B_HEADS, B_DV))
    o = jnp.concatenate([o_a.reshape(b_, t_, A_WIDTH), o_b.reshape(b_, t_, B_WIDTH)], axis=-1)
    return o.astype(u.dtype) @ w_out


def _nsa_mixer(u, w_in, cmp_pos, cmp_w1, cmp_w2, w_out):
    b_, t_, _ = u.shape
    G, HP = C_GROUPS, C_HPG
    z = (u @ w_in).astype(jnp.float32)
    cuts = [C_Q + j * C_KV for j in range(7)]
    q, kc, vc, ks, vs, kw, vw, gl = jnp.split(z, cuts, axis=-1)
    kv = lambda a: a.reshape(b_, t_, G, C_DK)
    kc, vc, ks, vs, kw, vw = (kv(a) for a in (kc, vc, ks, vs, kw, vw))
    qg = q.reshape(b_, t_, G, HP, C_DK).transpose(0, 2, 3, 1, 4) * (C_DK ** -0.5)
    gates = jax.nn.sigmoid(gl.reshape(b_, t_, G, HP, 3)).transpose(0, 2, 3, 1, 4)
    slopes = alibi_slopes(C_HEADS).reshape(G, HP)
    tpos = jnp.arange(t_)

    n_cmp = (t_ - CMP_LEN) // CMP_STRIDE + 1
    cstart = jnp.arange(n_cmp) * CMP_STRIDE
    cidx = cstart[:, None] + jnp.arange(CMP_LEN)[None, :]

    def compress(a, pos, w1, w2):
        blk = a[:, cidx] + pos[None, None, :, None, :]
        blk = blk.transpose(0, 1, 3, 2, 4).reshape(b_, n_cmp, G, CMP_LEN * C_DK)
        return jax.nn.silu(blk @ w1) @ w2

    k_cmp = compress(kc, cmp_pos[0], cmp_w1[0], cmp_w2[0])
    v_cmp = compress(vc, cmp_pos[1], cmp_w1[1], cmp_w2[1])
    dist_c = tpos[:, None] - (cstart + CMP_LEN - 1)[None, :]
    s_c = jnp.einsum('bghtd,bjgd->bghtj', qg, k_cmp) - slopes[:, :, None, None] * dist_c
    p_c = masked_softmax(s_c, dist_c >= 0)
    o_cmp = jnp.einsum('bghtj,bjge->bghte', p_c, v_cmp)

    n_blk = t_ // SEL_BLOCK
    n_sel = min(N_SEL, n_blk)
    bstart = jnp.arange(n_blk) * SEL_BLOCK
    overlap = ((cstart[:, None] < bstart[None, :] + SEL_BLOCK)
               & (cstart[:, None] + CMP_LEN > bstart[None, :])).astype(jnp.float32)
    imp = jnp.einsum('bghtj,ji->bgti', p_c, overlap)
    cur = (tpos // SEL_BLOCK)[:, None]
    bi = jnp.arange(n_blk)[None, :]
    forced = (bi == 0) | (bi == cur) | (bi == cur - 1)
    score = jnp.where(forced, SEL_FORCE, jnp.where(bi > cur, -SEL_FORCE, imp))
    _, sel_idx = lax.top_k(score, n_sel)

    ks_blocks = ks.reshape(b_, n_blk, SEL_BLOCK, G, C_DK).transpose(0, 3, 1, 2, 4)
    vs_blocks = vs.reshape(b_, n_blk, SEL_BLOCK, G, C_DV).transpose(0, 3, 1, 2, 4)
    nqb = t_ // SEL_Q_BLOCK
    q_b = qg.reshape(b_, G, HP, nqb, SEL_Q_BLOCK, C_DK).transpose(3, 0, 1, 2, 4, 5)
    idx_b = sel_idx.reshape(b_, G, nqb, SEL_Q_BLOCK, n_sel).transpose(2, 0, 1, 3, 4)
    t_b = tpos.reshape(nqb, SEL_Q_BLOCK)
    gather = jax.vmap(jax.vmap(lambda blocks, ix: blocks[ix]))
    n_keys = n_sel * SEL_BLOCK

    def sel_block(args):
        qb, ib, tb = args
        kg = gather(ks_blocks, ib).reshape(b_, G, SEL_Q_BLOCK, n_keys, C_DK)
        vg = gather(vs_blocks, ib).reshape(b_, G, SEL_Q_BLOCK, n_keys, C_DV)
        kpos = (ib[..., None] * SEL_BLOCK + jnp.arange(SEL_BLOCK)).reshape(b_, G, SEL_Q_BLOCK, n_keys)
        dist = tb[None, None, :, None] - kpos
        s = (jnp.einsum('bghqd,bgqkd->bghqk', qb, kg)
             - slopes[None, :, :, None, None] * dist[:, :, None])
        p = masked_softmax(s, (dist >= 0)[:, :, None])
        return jnp.einsum('bghqk,bgqke->bghqe', p, vg)

    o_slc = lax.map(sel_block, (q_b, idx_b, t_b))
    o_slc = o_slc.transpose(1, 2, 3, 0, 4, 5).reshape(b_, G, HP, t_, C_DV)

    nwb = t_ // Q_BLOCK
    band = jnp.arange(nwb)[:, None] * Q_BLOCK + jnp.arange(Q_BLOCK + WINDOW)[None, :]
    pad = ((0, 0), (0, 0), (WINDOW, 0), (0, 0))
    kb = jnp.pad(kw.transpose(0, 2, 1, 3), pad)[:, :, band]
    vb = jnp.pad(vw.transpose(0, 2, 1, 3), pad)[:, :, band]
    kpos = band - WINDOW
    dist_w = tpos.reshape(nwb, Q_BLOCK)[:, :, None] - kpos[:, None, :]
    valid_w = (dist_w >= 0) & (dist_w < WINDOW) & (kpos[:, None, :] >= 0)
    qw = qg.reshape(b_, G, HP, nwb, Q_BLOCK, C_DK)
    s_w = jnp.einsum('bghnqd,bgnkd->bghnqk', qw, kb) - slopes[:, :, None, None, None] * dist_w
    p_w = masked_softmax(s_w, valid_w)
    o_win = jnp.einsum('bghnqk,bgnke->bghnqe', p_w, vb).reshape(b_, G, HP, t_, C_DV)

    o = gates[..., 0:1] * o_cmp + gates[..., 1:2] * o_slc + gates[..., 2:3] * o_win
    o = o.transpose(0, 3, 1, 2, 4).reshape(b_, t_, ODD_OUT)
    return o.astype(u.dtype) @ w_out


def _swiglu(u, w_gu, w_down):
    gate, up = jnp.split(u @ w_gu, 2, axis=-1)
    return (jax.nn.silu(gate) * up) @ w_down


def setup_inputs(seed: int = 0) -> dict:
    key = jax.random.key(seed)
    ks = jax.random.split(key, 20)
    f32 = jnp.float32
    nrm = lambda k, shape: jax.random.normal(k, shape, f32)
    w = lambda k, shape, fan_in: nrm(k, shape) * fan_in ** -0.5
    dt = jnp.exp(jax.random.uniform(ks[6], (N_EVEN, B_HEADS), f32, np.log(1e-3), np.log(1e-1)))
    return {
        'x': nrm(ks[0], (BATCH, SEQ, D_MODEL)),
        'norm_g': 1.0 + 0.05 * nrm(ks[1], (DEPTH, 4, D_MODEL)),
        'ab_w_in': w(ks[2], (N_EVEN, D_MODEL, EVEN_IN), D_MODEL),
        'hgrn_lb_logits': 0.5 * nrm(ks[3], (N_EVEN, A_QK)),
        'gdn_conv_w': w(ks[4], (N_EVEN, CONV_K, 2 * B_QK + B_WIDTH), CONV_K),
        'gdn_a_log': jnp.log(jax.random.uniform(ks[5], (N_EVEN, B_HEADS), f32, 1.0, 16.0)),
        'gdn_dt_bias': dt + jnp.log(-jnp.expm1(-dt)),
        'hgrn_norm_g': 1.0 + 0.05 * nrm(ks[7], (N_EVEN, A_DV)),
        'gdn_norm_g': 1.0 + 0.05 * nrm(ks[8], (N_EVEN, B_DV)),
        'ab_w_out': w(ks[9], (N_EVEN, EVEN_OUT, D_MODEL), EVEN_OUT),
        'nsa_w_in': w(ks[10], (N_ODD, D_MODEL, ODD_IN), D_MODEL),
        'nsa_cmp_pos': 0.1 * nrm(ks[11], (N_ODD, 2, CMP_LEN, C_DK)),
        'nsa_cmp_w1': w(ks[12], (N_ODD, 2, CMP_LEN * C_DK, C_DK), CMP_LEN * C_DK),
        'nsa_cmp_w2': w(ks[13], (N_ODD, 2, C_DK, C_DK), C_DK),
        'nsa_w_out': w(ks[14], (N_ODD, ODD_OUT, D_MODEL), ODD_OUT),
        'ffn_w_gu': w(ks[15], (DEPTH, D_MODEL, 2 * D_FF), D_MODEL),
        'ffn_w_down': w(ks[16], (DEPTH, D_FF, D_MODEL), D_FF),
    }


def reference(x, norm_g, ab_w_in, hgrn_lb_logits, gdn_conv_w, gdn_a_log, gdn_dt_bias, hgrn_norm_g,
              gdn_norm_g, ab_w_out, nsa_w_in, nsa_cmp_pos, nsa_cmp_w1, nsa_cmp_w2, nsa_w_out,
              ffn_w_gu, ffn_w_down):
    lb_all = jnp.cumsum(jax.nn.softmax(hgrn_lb_logits.astype(jnp.float32), axis=0), axis=0)
    lb_all = lb_all - lb_all[:1]
    h = x
    for layer in range(DEPTH):
        j = layer // 2
        u = rmsnorm(h, norm_g[layer, 0])
        if layer % 2 == 0:
            m = _even_mixer(u, ab_w_in[j], lb_all[j].reshape(A_HEADS, A_DK), gdn_conv_w[j], gdn_a_log[j],
                            gdn_dt_bias[j], hgrn_norm_g[j], gdn_norm_g[j], ab_w_out[j])
        else:
            m = _nsa_mixer(u, nsa_w_in[j], nsa_cmp_pos[j], nsa_cmp_w1[j], nsa_cmp_w2[j], nsa_w_out[j])
        h = h + rmsnorm(m, norm_g[layer, 1])
        u = rmsnorm(h, norm_g[layer, 2])
        h = h + rmsnorm(_swiglu(u, ffn_w_gu[layer], ffn_w_down[layer]), norm_g[layer, 3])
    return h
```

```python
import functools

import numpy as np
import jax
import jax.numpy as jnp
from jax import lax
from jax.experimental import pallas as pl
from jax.experimental.pallas import tpu as pltpu

F32 = jnp.float32
BF16 = jnp.bfloat16
HIGHEST = lax.Precision.HIGHEST

NORM_EPS = 1e-6
MASK_VALUE = -1e30

LANES = 128
HEAD_DIM = 128
CHUNK = 64
CONV_K = 4
EVEN_HEADS = 8
EVEN_WIDTH = EVEN_HEADS * HEAD_DIM

NSA_GROUPS = 4
NSA_HPG = 4
NSA_HEADS = NSA_GROUPS * NSA_HPG
CMP_LEN = 32
CMP_STRIDE = 16
SEL_BLOCK = 64
N_SEL = 8
WINDOW = 512
SEL_FORCE = 1e4
ATT_TILE = 128

VMEM_LIMIT = 56 * 1024 * 1024


def _params(semantics):
    return pltpu.CompilerParams(dimension_semantics=semantics, vmem_limit_bytes=VMEM_LIMIT)


def _rms(x, g):
    return x * lax.rsqrt(jnp.mean(x * x, axis=-1, keepdims=True) + NORM_EPS) * g


def _sigmoid(x):
    return 1.0 / (1.0 + jnp.exp(-x))


def _silu(x):
    return x * _sigmoid(x)


def _bdot(a, b):
    return jnp.dot(a.astype(BF16), b.astype(BF16), preferred_element_type=F32)


def _bdot_nt(a, b):
    return lax.dot_general(a.astype(BF16), b.astype(BF16), (((1,), (1,)), ((), ())),
                           preferred_element_type=F32)


def _hdot(a, b):
    return jnp.dot(a, b, precision=HIGHEST, preferred_element_type=F32)


def _div_pow2(x, n):
    return jnp.right_shift(x, int(n).bit_length() - 1)


def _proj_in_kernel(h_ref, g_ref, w_ref, ws_ref, z_ref, zs_ref, xn_ref):
    @pl.when(pl.program_id(1) == 0)
    def _():
        xn = _rms(h_ref[...], g_ref[...]).astype(BF16)
        xn_ref[...] = xn
        zs_ref[...] = jnp.dot(xn, ws_ref[...], preferred_element_type=F32)

    z_ref[...] = jnp.dot(xn_ref[...], w_ref[...], preferred_element_type=F32)


def _proj_in(h, g, w, ws, tm=512, tn=512):
    m, d = h.shape
    n = w.shape[1]
    return pl.pallas_call(
        _proj_in_kernel,
        grid=(m // tm, n // tn),
        in_specs=[
            pl.BlockSpec((tm, d), lambda i, j: (i, 0)),
            pl.BlockSpec((1, d), lambda i, j: (0, 0)),
            pl.BlockSpec((d, tn), lambda i, j: (0, j)),
            pl.BlockSpec((d, LANES), lambda i, j: (0, 0)),
        ],
        out_specs=[
            pl.BlockSpec((tm, tn), lambda i, j: (i, j)),
            pl.BlockSpec((tm, LANES), lambda i, j: (i, 0)),
        ],
        out_shape=[jax.ShapeDtypeStruct((m, n), F32), jax.ShapeDtypeStruct((m, LANES), F32)],
        scratch_shapes=[pltpu.VMEM((tm, d), BF16)],
        compiler_params=_params(("parallel", "arbitrary")),
        name="proj_in",
    )(h, g, w, ws)


def _proj_out_kernel(o_ref, w_ref, h_ref, g_ref, out_ref):
    m = jnp.dot(o_ref[...], w_ref[...], preferred_element_type=F32)
    out_ref[...] = h_ref[...] + _rms(m, g_ref[...])


def _proj_out(o, w, h, g, tm=512):
    m, k = o.shape
    d = w.shape[1]
    return pl.pallas_call(
        _proj_out_kernel,
        grid=(m // tm,),
        in_specs=[
            pl.BlockSpec((tm, k), lambda i: (i, 0)),
            pl.BlockSpec((k, d), lambda i: (0, 0)),
            pl.BlockSpec((tm, d), lambda i: (i, 0)),
            pl.BlockSpec((1, d), lambda i: (0, 0)),
        ],
        out_specs=pl.BlockSpec((tm, d), lambda i: (i, 0)),
        out_shape=jax.ShapeDtypeStruct((m, d), F32),
        compiler_params=_params(("parallel",)),
        name="proj_out",
    )(o, w, h, g)


def _ffn_kernel(h_ref, gin_ref, wg_ref, wu_ref, wd_ref, gout_ref, out_ref, xn_ref, acc_ref):
    f = pl.program_id(1)

    @pl.when(f == 0)
    def _():
        xn_ref[...] = _rms(h_ref[...], gin_ref[...]).astype(BF16)
        acc_ref[...] = jnp.zeros_like(acc_ref)

    xn = xn_ref[...]
    gate = jnp.dot(xn, wg_ref[...], preferred_element_type=F32)
    up = jnp.dot(xn, wu_ref[...], preferred_element_type=F32)
    act = (_silu(gate) * up).astype(BF16)
    acc_ref[...] += jnp.dot(act, wd_ref[...], preferred_element_type=F32)

    @pl.when(f == pl.num_programs(1) - 1)
    def _():
        out_ref[...] = h_ref[...] + _rms(acc_ref[...], gout_ref[...])


def _ffn(h, g_in, w_gu, w_down, g_out, tm=512, tf=512):
    m, d = h.shape
    d_ff = w_down.shape[0]
    nf = d_ff // tf
    return pl.pallas_call(
        _ffn_kernel,
        grid=(m // tm, nf),
        in_specs=[
            pl.BlockSpec((tm, d), lambda i, f: (i, 0)),
            pl.BlockSpec((1, d), lambda i, f: (0, 0)),
            pl.BlockSpec((d, tf), lambda i, f: (0, f)),
            pl.BlockSpec((d, tf), lambda i, f: (0, nf + f)),
            pl.BlockSpec((tf, d), lambda i, f: (f, 0)),
            pl.BlockSpec((1, d), lambda i, f: (0, 0)),
        ],
        out_specs=pl.BlockSpec((tm, d), lambda i, f: (i, 0)),
        out_shape=jax.ShapeDtypeStruct((m, d), F32),
        scratch_shapes=[pltpu.VMEM((tm, d), BF16), pltpu.VMEM((tm, d), F32)],
        compiler_params=_params(("parallel", "arbitrary")),
        name="ffn",
    )(h, g_in, w_gu, w_gu, w_down, g_out)


def _hgrn_kernel(q_ref, f_ref, i_ref, gate_ref, lb_ref, ng_ref, o_ref, st_ref, b_ref, k_ref):
    @pl.when(pl.program_id(1) == 0)
    def _():
        st_ref[...] = jnp.zeros_like(st_ref)

    row = lax.broadcasted_iota(jnp.int32, (CHUNK, LANES), 0)
    lane = lax.broadcasted_iota(jnp.int32, (CHUNK, LANES), 1)
    r64 = lax.broadcasted_iota(jnp.int32, (CHUNK, CHUNK), 0)
    c64 = lax.broadcasted_iota(jnp.int32, (CHUNK, CHUNK), 1)
    tril = (r64 >= c64).astype(F32)
    ones = jnp.ones((LANES, LANES), BF16)

    for hd in range(EVEN_HEADS):
        sl = slice(hd * HEAD_DIM, (hd + 1) * HEAD_DIM)
        q = q_ref[:, sl]
        lb = lb_ref[:, sl]
        f = lb + (1.0 - lb) * _sigmoid(f_ref[:, sl])
        k = 1.0 - f
        b = _hdot(tril, jnp.log(f))
        b_ref[...] = b
        k_ref[...] = k

        def body(s, a, q=q, b=b):
            bs = b_ref[pl.ds(s, 1), :]
            ks = k_ref[pl.ds(s, 1), :]
            e = jnp.exp(jnp.where(row >= s, b - bs, -jnp.inf)) * (q * ks)
            r = jnp.dot(e.astype(BF16), ones, preferred_element_type=F32)
            return jnp.where(lane == s, r, a)

        a = lax.fori_loop(0, CHUNK, body, jnp.zeros((CHUNK, LANES), F32))
        v = i_ref[:, sl]
        st = st_ref[hd]
        o = _bdot(a[:, :CHUNK], v) + _bdot_nt(q * jnp.exp(b), st)
        b_end = b[CHUNK - 1:CHUNK, :]
        kd = k * jnp.exp(b_end - b)
        st_ref[hd] = jnp.exp(b_end) * st + _bdot(v.T, kd)
        o_ref[:, sl] = (_rms(o, ng_ref[...]) * _silu(gate_ref[:, sl])).astype(o_ref.dtype)


def _hgrn(z, lb, ng, batch, seq):
    nchunk = seq // CHUNK
    w = EVEN_WIDTH
    col = lambda c: pl.BlockSpec((CHUNK, w), lambda b, n, c=c: (b * nchunk + n, c))
    return pl.pallas_call(
        _hgrn_kernel,
        grid=(batch, nchunk),
        in_specs=[col(0), col(1), col(2), col(3),
                  pl.BlockSpec((1, w), lambda b, n: (0, 0)),
                  pl.BlockSpec((1, HEAD_DIM), lambda b, n: (0, 0))],
        out_specs=pl.BlockSpec((CHUNK, w), lambda b, n: (b * nchunk + n, 0)),
        out_shape=jax.ShapeDtypeStruct((batch * seq, w), BF16),
        scratch_shapes=[pltpu.VMEM((EVEN_HEADS, HEAD_DIM, HEAD_DIM), F32),
                        pltpu.VMEM((CHUNK, LANES), F32), pltpu.VMEM((CHUNK, LANES), F32)],
        compiler_params=_params(("parallel", "arbitrary")),
        name="hgrn2",
    )(z, z, z, z, lb, ng)


def _unit_lower_inverse(l_mat):
    r = lax.broadcasted_iota(jnp.int32, (CHUNK, CHUNK), 0)
    c = lax.broadcasted_iota(jnp.int32, (CHUNK, CHUNK), 1)
    eye = (r == c).astype(F32)
    same16 = _div_pow2(r, 16) == _div_pow2(c, 16)
    same32 = _div_pow2(r, 32) == _div_pow2(c, 32)
    m1 = jnp.where(same16, -l_mat, 0.0)
    m2 = _hdot(m1, m1)
    m4 = _hdot(m2, m2)
    m8 = _hdot(m4, m4)
    p = eye + m1
    p = p + _hdot(p, m2)
    p = p + _hdot(p, m4)
    p = p + _hdot(p, m8)
    c1 = jnp.where(same32 & jnp.logical_not(same16), l_mat, 0.0)
    p = p - _hdot(_hdot(p, c1), p)
    c2 = jnp.where(same32, 0.0, l_mat)
    p = p - _hdot(_hdot(p, c2), p)
    return p


def _gdn_kernel(xq_ref, xk_ref, xv_ref, gate_ref, zs_ref, cw_ref, alog_ref, dtb_ref, ng_ref,
                o_ref, s_ref, xbuf_ref):
    n = pl.program_id(1)
    w = EVEN_WIDTH

    @pl.when(n == 0)
    def _():
        s_ref[...] = jnp.zeros_like(s_ref)
        xbuf_ref[:, 0:8, :] = jnp.zeros((3, 8, w), F32)

    conv = []
    for a, x_ref in enumerate((xq_ref, xk_ref, xv_ref)):
        x = x_ref[...]
        xbuf_ref[a, 8:8 + CHUNK, :] = x
        y = jnp.zeros((CHUNK, w), F32)
        for j in range(CONV_K):
            off = 8 - (CONV_K - 1) + j
            y = y + xbuf_ref[a, off:off + CHUNK, :] * cw_ref[j:j + 1, a * w:(a + 1) * w]
        xbuf_ref[a, 0:8, :] = x[CHUNK - 8:, :]
        conv.append(_silu(y))
    cq, ck, cv = conv

    zs = zs_ref[...]
    log_decay = -jnp.exp(alog_ref[...]) * jax.nn.softplus(zs + dtb_ref[...])
    r64 = lax.broadcasted_iota(jnp.int32, (CHUNK, CHUNK), 0)
    c64 = lax.broadcasted_iota(jnp.int32, (CHUNK, CHUNK), 1)
    tril = r64 >= c64
    gcum = _hdot(tril.astype(F32), log_decay)
    gcum_t = gcum.T
    beta_all = _sigmoid(zs)

    for hd in range(EVEN_HEADS):
        sl = slice(hd * HEAD_DIM, (hd + 1) * HEAD_DIM)
        q = cq[:, sl]
        q = q * lax.rsqrt(jnp.sum(q * q, axis=-1, keepdims=True) + NORM_EPS) * (HEAD_DIM ** -0.5)
        k = ck[:, sl]
        k = k * lax.rsqrt(jnp.sum(k * k, axis=-1, keepdims=True) + NORM_EPS)
        v = cv[:, sl]
        g_col = gcum[:, hd:hd + 1]
        g_row = gcum_t[hd:hd + 1, :]
        beta = beta_all[:, EVEN_HEADS + hd:EVEN_HEADS + hd + 1]
        decay = jnp.exp(jnp.where(tril, g_col - g_row, -jnp.inf))
        kb = k * beta
        l_mat = jnp.where(r64 > c64, _bdot_nt(kb, k) * decay, 0.0)
        qk = _bdot_nt(q, k) * decay
        eg = jnp.exp(g_col)
        rhs = jnp.concatenate([v * beta, kb * eg], axis=-1)
        sol = _hdot(_unit_lower_inverse(l_mat), rhs)
        u = sol[:, :HEAD_DIM]
        wm = sol[:, HEAD_DIM:]
        s = s_ref[hd]
        v_new = u - _bdot(wm, s)
        o = _bdot(q * eg, s) + _bdot(qk, v_new)
        g_end = g_col[CHUNK - 1:CHUNK, :]
        kd = k * jnp.exp(g_end - g_col)
        s_ref[hd] = jnp.exp(g_end) * s + _bdot(kd.T, v_new)
        o_ref[:, sl] = (_rms(o, ng_ref[...]) * _silu(gate_ref[:, sl])).astype(o_ref.dtype)


def _gdn(z, zs, conv_w, alog, dtb, ng, batch, seq):
    nchunk = seq // CHUNK
    w = EVEN_WIDTH
    col = lambda c: pl.BlockSpec((CHUNK, w), lambda b, n, c=c: (b * nchunk + n, c))
    vec = lambda width: pl.BlockSpec((1, width), lambda b, n: (0, 0))
    return pl.pallas_call(
        _gdn_kernel,
        grid=(batch, nchunk),
        in_specs=[col(4), col(5), col(6), col(7),
                  pl.BlockSpec((CHUNK, LANES), lambda b, n: (b * nchunk + n, 0)),
                  pl.BlockSpec((CONV_K, 3 * w), lambda b, n: (0, 0)),
                  vec(LANES), vec(LANES), vec(HEAD_DIM)],
        out_specs=pl.BlockSpec((CHUNK, w), lambda b, n: (b * nchunk + n, 0)),
        out_shape=jax.ShapeDtypeStruct((batch * seq, w), BF16),
        scratch_shapes=[pltpu.VMEM((EVEN_HEADS, HEAD_DIM, HEAD_DIM), F32),
                        pltpu.VMEM((3, 8 + CHUNK, w), F32)],
        compiler_params=_params(("parallel", "arbitrary")),
        name="gated_delta",
    )(z, z, z, z, zs, conv_w, alog, dtb, ng)


def _compress_kernel(kc_ref, vc_ref, pos_ref, w1_ref, w2_ref, ko_ref, vo_ref):
    nblk = ko_ref.shape[0]
    row = lax.broadcasted_iota(jnp.int32, (nblk, HEAD_DIM), 0)
    half = CMP_STRIDE * HEAD_DIM
    for a, (x_ref, out_ref) in enumerate(((kc_ref, ko_ref), (vc_ref, vo_ref))):
        p1 = jnp.zeros((nblk, HEAD_DIM), F32)
        p2 = jnp.zeros((nblk, HEAD_DIM), F32)
        for l in range(CMP_STRIDE):
            rows = x_ref[pl.ds(l, nblk, stride=CMP_STRIDE), :]
            p1 = p1 + _bdot(rows + pos_ref[a, l:l + 1, :], w1_ref[a, l * HEAD_DIM:(l + 1) * HEAD_DIM, :])
            p2 = p2 + _bdot(rows + pos_ref[a, CMP_STRIDE + l:CMP_STRIDE + l + 1, :],
                            w1_ref[a, half + l * HEAD_DIM:half + (l + 1) * HEAD_DIM, :])
        pre = p1 + pltpu.roll(p2, nblk - 1, 0)
        out = _bdot(_silu(pre), w2_ref[a])
        out_ref[...] = jnp.where(row < nblk - 1, out, 0.0)


def _compress(z, pos, w1, w2, batch, seq):
    nblk = seq // CMP_STRIDE
    src = lambda base: pl.BlockSpec((None, seq, HEAD_DIM), lambda b, g, base=base: (b, 0, base + g))
    out = pl.BlockSpec((None, None, nblk, HEAD_DIM), lambda b, g: (b, g, 0, 0))
    full = lambda a: pl.BlockSpec(a.shape, lambda b, g: (0,) * a.ndim)
    shape = jax.ShapeDtypeStruct((batch, NSA_GROUPS, nblk, HEAD_DIM), F32)
    return pl.pallas_call(
        _compress_kernel,
        grid=(batch, NSA_GROUPS),
        in_specs=[src(16), src(20), full(pos), full(w1), full(w2)],
        out_specs=[out, out],
        out_shape=[shape, shape],
        compiler_params=_params(("parallel", "parallel")),
        name="nsa_compress",
    )(z, z, pos, w1, w2)


def _nsa_kernel(q_ref, kc_ref, vc_ref, ks_ref, vs_ref, kw_ref, vw_ref, gl_ref, o_ref,
                m_ref, l_ref, acc_ref, *, seq):
    g = pl.program_id(1)
    qi = pl.program_id(2)
    tq = ATT_TILE
    rows = NSA_HPG * tq
    q0 = qi * tq
    ncmp = seq // CMP_STRIDE
    nblk = seq // SEL_BLOCK

    qh = jnp.concatenate([q_ref[:, hp * HEAD_DIM:(hp + 1) * HEAD_DIM] for hp in range(NSA_HPG)], axis=0)
    qh = qh * (HEAD_DIM ** -0.5)
    qb = qh.astype(BF16)

    rrow = lax.broadcasted_iota(jnp.int32, (rows, LANES), 0)
    rlane = lax.broadcasted_iota(jnp.int32, (rows, LANES), 1)
    hp_idx = _div_pow2(rrow, tq)
    tpos = q0 + (rrow - hp_idx * tq)
    head = (g * NSA_HPG + hp_idx).astype(F32)
    slope = jnp.exp((-8.0 * np.log(2.0) / NSA_HEADS) * (head + 1.0))

    clane = lax.broadcasted_iota(jnp.int32, (rows, ncmp), 1)
    dist_c = tpos[:, :ncmp] - (clane * CMP_STRIDE + CMP_LEN - 1)
    valid_c = (dist_c >= 0) & (clane < ncmp - 1)
    s_c = lax.dot_general(qh, kc_ref[...], (((1,), (1,)), ((), ())), precision=HIGHEST,
                          preferred_element_type=F32)
    s_c = jnp.where(valid_c, s_c - slope[:, :ncmp] * dist_c.astype(F32), MASK_VALUE)
    m_c = jnp.max(s_c, axis=-1, keepdims=True)
    e_c = jnp.where(valid_c, jnp.exp(s_c - m_c), 0.0)
    den = jnp.sum(e_c, axis=-1, keepdims=True)
    p_c = e_c / jnp.where(den > 0.0, den, 1.0)
    o_cmp = _bdot(p_c, vc_ref[...])

    p_sum = p_c[0:tq] + p_c[tq:2 * tq] + p_c[2 * tq:3 * tq] + p_c[3 * tq:4 * tq]
    oj = lax.broadcasted_iota(jnp.int32, (ncmp, LANES), 0) * CMP_STRIDE
    ob = lax.broadcasted_iota(jnp.int32, (ncmp, LANES), 1) * SEL_BLOCK
    overlap = ((oj < ob + SEL_BLOCK) & (oj + CMP_LEN > ob)).astype(F32)
    imp = _hdot(p_sum, overlap)
    lane = lax.broadcasted_iota(jnp.int32, (tq, LANES), 1)
    cur = _div_pow2(q0 + lax.broadcasted_iota(jnp.int32, (tq, LANES), 0), SEL_BLOCK)
    forced = (lane == 0) | (lane == cur) | (lane == cur - 1)
    score = jnp.where(forced, SEL_FORCE, jnp.where(lane > cur, -SEL_FORCE, imp))
    score = jnp.where(lane < nblk, score, -jnp.inf)
    sel = jnp.zeros((tq, LANES), F32)
    for _ in range(min(N_SEL, nblk)):
        best = jnp.max(score, axis=-1, keepdims=True)
        pick = jnp.min(jnp.where(score == best, lane, LANES), axis=-1, keepdims=True)
        hit = lane == pick
        sel = jnp.where(hit, 1.0, sel)
        score = jnp.where(hit, -jnp.inf, score)
    sel_b = sel.astype(BF16)

    ones = jnp.ones((LANES, LANES), BF16)
    erow = lax.broadcasted_iota(jnp.int32, (LANES, LANES), 0)
    elane = lax.broadcasted_iota(jnp.int32, (LANES, LANES), 1)

    def attend(k_ref, v_ref, lo, hi, mask_fn):
        m_ref[...] = jnp.full_like(m_ref, MASK_VALUE)
        l_ref[...] = jnp.zeros_like(l_ref)
        acc_ref[...] = jnp.zeros_like(acc_ref)

        def body(kt, carry):
            k0 = pl.multiple_of(kt * ATT_TILE, ATT_TILE)
            kpos = k0 + rlane
            dist = tpos - kpos
            valid = mask_fn(kt, dist)
            s = _bdot_nt(qb, k_ref[pl.ds(k0, ATT_TILE), :])
            s = jnp.where(valid, s - slope * dist.astype(F32), MASK_VALUE)
            m_prev = m_ref[...]
            m_new = jnp.maximum(m_prev, jnp.max(s, axis=-1, keepdims=True))
            p = jnp.where(valid, jnp.exp(s - m_new), 0.0).astype(BF16)
            alpha = jnp.exp(m_prev - m_new)
            l_ref[...] = alpha * l_ref[...] + jnp.dot(p, ones, preferred_element_type=F32)
            acc_ref[...] = alpha * acc_ref[...] + jnp.dot(p, v_ref[pl.ds(k0, ATT_TILE), :].astype(BF16),
                                                          preferred_element_type=F32)
            m_ref[...] = m_new
            return carry

        lax.fori_loop(lo, hi, body, 0)
        l = l_ref[...]
        return jnp.where(l > 0.0, acc_ref[...] / jnp.where(l > 0.0, l, 1.0), 0.0)

    def sel_mask(kt, dist):
        expand = (erow == kt * (ATT_TILE // SEL_BLOCK) + _div_pow2(elane, SEL_BLOCK)).astype(BF16)
        chosen = jnp.dot(sel_b, expand, preferred_element_type=F32)
        chosen = jnp.concatenate([chosen] * NSA_HPG, axis=0)
        return (chosen > 0.5) & (dist >= 0)

    o_slc = attend(ks_ref, vs_ref, 0, qi + 1, sel_mask)

    def win_mask(kt, dist):
        return (dist >= 0) & (dist < WINDOW)

    o_win = attend(kw_ref, vw_ref, jnp.maximum(qi - WINDOW // ATT_TILE, 0), qi + 1, win_mask)

    ngate = NSA_HPG * 3
    pick_gate = (erow == g * ngate + elane).astype(F32)
    gates = _hdot(_sigmoid(gl_ref[...]), pick_gate)
    for hp in range(NSA_HPG):
        rs = slice(hp * tq, (hp + 1) * tq)
        o = (gates[:, 3 * hp:3 * hp + 1] * o_cmp[rs] + gates[:, 3 * hp + 1:3 * hp + 2] * o_slc[rs]
             + gates[:, 3 * hp + 2:3 * hp + 3] * o_win[rs])
        o_ref[:, hp * HEAD_DIM:(hp + 1) * HEAD_DIM] = o.astype(o_ref.dtype)


def _nsa_attention(z, zs, k_cmp, v_cmp, batch, seq):
    tq = ATT_TILE
    hq = NSA_HPG * HEAD_DIM
    nblk = seq // CMP_STRIDE
    kv = lambda base: pl.BlockSpec((None, seq, HEAD_DIM), lambda b, g, i, base=base: (b, 0, base + g))
    cmp_spec = pl.BlockSpec((None, None, nblk, HEAD_DIM), lambda b, g, i: (b, g, 0, 0))
    return pl.pallas_call(
        functools.partial(_nsa_kernel, seq=seq),
        grid=(batch, NSA_GROUPS, seq // tq),
        in_specs=[pl.BlockSpec((None, tq, hq), lambda b, g, i: (b, i, g)),
                  cmp_spec, cmp_spec, kv(24), kv(28), kv(32), kv(36),
                  pl.BlockSpec((None, tq, LANES), lambda b, g, i: (b, i, 0))],
        out_specs=pl.BlockSpec((None, tq, hq), lambda b, g, i: (b, i, g)),
        out_shape=jax.ShapeDtypeStruct((batch, seq, NSA_HEADS * HEAD_DIM), BF16),
        scratch_shapes=[pltpu.VMEM((NSA_HPG * tq, LANES), F32)] * 3,
        compiler_params=_params(("parallel", "parallel", "arbitrary")),
        name="nsa_attention",
    )(z, k_cmp, v_cmp, z, z, z, z, zs)


def _split_w_in(w_in, n_main):
    tail = w_in[:, n_main:]
    tail = jnp.pad(tail, ((0, 0), (0, LANES - tail.shape[1])))
    return w_in[:, :n_main].astype(BF16), tail.astype(BF16)


def _pad_lanes(v, offset=0):
    out = jnp.zeros((1, LANES), F32)
    return out.at[0, offset:offset + v.shape[0]].set(v.astype(F32))


def _even_layer(h, g_in, g_out, w_in, lb, conv_w, a_log, dt_bias, hgrn_g, gdn_g, w_out, batch, seq):
    w_main, w_tail = _split_w_in(w_in, 8 * EVEN_WIDTH)
    z, zs = _proj_in(h, g_in, w_main, w_tail)
    o_a = _hgrn(z, lb.reshape(1, EVEN_WIDTH), hgrn_g.reshape(1, HEAD_DIM).astype(F32), batch, seq)
    o_b = _gdn(z, zs, conv_w.astype(F32), _pad_lanes(a_log), _pad_lanes(dt_bias),
               gdn_g.reshape(1, HEAD_DIM).astype(F32), batch, seq)
    o = jnp.concatenate([o_a, o_b], axis=-1)
    return _proj_out(o, w_out.astype(BF16), h, g_out)


def _odd_layer(h, g_in, g_out, w_in, cmp_pos, cmp_w1, cmp_w2, w_out, batch, seq):
    n_main = NSA_HEADS * HEAD_DIM + 6 * NSA_GROUPS * HEAD_DIM
    w_main, w_tail = _split_w_in(w_in, n_main)
    z, zs = _proj_in(h, g_in, w_main, w_tail)
    z3 = z.reshape(batch, seq, n_main)
    k_cmp, v_cmp = _compress(z3, cmp_pos.astype(F32), cmp_w1.astype(BF16), cmp_w2.astype(BF16), batch, seq)
    o = _nsa_attention(z3, zs.reshape(batch, seq, LANES), k_cmp, v_cmp, batch, seq)
    return _proj_out(o.reshape(batch * seq, -1), w_out.astype(BF16), h, g_out)


def kernel(x, norm_g, ab_w_in, hgrn_lb_logits, gdn_conv_w, gdn_a_log, gdn_dt_bias, hgrn_norm_g, gdn_norm_g,
           ab_w_out, nsa_w_in, nsa_cmp_pos, nsa_cmp_w1, nsa_cmp_w2, nsa_w_out, ffn_w_gu, ffn_w_down):
    batch, seq, d = x.shape
    depth = norm_g.shape[0]
    lb_all = jnp.cumsum(jax.nn.softmax(hgrn_lb_logits.astype(F32), axis=0), axis=0)
    lb_all = lb_all - lb_all[:1]
    g = norm_g.astype(F32).reshape(depth, 4, 1, d)
    h = x.reshape(batch * seq, d)
    for layer in range(depth):
        j = layer // 2
        if layer % 2 == 0:
            h = _even_layer(h, g[layer, 0], g[layer, 1], ab_w_in[j], lb_all[j], gdn_conv_w[j], gdn_a_log[j],
                            gdn_dt_bias[j], hgrn_norm_g[j], gdn_norm_g[j], ab_w_out[j], batch, seq)
        else:
            h = _odd_layer(h, g[layer, 0], g[layer, 1], nsa_w_in[j], nsa_cmp_pos[j], nsa_cmp_w1[j],
                           nsa_cmp_w2[j], nsa_w_out[j], batch, seq)
        h = _ffn(h, g[layer, 2], ffn_w_gu[layer].astype(BF16), ffn_w_down[layer].astype(BF16), g[layer, 3])
    return h.reshape(batch, seq, d)
```

```python
import functools

import numpy as np
import jax
import jax.numpy as jnp
from jax import lax
from jax.experimental import pallas as pl
from jax.experimental.pallas import tpu as pltpu

F32 = jnp.float32
BF16 = jnp.bfloat16
HIGHEST = lax.Precision.HIGHEST

NORM_EPS = 1e-6
MASK_VALUE = -1e30

LANES = 128
HEAD_DIM = 128
CHUNK = 64
HGRN_BAND = 16
CONV_K = 4
EVEN_HEADS = 8
EVEN_WIDTH = EVEN_HEADS * HEAD_DIM

NSA_GROUPS = 4
NSA_HPG = 4
NSA_HEADS = NSA_GROUPS * NSA_HPG
CMP_LEN = 32
CMP_STRIDE = 16
SEL_BLOCK = 64
N_SEL = 8
WINDOW = 512
SEL_FORCE = 1e4
ATT_TILE = 128
KAUG_POS_LANE = 32
SEL_MASK_BIG = 2.0 ** 100

VMEM_LIMIT = 56 * 1024 * 1024


def _params(semantics):
    return pltpu.CompilerParams(dimension_semantics=semantics, vmem_limit_bytes=VMEM_LIMIT)


def _rms(x, g):
    return x * lax.rsqrt(jnp.mean(x * x, axis=-1, keepdims=True) + NORM_EPS) * g


def _sigmoid(x):
    return 1.0 / (1.0 + jnp.exp(-x))


def _silu(x):
    return x * _sigmoid(x)


def _bdot(a, b):
    return jnp.dot(a.astype(BF16), b.astype(BF16), preferred_element_type=F32)


def _bdot_nt(a, b):
    return lax.dot_general(a.astype(BF16), b.astype(BF16), (((1,), (1,)), ((), ())),
                           preferred_element_type=F32)


def _hdot(a, b):
    return jnp.dot(a, b, precision=HIGHEST, preferred_element_type=F32)


def _div_pow2(x, n):
    return jnp.right_shift(x, int(n).bit_length() - 1)


def _proj_in_kernel(h_ref, g_ref, w_ref, ws_ref, z_ref, zs_ref, xn_ref):
    @pl.when(pl.program_id(1) == 0)
    def _():
        xn = _rms(h_ref[...], g_ref[...]).astype(BF16)
        xn_ref[...] = xn
        zs_ref[...] = jnp.dot(xn, ws_ref[...], preferred_element_type=F32)

    z_ref[...] = jnp.dot(xn_ref[...], w_ref[...], preferred_element_type=F32)


def _proj_in(h, g, w, ws, tm=512, tn=512):
    m, d = h.shape
    n = w.shape[1]
    return pl.pallas_call(
        _proj_in_kernel,
        grid=(m // tm, n // tn),
        in_specs=[
            pl.BlockSpec((tm, d), lambda i, j: (i, 0)),
            pl.BlockSpec((1, d), lambda i, j: (0, 0)),
            pl.BlockSpec((d, tn), lambda i, j: (0, j)),
            pl.BlockSpec((d, LANES), lambda i, j: (0, 0)),
        ],
        out_specs=[
            pl.BlockSpec((tm, tn), lambda i, j: (i, j)),
            pl.BlockSpec((tm, LANES), lambda i, j: (i, 0)),
        ],
        out_shape=[jax.ShapeDtypeStruct((m, n), F32), jax.ShapeDtypeStruct((m, LANES), F32)],
        scratch_shapes=[pltpu.VMEM((tm, d), BF16)],
        compiler_params=_params(("parallel", "arbitrary")),
        name="proj_in",
    )(h, g, w, ws)


def _proj_out_kernel(o_ref, w_ref, h_ref, g_ref, out_ref):
    m = jnp.dot(o_ref[...], w_ref[...], preferred_element_type=F32)
    out_ref[...] = h_ref[...] + _rms(m, g_ref[...])


def _proj_out(o, w, h, g, tm=512):
    m, k = o.shape
    d = w.shape[1]
    return pl.pallas_call(
        _proj_out_kernel,
        grid=(m // tm,),
        in_specs=[
            pl.BlockSpec((tm, k), lambda i: (i, 0)),
            pl.BlockSpec((k, d), lambda i: (0, 0)),
            pl.BlockSpec((tm, d), lambda i: (i, 0)),
            pl.BlockSpec((1, d), lambda i: (0, 0)),
        ],
        out_specs=pl.BlockSpec((tm, d), lambda i: (i, 0)),
        out_shape=jax.ShapeDtypeStruct((m, d), F32),
        compiler_params=_params(("parallel",)),
        name="proj_out",
    )(o, w, h, g)


def _ffn_kernel(h_ref, gin_ref, wg_ref, wu_ref, wd_ref, gout_ref, out_ref, xn_ref, acc_ref):
    f = pl.program_id(1)

    @pl.when(f == 0)
    def _():
        xn_ref[...] = _rms(h_ref[...], gin_ref[...]).astype(BF16)
        acc_ref[...] = jnp.zeros_like(acc_ref)

    xn = xn_ref[...]
    gate = jnp.dot(xn, wg_ref[...], preferred_element_type=F32)
    up = jnp.dot(xn, wu_ref[...], preferred_element_type=F32)
    act = (_silu(gate) * up).astype(BF16)
    acc_ref[...] += jnp.dot(act, wd_ref[...], preferred_element_type=F32)

    @pl.when(f == pl.num_programs(1) - 1)
    def _():
        out_ref[...] = h_ref[...] + _rms(acc_ref[...], gout_ref[...])


def _ffn(h, g_in, w_gu, w_down, g_out, tm=512, tf=512):
    m, d = h.shape
    d_ff = w_down.shape[0]
    nf = d_ff // tf
    return pl.pallas_call(
        _ffn_kernel,
        grid=(m // tm, nf),
        in_specs=[
            pl.BlockSpec((tm, d), lambda i, f: (i, 0)),
            pl.BlockSpec((1, d), lambda i, f: (0, 0)),
            pl.BlockSpec((d, tf), lambda i, f: (0, f)),
            pl.BlockSpec((d, tf), lambda i, f: (0, nf + f)),
            pl.BlockSpec((tf, d), lambda i, f: (f, 0)),
            pl.BlockSpec((1, d), lambda i, f: (0, 0)),
        ],
        out_specs=pl.BlockSpec((tm, d), lambda i, f: (i, 0)),
        out_shape=jax.ShapeDtypeStruct((m, d), F32),
        scratch_shapes=[pltpu.VMEM((tm, d), BF16), pltpu.VMEM((tm, d), F32)],
        compiler_params=_params(("parallel", "arbitrary")),
        name="ffn",
    )(h, g_in, w_gu, w_gu, w_down, g_out)


def _hgrn_kernel(q_ref, f_ref, i_ref, gate_ref, lb_ref, ng_ref, o_ref, st_ref, bpad_ref, kpad_ref):
    band = HGRN_BAND
    w = EVEN_WIDTH

    @pl.when(pl.program_id(1) == 0)
    def _():
        st_ref[...] = jnp.zeros_like(st_ref)
        bpad_ref[0:band, :] = jnp.zeros((band, w), F32)
        kpad_ref[0:band, :] = jnp.zeros((band, w), F32)

    r64 = lax.broadcasted_iota(jnp.int32, (CHUNK, CHUNK), 0)
    c64 = lax.broadcasted_iota(jnp.int32, (CHUNK, CHUNK), 1)
    tril = (r64 >= c64).astype(F32)
    offset = (lax.broadcasted_iota(jnp.int32, (CHUNK, LANES), 0)
              - lax.broadcasted_iota(jnp.int32, (CHUNK, LANES), 1))
    ones = jnp.ones((LANES, LANES), BF16)

    lb = lb_ref[...]
    f_all = lb + (1.0 - lb) * _sigmoid(f_ref[...])
    b_all = _hdot(tril, jnp.log(f_all))
    bpad_ref[band:band + CHUNK, :] = b_all
    kpad_ref[band:band + CHUNK, :] = 1.0 - f_all

    for hd in range(EVEN_HEADS):
        sl = slice(hd * HEAD_DIM, (hd + 1) * HEAD_DIM)
        q = q_ref[:, sl]
        b = b_all[:, sl]
        k = kpad_ref[band:band + CHUNK, sl]
        v = i_ref[:, sl]

        a = jnp.zeros((CHUNK, LANES), F32)
        for d in range(band):
            bs = bpad_ref[band - d:band - d + CHUNK, sl]
            ks = kpad_ref[band - d:band - d + CHUNK, sl]
            e = jnp.exp(b - bs) * (q * ks)
            r = jnp.dot(e.astype(BF16), ones, preferred_element_type=F32)
            a = jnp.where(offset == d, r, a)

        far = [jnp.zeros((band, CHUNK), F32)]
        for blk in range(1, CHUNK // band):
            r0 = blk * band
            br = b[r0:r0 + 1, :]
            qi = q[r0:r0 + band, :] * jnp.exp(b[r0:r0 + band, :] - br)
            ki = k * jnp.exp(jnp.minimum(br - b, 0.0))
            far.append(_bdot_nt(qi, ki))
        a_far = jnp.where(r64 - c64 >= band, jnp.concatenate(far, axis=0), 0.0)

        st = st_ref[hd]
        o = _bdot(a[:, :CHUNK] + a_far, v) + _bdot_nt(q * jnp.exp(b), st)
        b_end = b[CHUNK - 1:CHUNK, :]
        kd = k * jnp.exp(b_end - b)
        st_ref[hd] = jnp.exp(b_end) * st + _bdot(v.T, kd)
        o_ref[:, sl] = (_rms(o, ng_ref[...]) * _silu(gate_ref[:, sl])).astype(o_ref.dtype)


def _hgrn(z, lb, ng, batch, seq):
    nchunk = seq // CHUNK
    w = EVEN_WIDTH
    col = lambda c: pl.BlockSpec((CHUNK, w), lambda b, n, c=c: (b * nchunk + n, c))
    return pl.pallas_call(
        _hgrn_kernel,
        grid=(batch, nchunk),
        in_specs=[col(0), col(1), col(2), col(3),
                  pl.BlockSpec((1, w), lambda b, n: (0, 0)),
                  pl.BlockSpec((1, HEAD_DIM), lambda b, n: (0, 0))],
        out_specs=pl.BlockSpec((CHUNK, w), lambda b, n: (b * nchunk + n, 0)),
        out_shape=jax.ShapeDtypeStruct((batch * seq, w), BF16),
        scratch_shapes=[pltpu.VMEM((EVEN_HEADS, HEAD_DIM, HEAD_DIM), F32),
                        pltpu.VMEM((HGRN_BAND + CHUNK, w), F32), pltpu.VMEM((HGRN_BAND + CHUNK, w), F32)],
        compiler_params=_params(("parallel", "arbitrary")),
        name="hgrn2",
    )(z, z, z, z, lb, ng)


def _unit_lower_inverses(l_mats):
    r = lax.broadcasted_iota(jnp.int32, (CHUNK, CHUNK), 0)
    c = lax.broadcasted_iota(jnp.int32, (CHUNK, CHUNK), 1)
    eye = (r == c).astype(F32)
    same16 = _div_pow2(r, 16) == _div_pow2(c, 16)
    same32 = _div_pow2(r, 32) == _div_pow2(c, 32)
    m = [jnp.where(same16, -l, 0.0) for l in l_mats]
    p = [eye + x for x in m]
    for _ in range(3):
        m = [_hdot(x, x) for x in m]
        p = [y + _hdot(y, x) for y, x in zip(p, m)]
    for off_diag in (same32 & jnp.logical_not(same16), jnp.logical_not(same32)):
        t = [_hdot(y, jnp.where(off_diag, l, 0.0)) for y, l in zip(p, l_mats)]
        p = [y - _hdot(x, y) for y, x in zip(p, t)]
    return p


def _gdn_kernel(xq_ref, xk_ref, xv_ref, gate_ref, zs_ref, cw_ref, alog_ref, dtb_ref, ng_ref,
                o_ref, s_ref, xbuf_ref):
    n = pl.program_id(1)
    w = EVEN_WIDTH

    @pl.when(n == 0)
    def _():
        s_ref[...] = jnp.zeros_like(s_ref)
        xbuf_ref[:, 0:8, :] = jnp.zeros((3, 8, w), F32)

    conv = []
    for a, x_ref in enumerate((xq_ref, xk_ref, xv_ref)):
        x = x_ref[...]
        xbuf_ref[a, 8:8 + CHUNK, :] = x
        y = jnp.zeros((CHUNK, w), F32)
        for j in range(CONV_K):
            off = 8 - (CONV_K - 1) + j
            y = y + xbuf_ref[a, off:off + CHUNK, :] * cw_ref[j:j + 1, a * w:(a + 1) * w]
        xbuf_ref[a, 0:8, :] = x[CHUNK - 8:, :]
        conv.append(_silu(y))
    cq, ck, cv = conv

    zs = zs_ref[...]
    log_decay = -jnp.exp(alog_ref[...]) * jax.nn.softplus(zs + dtb_ref[...])
    r64 = lax.broadcasted_iota(jnp.int32, (CHUNK, CHUNK), 0)
    c64 = lax.broadcasted_iota(jnp.int32, (CHUNK, CHUNK), 1)
    tril = r64 >= c64
    gcum = _hdot(tril.astype(F32), log_decay)
    gcum_t = gcum.T
    beta_all = _sigmoid(zs)

    heads = range(EVEN_HEADS)
    sls = [slice(hd * HEAD_DIM, (hd + 1) * HEAD_DIM) for hd in heads]
    l2 = lambda x: x * lax.rsqrt(jnp.sum(x * x, axis=-1, keepdims=True) + NORM_EPS)
    q = [l2(cq[:, sl]) * (HEAD_DIM ** -0.5) for sl in sls]
    k = [l2(ck[:, sl]) for sl in sls]
    v = [cv[:, sl] for sl in sls]
    g_col = [gcum[:, hd:hd + 1] for hd in heads]
    beta = [beta_all[:, EVEN_HEADS + hd:EVEN_HEADS + hd + 1] for hd in heads]
    decay = [jnp.exp(jnp.where(tril, g_col[hd] - gcum_t[hd:hd + 1, :], -jnp.inf)) for hd in heads]
    kb = [k[hd] * beta[hd] for hd in heads]
    l_mat = [jnp.where(r64 > c64, _bdot_nt(kb[hd], k[hd]) * decay[hd], 0.0) for hd in heads]
    qk = [_bdot_nt(q[hd], k[hd]) * decay[hd] for hd in heads]
    eg = [jnp.exp(g_col[hd]) for hd in heads]
    rhs = [jnp.concatenate([v[hd] * beta[hd], kb[hd] * eg[hd]], axis=-1) for hd in heads]
    inv = _unit_lower_inverses(l_mat)
    sol = [_hdot(inv[hd], rhs[hd]) for hd in heads]
    s = [s_ref[hd] for hd in heads]
    v_new = [sol[hd][:, :HEAD_DIM] - _bdot(sol[hd][:, HEAD_DIM:], s[hd]) for hd in heads]
    o = [_bdot(q[hd] * eg[hd], s[hd]) + _bdot(qk[hd], v_new[hd]) for hd in heads]
    for hd in heads:
        g_end = g_col[hd][CHUNK - 1:CHUNK, :]
        kd = k[hd] * jnp.exp(g_end - g_col[hd])
        s_ref[hd] = jnp.exp(g_end) * s[hd] + _bdot(kd.T, v_new[hd])
        o_ref[:, sls[hd]] = (_rms(o[hd], ng_ref[...]) * _silu(gate_ref[:, sls[hd]])).astype(o_ref.dtype)


def _gdn(z, zs, conv_w, alog, dtb, ng, batch, seq):
    nchunk = seq // CHUNK
    w = EVEN_WIDTH
    col = lambda c: pl.BlockSpec((CHUNK, w), lambda b, n, c=c: (b * nchunk + n, c))
    vec = lambda width: pl.BlockSpec((1, width), lambda b, n: (0, 0))
    return pl.pallas_call(
        _gdn_kernel,
        grid=(batch, nchunk),
        in_specs=[col(4), col(5), col(6), col(7),
                  pl.BlockSpec((CHUNK, LANES), lambda b, n: (b * nchunk + n, 0)),
                  pl.BlockSpec((CONV_K, 3 * w), lambda b, n: (0, 0)),
                  vec(LANES), vec(LANES), vec(HEAD_DIM)],
        out_specs=pl.BlockSpec((CHUNK, w), lambda b, n: (b * nchunk + n, 0)),
        out_shape=jax.ShapeDtypeStruct((batch * seq, w), BF16),
        scratch_shapes=[pltpu.VMEM((EVEN_HEADS, HEAD_DIM, HEAD_DIM), F32),
                        pltpu.VMEM((3, 8 + CHUNK, w), F32)],
        compiler_params=_params(("parallel", "arbitrary")),
        name="gated_delta",
    )(z, z, z, z, zs, conv_w, alog, dtb, ng)


def _compress_kernel(kc_ref, vc_ref, pos_ref, w1_ref, w2_ref, ko_ref, vo_ref):
    nblk = ko_ref.shape[0]
    row = lax.broadcasted_iota(jnp.int32, (nblk, HEAD_DIM), 0)
    half = CMP_STRIDE * HEAD_DIM
    for a, (x_ref, out_ref) in enumerate(((kc_ref, ko_ref), (vc_ref, vo_ref))):
        p1 = jnp.zeros((nblk, HEAD_DIM), F32)
        p2 = jnp.zeros((nblk, HEAD_DIM), F32)
        for l in range(CMP_STRIDE):
            rows = x_ref[pl.ds(l, nblk, stride=CMP_STRIDE), :]
            p1 = p1 + _bdot(rows + pos_ref[a, l:l + 1, :], w1_ref[a, l * HEAD_DIM:(l + 1) * HEAD_DIM, :])
            p2 = p2 + _bdot(rows + pos_ref[a, CMP_STRIDE + l:CMP_STRIDE + l + 1, :],
                            w1_ref[a, half + l * HEAD_DIM:half + (l + 1) * HEAD_DIM, :])
        pre = p1 + pltpu.roll(p2, nblk - 1, 0)
        out = _bdot(_silu(pre), w2_ref[a])
        out_ref[...] = jnp.where(row < nblk - 1, out, 0.0)


def _compress(z, pos, w1, w2, batch, seq):
    nblk = seq // CMP_STRIDE
    src = lambda base: pl.BlockSpec((None, seq, HEAD_DIM), lambda b, g, base=base: (b, 0, base + g))
    out = pl.BlockSpec((None, None, nblk, HEAD_DIM), lambda b, g: (b, g, 0, 0))
    full = lambda a: pl.BlockSpec(a.shape, lambda b, g: (0,) * a.ndim)
    shape = jax.ShapeDtypeStruct((batch, NSA_GROUPS, nblk, HEAD_DIM), F32)
    return pl.pallas_call(
        _compress_kernel,
        grid=(batch, NSA_GROUPS),
        in_specs=[src(16), src(20), full(pos), full(w1), full(w2)],
        out_specs=[out, out],
        out_shape=[shape, shape],
        compiler_params=_params(("parallel", "parallel")),
        name="nsa_compress",
    )(z, z, pos, w1, w2)


def _nsa_kernel(q_ref, kc_ref, vc_ref, ks_ref, vs_ref, kw_ref, vw_ref, gl_ref, kaug_ref, o_ref,
                m_ref, l_ref, acc_ref, m2_ref, l2_ref, acc2_ref, *, seq):
    g = pl.program_id(1)
    qi = pl.program_id(2)
    tq = ATT_TILE
    rows = NSA_HPG * tq
    q0 = qi * tq
    ncmp = seq // CMP_STRIDE
    nblk = seq // SEL_BLOCK

    qh = jnp.concatenate([q_ref[:, hp * HEAD_DIM:(hp + 1) * HEAD_DIM] for hp in range(NSA_HPG)], axis=0)
    qh = qh * (HEAD_DIM ** -0.5)
    qb = qh.astype(BF16)

    rrow = lax.broadcasted_iota(jnp.int32, (rows, LANES), 0)
    rlane = lax.broadcasted_iota(jnp.int32, (rows, LANES), 1)
    hp_idx = _div_pow2(rrow, tq)
    tpos = q0 + (rrow - hp_idx * tq)
    head = (g * NSA_HPG + hp_idx).astype(F32)
    slope = jnp.exp((-8.0 * np.log(2.0) / NSA_HEADS) * (head + 1.0))

    clane = lax.broadcasted_iota(jnp.int32, (rows, ncmp), 1)
    dist_c = tpos[:, :ncmp] - (clane * CMP_STRIDE + CMP_LEN - 1)
    valid_c = (dist_c >= 0) & (clane < ncmp - 1)
    s_c = lax.dot_general(qh, kc_ref[...], (((1,), (1,)), ((), ())), precision=HIGHEST,
                          preferred_element_type=F32)
    s_c = jnp.where(valid_c, s_c - slope[:, :ncmp] * dist_c.astype(F32), MASK_VALUE)
    m_c = jnp.max(s_c, axis=-1, keepdims=True)
    e_c = jnp.where(valid_c, jnp.exp(s_c - m_c), 0.0)
    den = jnp.sum(e_c, axis=-1, keepdims=True)
    p_c = e_c / jnp.where(den > 0.0, den, 1.0)
    o_cmp = _bdot(p_c, vc_ref[...])

    p_sum = p_c[0:tq] + p_c[tq:2 * tq] + p_c[2 * tq:3 * tq] + p_c[3 * tq:4 * tq]
    oj = lax.broadcasted_iota(jnp.int32, (ncmp, LANES), 0) * CMP_STRIDE
    ob = lax.broadcasted_iota(jnp.int32, (ncmp, LANES), 1) * SEL_BLOCK
    overlap = ((oj < ob + SEL_BLOCK) & (oj + CMP_LEN > ob)).astype(F32)
    imp = _hdot(p_sum, overlap)
    lane = lax.broadcasted_iota(jnp.int32, (tq, LANES), 1)
    cur = _div_pow2(q0 + lax.broadcasted_iota(jnp.int32, (tq, LANES), 0), SEL_BLOCK)
    forced = (lane == 0) | (lane == cur) | (lane == cur - 1)
    score = jnp.where(forced, SEL_FORCE, jnp.where(lane > cur, -SEL_FORCE, imp))
    score = jnp.where(lane < nblk, score, -jnp.inf)
    sel = jnp.zeros((tq, LANES), F32)
    for _ in range(min(N_SEL, nblk)):
        best = jnp.max(score, axis=-1, keepdims=True)
        pick = jnp.min(jnp.where(score == best, lane, LANES), axis=-1, keepdims=True)
        hit = lane == pick
        sel = jnp.where(hit, 1.0, sel)
        score = jnp.where(hit, -jnp.inf, score)
    s_hi = slope.astype(BF16).astype(F32)
    s_mid = (slope - s_hi).astype(BF16).astype(F32)
    s_lo = slope - s_hi - s_mid
    piece = jnp.where(rlane < KAUG_POS_LANE + 2, s_hi, jnp.where(rlane < KAUG_POS_LANE + 4, s_mid, s_lo))
    bias_cols = jnp.where((rlane >= KAUG_POS_LANE) & (rlane < KAUG_POS_LANE + 6), piece, 0.0)
    sel_cols = (jnp.concatenate([sel] * NSA_HPG, axis=0) - 1.0) * SEL_MASK_BIG
    q_sel = jnp.concatenate([qb, jnp.where(rlane < KAUG_POS_LANE, sel_cols, bias_cols).astype(BF16)], axis=1)
    q_win = jnp.concatenate([qb, bias_cols.astype(BF16)], axis=1)

    pair = 2 * ATT_TILE
    rrow2 = lax.broadcasted_iota(jnp.int32, (rows, pair), 0)
    dist0 = (q0 + rrow2 - _div_pow2(rrow2, tq) * tq) - lax.broadcasted_iota(jnp.int32, (rows, pair), 1)
    ones = jnp.ones((pair, LANES), BF16)
    for ref in (m_ref, m2_ref):
        ref[...] = jnp.full_like(ref, MASK_VALUE)
    for ref in (l_ref, acc_ref, l2_ref, acc2_ref):
        ref[...] = jnp.zeros_like(ref)
    sel_branch = (q_sel, ks_ref, vs_ref, m_ref, l_ref, acc_ref)
    win_branch = (q_win, kw_ref, vw_ref, m2_ref, l2_ref, acc2_ref)

    def attend_pair(j, branches, masks):
        k0 = pl.multiple_of(j * pair, pair)
        kaug = kaug_ref[pl.ds(k0, pair), :]
        s = []
        for (qa, k_ref, _, _, _, _), mask in zip(branches, masks):
            kp = jnp.concatenate([k_ref[pl.ds(k0, pair), :].astype(BF16), kaug], axis=1)
            sc = lax.dot_general(qa, kp, (((1,), (1,)), ((), ())), preferred_element_type=F32)
            s.append(sc if mask is None else jnp.where(mask(k0), sc, MASK_VALUE))
        m_prev = [br[3][...] for br in branches]
        m_new = [jnp.maximum(mp, jnp.max(jnp.maximum(sc[:, :LANES], sc[:, LANES:]), axis=-1, keepdims=True))
                 for mp, sc in zip(m_prev, s)]
        p = [jnp.concatenate([jnp.exp(sc[:, :LANES] - mn), jnp.exp(sc[:, LANES:] - mn)], axis=1).astype(BF16)
             for sc, mn in zip(s, m_new)]
        alpha = [jnp.exp(mp - mn) for mp, mn in zip(m_prev, m_new)]
        for (_, _, v_ref, mr, lr, ar), pb, al, mn in zip(branches, p, alpha, m_new):
            lr[...] = al * lr[...] + jnp.dot(pb, ones, preferred_element_type=F32)
            ar[...] = al * ar[...] + jnp.dot(pb, v_ref[pl.ds(k0, pair), :].astype(BF16),
                                             preferred_element_type=F32)
            mr[...] = mn

    causal = lambda k0: dist0 >= k0
    in_window = lambda k0: dist0 < k0 + WINDOW
    npair = jnp.right_shift(qi, 1) + 1
    win_pairs = WINDOW // pair + 1

    def far_body(j, carry):
        attend_pair(j, (sel_branch,), (None,))
        return carry

    lax.fori_loop(0, jnp.maximum(npair - win_pairs, 0), far_body, 0)
    for back in range(win_pairs, 0, -1):
        j = npair - back
        sel_mask = causal if back == 1 else None
        win_mask = causal if back == 1 else (in_window if back == win_pairs else None)

        @pl.when(j >= 0)
        def _(j=j, sel_mask=sel_mask, win_mask=win_mask):
            attend_pair(j, (sel_branch, win_branch), (sel_mask, win_mask))

    o_slc = acc_ref[...] / l_ref[...]
    o_win = acc2_ref[...] / l2_ref[...]

    ngate = NSA_HPG * 3
    erow = lax.broadcasted_iota(jnp.int32, (LANES, LANES), 0)
    elane = lax.broadcasted_iota(jnp.int32, (LANES, LANES), 1)
    pick_gate = (erow == g * ngate + elane).astype(F32)
    gates = _hdot(_sigmoid(gl_ref[...]), pick_gate)
    for hp in range(NSA_HPG):
        rs = slice(hp * tq, (hp + 1) * tq)
        o = (gates[:, 3 * hp:3 * hp + 1] * o_cmp[rs] + gates[:, 3 * hp + 1:3 * hp + 2] * o_slc[rs]
             + gates[:, 3 * hp + 2:3 * hp + 3] * o_win[rs])
        o_ref[:, hp * HEAD_DIM:(hp + 1) * HEAD_DIM] = o.astype(o_ref.dtype)


def _nsa_attention(z, zs, k_cmp, v_cmp, batch, seq):
    tq = ATT_TILE
    hq = NSA_HPG * HEAD_DIM
    nblk = seq // CMP_STRIDE
    assert seq % (2 * ATT_TILE) == 0 and seq // SEL_BLOCK <= KAUG_POS_LANE
    kv = lambda base: pl.BlockSpec((None, seq, HEAD_DIM), lambda b, g, i, base=base: (b, 0, base + g))
    cmp_spec = pl.BlockSpec((None, None, nblk, HEAD_DIM), lambda b, g, i: (b, g, 0, 0))
    return pl.pallas_call(
        functools.partial(_nsa_kernel, seq=seq),
        grid=(batch, NSA_GROUPS, seq // tq),
        in_specs=[pl.BlockSpec((None, tq, hq), lambda b, g, i: (b, i, g)),
                  cmp_spec, cmp_spec, kv(24), kv(28), kv(32), kv(36),
                  pl.BlockSpec((None, tq, LANES), lambda b, g, i: (b, i, 0)),
                  pl.BlockSpec((seq, LANES), lambda b, g, i: (0, 0))],
        out_specs=pl.BlockSpec((None, tq, hq), lambda b, g, i: (b, i, g)),
        out_shape=jax.ShapeDtypeStruct((batch, seq, NSA_HEADS * HEAD_DIM), BF16),
        scratch_shapes=[pltpu.VMEM((NSA_HPG * tq, LANES), F32)] * 6,
        compiler_params=_params(("parallel", "parallel", "arbitrary")),
        name="nsa_attention",
    )(z, k_cmp, v_cmp, z, z, z, z, zs, _key_aug(seq))


def _key_aug(seq):
    pair = 2 * ATT_TILE
    kpos = np.arange(seq)
    aug = np.zeros((seq, LANES), np.float32)
    aug[kpos, kpos // SEL_BLOCK] = 1.0
    for c in range(3):
        aug[:, KAUG_POS_LANE + 2 * c] = kpos - kpos % pair
        aug[:, KAUG_POS_LANE + 2 * c + 1] = kpos % pair
    return jnp.asarray(aug, BF16)


def _split_w_in(w_in, n_main):
    tail = w_in[:, n_main:]
    tail = jnp.pad(tail, ((0, 0), (0, LANES - tail.shape[1])))
    return w_in[:, :n_main].astype(BF16), tail.astype(BF16)


def _pad_lanes(v, offset=0):
    out = jnp.zeros((1, LANES), F32)
    return out.at[0, offset:offset + v.shape[0]].set(v.astype(F32))


def _even_layer(h, g_in, g_out, w_in, lb, conv_w, a_log, dt_bias, hgrn_g, gdn_g, w_out, batch, seq):
    w_main, w_tail = _split_w_in(w_in, 8 * EVEN_WIDTH)
    z, zs = _proj_in(h, g_in, w_main, w_tail)
    o_a = _hgrn(z, lb.reshape(1, EVEN_WIDTH), hgrn_g.reshape(1, HEAD_DIM).astype(F32), batch, seq)
    o_b = _gdn(z, zs, conv_w.astype(F32), _pad_lanes(a_log), _pad_lanes(dt_bias),
               gdn_g.reshape(1, HEAD_DIM).astype(F32), batch, seq)
    o = jnp.concatenate([o_a, o_b], axis=-1)
    return _proj_out(o, w_out.astype(BF16), h, g_out)


def _odd_layer(h, g_in, g_out, w_in, cmp_pos, cmp_w1, cmp_w2, w_out, batch, seq):
    n_main = NSA_HEADS * HEAD_DIM + 6 * NSA_GROUPS * HEAD_DIM
    w_main, w_tail = _split_w_in(w_in, n_main)
    z, zs = _proj_in(h, g_in, w_main, w_tail)
    z3 = z.reshape(batch, seq, n_main)
    k_cmp, v_cmp = _compress(z3, cmp_pos.astype(F32), cmp_w1.astype(BF16), cmp_w2.astype(BF16), batch, seq)
    o = _nsa_attention(z3, zs.reshape(batch, seq, LANES), k_cmp, v_cmp, batch, seq)
    return _proj_out(o.reshape(batch * seq, -1), w_out.astype(BF16), h, g_out)


def kernel(x, norm_g, ab_w_in, hgrn_lb_logits, gdn_conv_w, gdn_a_log, gdn_dt_bias, hgrn_norm_g, gdn_norm_g,
           ab_w_out, nsa_w_in, nsa_cmp_pos, nsa_cmp_w1, nsa_cmp_w2, nsa_w_out, ffn_w_gu, ffn_w_down):
    batch, seq, d = x.shape
    depth = norm_g.shape[0]
    lb_all = jnp.cumsum(jax.nn.softmax(hgrn_lb_logits.astype(F32), axis=0), axis=0)
    lb_all = lb_all - lb_all[:1]
    g = norm_g.astype(F32).reshape(depth, 4, 1, d)
    h = x.reshape(batch * seq, d)
    for layer in range(depth):
        j = layer // 2
        if layer % 2 == 0:
            h = _even_layer(h, g[layer, 0], g[layer, 1], ab_w_in[j], lb_all[j], gdn_conv_w[j], gdn_a_log[j],
                            gdn_dt_bias[j], hgrn_norm_g[j], gdn_norm_g[j], ab_w_out[j], batch, seq)
        else:
            h = _odd_layer(h, g[layer, 0], g[layer, 1], nsa_w_in[j], nsa_cmp_pos[j], nsa_cmp_w1[j],
                           nsa_cmp_w2[j], nsa_w_out[j], batch, seq)
        h = _ffn(h, g[layer, 2], ffn_w_gu[layer].astype(BF16), ffn_w_down[layer].astype(BF16), g[layer, 3])
    return h.reshape(batch, seq, d)
```

```python
import functools

import numpy as np
import jax
import jax.numpy as jnp
from jax import lax
from jax.experimental import pallas as pl
from jax.experimental.pallas import tpu as pltpu

F32 = jnp.float32
BF16 = jnp.bfloat16
HIGHEST = lax.Precision.HIGHEST

NORM_EPS = 1e-6
MASK_VALUE = -1e30

LANES = 128
HEAD_DIM = 128
CHUNK = 64
HGRN_BAND = 16
CONV_K = 4
EVEN_HEADS = 8
EVEN_WIDTH = EVEN_HEADS * HEAD_DIM

NSA_GROUPS = 4
NSA_HPG = 4
NSA_HEADS = NSA_GROUPS * NSA_HPG
CMP_LEN = 32
CMP_STRIDE = 16
SEL_BLOCK = 64
N_SEL = 8
WINDOW = 512
SEL_FORCE = 1e4
ATT_TILE = 256
KEY_TILE = 256
KAUG_POS_LANE = 32
SEL_MASK_BIG = 2.0 ** 100

VMEM_LIMIT = 56 * 1024 * 1024


def _params(semantics):
    return pltpu.CompilerParams(dimension_semantics=semantics, vmem_limit_bytes=VMEM_LIMIT)


def _rms(x, g):
    return x * lax.rsqrt(jnp.mean(x * x, axis=-1, keepdims=True) + NORM_EPS) * g


def _sigmoid(x):
    return 1.0 / (1.0 + jnp.exp(-x))


def _silu(x):
    return x * _sigmoid(x)


def _bdot(a, b):
    return jnp.dot(a.astype(BF16), b.astype(BF16), preferred_element_type=F32)


def _bdot_nt(a, b):
    return lax.dot_general(a.astype(BF16), b.astype(BF16), (((1,), (1,)), ((), ())),
                           preferred_element_type=F32)


def _hdot(a, b):
    return jnp.dot(a, b, precision=HIGHEST, preferred_element_type=F32)


def _div_pow2(x, n):
    return jnp.right_shift(x, int(n).bit_length() - 1)


def _proj_in_kernel(h_ref, g_ref, w_ref, ws_ref, z_ref, zs_ref, xn_ref):
    @pl.when(pl.program_id(1) == 0)
    def _():
        xn = _rms(h_ref[...], g_ref[...]).astype(BF16)
        xn_ref[...] = xn
        zs_ref[...] = jnp.dot(xn, ws_ref[...], preferred_element_type=F32)

    z_ref[...] = jnp.dot(xn_ref[...], w_ref[...], preferred_element_type=F32).astype(z_ref.dtype)


def _proj_in(h, g, w, ws, tm=512, tn=1024):
    m, d = h.shape
    n = w.shape[1]
    assert m % tm == 0 and n % tn == 0
    return pl.pallas_call(
        _proj_in_kernel,
        grid=(m // tm, n // tn),
        in_specs=[
            pl.BlockSpec((tm, d), lambda i, j: (i, 0)),
            pl.BlockSpec((1, d), lambda i, j: (0, 0)),
            pl.BlockSpec((d, tn), lambda i, j: (0, j)),
            pl.BlockSpec((d, LANES), lambda i, j: (0, 0)),
        ],
        out_specs=[
            pl.BlockSpec((tm, tn), lambda i, j: (i, j)),
            pl.BlockSpec((tm, LANES), lambda i, j: (i, 0)),
        ],
        out_shape=[jax.ShapeDtypeStruct((m, n), BF16), jax.ShapeDtypeStruct((m, LANES), F32)],
        scratch_shapes=[pltpu.VMEM((tm, d), BF16)],
        compiler_params=_params(("parallel", "arbitrary")),
        name="proj_in",
    )(h, g, w, ws)


def _proj_out_kernel(o_ref, w_ref, h_ref, g_ref, out_ref):
    m = jnp.dot(o_ref[...], w_ref[...], preferred_element_type=F32)
    out_ref[...] = h_ref[...] + _rms(m, g_ref[...])


def _proj_out(o, w, h, g, tm=512):
    m, k = o.shape
    d = w.shape[1]
    return pl.pallas_call(
        _proj_out_kernel,
        grid=(m // tm,),
        in_specs=[
            pl.BlockSpec((tm, k), lambda i: (i, 0)),
            pl.BlockSpec((k, d), lambda i: (0, 0)),
            pl.BlockSpec((tm, d), lambda i: (i, 0)),
            pl.BlockSpec((1, d), lambda i: (0, 0)),
        ],
        out_specs=pl.BlockSpec((tm, d), lambda i: (i, 0)),
        out_shape=jax.ShapeDtypeStruct((m, d), F32),
        compiler_params=_params(("parallel",)),
        name="proj_out",
    )(o, w, h, g)


def _ffn_kernel(h_ref, gin_ref, wg_ref, wu_ref, wd_ref, gout_ref, out_ref, xn_ref, acc_ref):
    f = pl.program_id(1)

    @pl.when(f == 0)
    def _():
        xn_ref[...] = _rms(h_ref[...], gin_ref[...]).astype(BF16)
        acc_ref[...] = jnp.zeros_like(acc_ref)

    xn = xn_ref[...]
    gate = jnp.dot(xn, wg_ref[...], preferred_element_type=F32)
    up = jnp.dot(xn, wu_ref[...], preferred_element_type=F32)
    act = (_silu(gate) * up).astype(BF16)
    acc_ref[...] += jnp.dot(act, wd_ref[...], preferred_element_type=F32)

    @pl.when(f == pl.num_programs(1) - 1)
    def _():
        out_ref[...] = h_ref[...] + _rms(acc_ref[...], gout_ref[...])


def _ffn(h, g_in, w_gu, w_down, g_out, tm=512, tf=512):
    m, d = h.shape
    d_ff = w_down.shape[0]
    nf = d_ff // tf
    return pl.pallas_call(
        _ffn_kernel,
        grid=(m // tm, nf),
        in_specs=[
            pl.BlockSpec((tm, d), lambda i, f: (i, 0)),
            pl.BlockSpec((1, d), lambda i, f: (0, 0)),
            pl.BlockSpec((d, tf), lambda i, f: (0, f)),
            pl.BlockSpec((d, tf), lambda i, f: (0, nf + f)),
            pl.BlockSpec((tf, d), lambda i, f: (f, 0)),
            pl.BlockSpec((1, d), lambda i, f: (0, 0)),
        ],
        out_specs=pl.BlockSpec((tm, d), lambda i, f: (i, 0)),
        out_shape=jax.ShapeDtypeStruct((m, d), F32),
        scratch_shapes=[pltpu.VMEM((tm, d), BF16), pltpu.VMEM((tm, d), F32)],
        compiler_params=_params(("parallel", "arbitrary")),
        name="ffn",
    )(h, g_in, w_gu, w_gu, w_down, g_out)


def _hgrn_kernel(q_ref, f_ref, i_ref, gate_ref, lb_ref, ng_ref, o_ref, st_ref, bpad_ref, kpad_ref):
    band = HGRN_BAND
    w = EVEN_WIDTH

    @pl.when(pl.program_id(1) == 0)
    def _():
        st_ref[...] = jnp.zeros_like(st_ref)
        bpad_ref[0:band, :] = jnp.zeros((band, w), F32)
        kpad_ref[0:band, :] = jnp.zeros((band, w), F32)

    r64 = lax.broadcasted_iota(jnp.int32, (CHUNK, CHUNK), 0)
    c64 = lax.broadcasted_iota(jnp.int32, (CHUNK, CHUNK), 1)
    tril = (r64 >= c64).astype(F32)
    offset = (lax.broadcasted_iota(jnp.int32, (CHUNK, LANES), 0)
              - lax.broadcasted_iota(jnp.int32, (CHUNK, LANES), 1))
    ones = jnp.ones((LANES, LANES), BF16)

    lb = lb_ref[...]
    f_all = lb + (1.0 - lb) * _sigmoid(f_ref[...].astype(F32))
    b_all = _hdot(tril, jnp.log(f_all))
    bpad_ref[band:band + CHUNK, :] = b_all
    kpad_ref[band:band + CHUNK, :] = 1.0 - f_all

    for hd in range(EVEN_HEADS):
        sl = slice(hd * HEAD_DIM, (hd + 1) * HEAD_DIM)
        q = q_ref[:, sl].astype(F32)
        b = b_all[:, sl]
        k = kpad_ref[band:band + CHUNK, sl]
        v = i_ref[:, sl].astype(F32)

        a = jnp.zeros((CHUNK, LANES), F32)
        for d in range(band):
            bs = bpad_ref[band - d:band - d + CHUNK, sl]
            ks = kpad_ref[band - d:band - d + CHUNK, sl]
            e = jnp.exp(b - bs) * (q * ks)
            r = jnp.dot(e.astype(BF16), ones, preferred_element_type=F32)
            a = jnp.where(offset == d, r, a)

        far = [jnp.zeros((band, CHUNK), F32)]
        for blk in range(1, CHUNK // band):
            r0 = blk * band
            br = b[r0:r0 + 1, :]
            qi = q[r0:r0 + band, :] * jnp.exp(b[r0:r0 + band, :] - br)
            ki = k * jnp.exp(jnp.minimum(br - b, 0.0))
            far.append(_bdot_nt(qi, ki))
        a_far = jnp.where(r64 - c64 >= band, jnp.concatenate(far, axis=0), 0.0)

        st = st_ref[hd]
        o = _bdot(a[:, :CHUNK] + a_far, v) + _bdot_nt(q * jnp.exp(b), st)
        b_end = b[CHUNK - 1:CHUNK, :]
        kd = k * jnp.exp(b_end - b)
        st_ref[hd] = jnp.exp(b_end) * st + _bdot(v.T, kd)
        o_ref[:, sl] = (_rms(o, ng_ref[...]) * _silu(gate_ref[:, sl].astype(F32))).astype(o_ref.dtype)


def _hgrn(z, lb, ng, batch, seq):
    nchunk = seq // CHUNK
    w = EVEN_WIDTH
    col = lambda c: pl.BlockSpec((CHUNK, w), lambda b, n, c=c: (b * nchunk + n, c))
    return pl.pallas_call(
        _hgrn_kernel,
        grid=(batch, nchunk),
        in_specs=[col(0), col(1), col(2), col(3),
                  pl.BlockSpec((1, w), lambda b, n: (0, 0)),
                  pl.BlockSpec((1, HEAD_DIM), lambda b, n: (0, 0))],
        out_specs=pl.BlockSpec((CHUNK, w), lambda b, n: (b * nchunk + n, 0)),
        out_shape=jax.ShapeDtypeStruct((batch * seq, w), BF16),
        scratch_shapes=[pltpu.VMEM((EVEN_HEADS, HEAD_DIM, HEAD_DIM), F32),
                        pltpu.VMEM((HGRN_BAND + CHUNK, w), F32), pltpu.VMEM((HGRN_BAND + CHUNK, w), F32)],
        compiler_params=_params(("parallel", "arbitrary")),
        name="hgrn2",
    )(z, z, z, z, lb, ng)


def _unit_lower_inverses(l_mats):
    r = lax.broadcasted_iota(jnp.int32, (CHUNK, CHUNK), 0)
    c = lax.broadcasted_iota(jnp.int32, (CHUNK, CHUNK), 1)
    eye = (r == c).astype(F32)
    same16 = _div_pow2(r, 16) == _div_pow2(c, 16)
    same32 = _div_pow2(r, 32) == _div_pow2(c, 32)
    m = [jnp.where(same16, -l, 0.0) for l in l_mats]
    p = [eye + x for x in m]
    for _ in range(3):
        m = [_hdot(x, x) for x in m]
        p = [y + _hdot(y, x) for y, x in zip(p, m)]
    for off_diag in (same32 & jnp.logical_not(same16), jnp.logical_not(same32)):
        t = [_hdot(y, jnp.where(off_diag, l, 0.0)) for y, l in zip(p, l_mats)]
        p = [y - _hdot(x, y) for y, x in zip(p, t)]
    return p


def _gdn_kernel(xq_ref, xk_ref, xv_ref, gate_ref, zs_ref, cw_ref, alog_ref, dtb_ref, ng_ref,
                o_ref, s_ref, xbuf_ref):
    n = pl.program_id(1)
    w = EVEN_WIDTH

    @pl.when(n == 0)
    def _():
        s_ref[...] = jnp.zeros_like(s_ref)
        xbuf_ref[:, 0:8, :] = jnp.zeros((3, 8, w), F32)

    conv = []
    for a, x_ref in enumerate((xq_ref, xk_ref, xv_ref)):
        x = x_ref[...].astype(F32)
        xbuf_ref[a, 8:8 + CHUNK, :] = x
        y = jnp.zeros((CHUNK, w), F32)
        for j in range(CONV_K):
            off = 8 - (CONV_K - 1) + j
            y = y + xbuf_ref[a, off:off + CHUNK, :] * cw_ref[j:j + 1, a * w:(a + 1) * w]
        xbuf_ref[a, 0:8, :] = x[CHUNK - 8:, :]
        conv.append(_silu(y))
    cq, ck, cv = conv

    zs = zs_ref[...]
    log_decay = -jnp.exp(alog_ref[...]) * jax.nn.softplus(zs + dtb_ref[...])
    r64 = lax.broadcasted_iota(jnp.int32, (CHUNK, CHUNK), 0)
    c64 = lax.broadcasted_iota(jnp.int32, (CHUNK, CHUNK), 1)
    tril = r64 >= c64
    gcum = _hdot(tril.astype(F32), log_decay)
    gcum_t = gcum.T
    beta_all = _sigmoid(zs)

    heads = range(EVEN_HEADS)
    sls = [slice(hd * HEAD_DIM, (hd + 1) * HEAD_DIM) for hd in heads]
    l2 = lambda x: x * lax.rsqrt(jnp.sum(x * x, axis=-1, keepdims=True) + NORM_EPS)
    q = [l2(cq[:, sl]) * (HEAD_DIM ** -0.5) for sl in sls]
    k = [l2(ck[:, sl]) for sl in sls]
    v = [cv[:, sl] for sl in sls]
    g_col = [gcum[:, hd:hd + 1] for hd in heads]
    beta = [beta_all[:, EVEN_HEADS + hd:EVEN_HEADS + hd + 1] for hd in heads]
    decay = [jnp.exp(jnp.where(tril, g_col[hd] - gcum_t[hd:hd + 1, :], -jnp.inf)) for hd in heads]
    kb = [k[hd] * beta[hd] for hd in heads]
    l_mat = [jnp.where(r64 > c64, _bdot_nt(kb[hd], k[hd]) * decay[hd], 0.0) for hd in heads]
    qk = [_bdot_nt(q[hd], k[hd]) * decay[hd] for hd in heads]
    eg = [jnp.exp(g_col[hd]) for hd in heads]
    rhs = [jnp.concatenate([v[hd] * beta[hd], kb[hd] * eg[hd]], axis=-1) for hd in heads]
    inv = _unit_lower_inverses(l_mat)
    sol = [_hdot(inv[hd], rhs[hd]) for hd in heads]
    s = [s_ref[hd] for hd in heads]
    v_new = [sol[hd][:, :HEAD_DIM] - _bdot(sol[hd][:, HEAD_DIM:], s[hd]) for hd in heads]
    o = [_bdot(q[hd] * eg[hd], s[hd]) + _bdot(qk[hd], v_new[hd]) for hd in heads]
    for hd in heads:
        g_end = g_col[hd][CHUNK - 1:CHUNK, :]
        kd = k[hd] * jnp.exp(g_end - g_col[hd])
        s_ref[hd] = jnp.exp(g_end) * s[hd] + _bdot(kd.T, v_new[hd])
        o_ref[:, sls[hd]] = (_rms(o[hd], ng_ref[...])
                             * _silu(gate_ref[:, sls[hd]].astype(F32))).astype(o_ref.dtype)


def _gdn(z, zs, conv_w, alog, dtb, ng, batch, seq):
    nchunk = seq // CHUNK
    w = EVEN_WIDTH
    col = lambda c: pl.BlockSpec((CHUNK, w), lambda b, n, c=c: (b * nchunk + n, c))
    vec = lambda width: pl.BlockSpec((1, width), lambda b, n: (0, 0))
    return pl.pallas_call(
        _gdn_kernel,
        grid=(batch, nchunk),
        in_specs=[col(4), col(5), col(6), col(7),
                  pl.BlockSpec((CHUNK, LANES), lambda b, n: (b * nchunk + n, 0)),
                  pl.BlockSpec((CONV_K, 3 * w), lambda b, n: (0, 0)),
                  vec(LANES), vec(LANES), vec(HEAD_DIM)],
        out_specs=pl.BlockSpec((CHUNK, w), lambda b, n: (b * nchunk + n, 0)),
        out_shape=jax.ShapeDtypeStruct((batch * seq, w), BF16),
        scratch_shapes=[pltpu.VMEM((EVEN_HEADS, HEAD_DIM, HEAD_DIM), F32),
                        pltpu.VMEM((3, 8 + CHUNK, w), F32)],
        compiler_params=_params(("parallel", "arbitrary")),
        name="gated_delta",
    )(z, z, z, z, zs, conv_w, alog, dtb, ng)


def _compress_kernel(kc_ref, vc_ref, pos_ref, w1_ref, w2_ref, ko_ref, vo_ref, x_ref):
    nblk = ko_ref.shape[0]
    row = lax.broadcasted_iota(jnp.int32, (nblk, HEAD_DIM), 0)
    half = CMP_STRIDE * HEAD_DIM
    for a, (src_ref, out_ref) in enumerate(((kc_ref, ko_ref), (vc_ref, vo_ref))):
        x_ref[...] = src_ref[...].astype(F32)
        p1 = jnp.zeros((nblk, HEAD_DIM), F32)
        p2 = jnp.zeros((nblk, HEAD_DIM), F32)
        for l in range(CMP_STRIDE):
            rows = x_ref[pl.ds(l, nblk, stride=CMP_STRIDE), :]
            p1 = p1 + _bdot(rows + pos_ref[a, l:l + 1, :], w1_ref[a, l * HEAD_DIM:(l + 1) * HEAD_DIM, :])
            p2 = p2 + _bdot(rows + pos_ref[a, CMP_STRIDE + l:CMP_STRIDE + l + 1, :],
                            w1_ref[a, half + l * HEAD_DIM:half + (l + 1) * HEAD_DIM, :])
        pre = p1 + pltpu.roll(p2, nblk - 1, 0)
        out = _bdot(_silu(pre), w2_ref[a])
        out_ref[...] = jnp.where(row < nblk - 1, out, 0.0)


def _compress(z, pos, w1, w2, batch, seq):
    nblk = seq // CMP_STRIDE
    src = lambda base: pl.BlockSpec((None, seq, HEAD_DIM), lambda b, g, base=base: (b, 0, base + g))
    out = pl.BlockSpec((None, None, nblk, HEAD_DIM), lambda b, g: (b, g, 0, 0))
    full = lambda a: pl.BlockSpec(a.shape, lambda b, g: (0,) * a.ndim)
    shape = jax.ShapeDtypeStruct((batch, NSA_GROUPS, nblk, HEAD_DIM), F32)
    return pl.pallas_call(
        _compress_kernel,
        grid=(batch, NSA_GROUPS),
        in_specs=[src(16), src(20), full(pos), full(w1), full(w2)],
        out_specs=[out, out],
        out_shape=[shape, shape],
        scratch_shapes=[pltpu.VMEM((seq, HEAD_DIM), F32)],
        compiler_params=_params(("parallel", "parallel")),
        name="nsa_compress",
    )(z, z, pos, w1, w2)


def _nsa_kernel(q_ref, kc_ref, vc_ref, ks_ref, vs_ref, kw_ref, vw_ref, gl_ref, kaug_ref, o_ref,
                m_ref, l_ref, acc_ref, m2_ref, l2_ref, acc2_ref, *, seq):
    g = pl.program_id(1)
    qi = pl.program_id(2)
    tq = ATT_TILE
    rows = NSA_HPG * tq
    q0 = qi * tq
    ncmp = seq // CMP_STRIDE
    nblk = seq // SEL_BLOCK

    qh = jnp.concatenate([q_ref[:, hp * HEAD_DIM:(hp + 1) * HEAD_DIM] for hp in range(NSA_HPG)], axis=0)
    qb = (qh.astype(F32) * (HEAD_DIM ** -0.5)).astype(BF16)

    rrow = lax.broadcasted_iota(jnp.int32, (rows, LANES), 0)
    rlane = lax.broadcasted_iota(jnp.int32, (rows, LANES), 1)
    hp_idx = _div_pow2(rrow, tq)
    tpos = q0 + (rrow - hp_idx * tq)
    head = (g * NSA_HPG + hp_idx).astype(F32)
    slope = jnp.exp((-8.0 * np.log(2.0) / NSA_HEADS) * (head + 1.0))

    clane = lax.broadcasted_iota(jnp.int32, (rows, ncmp), 1)
    dist_c = tpos[:, :ncmp] - (clane * CMP_STRIDE + CMP_LEN - 1)
    valid_c = (dist_c >= 0) & (clane < ncmp - 1)
    s_c = _bdot_nt(qb, kc_ref[...])
    s_c = jnp.where(valid_c, s_c - slope[:, :ncmp] * dist_c.astype(F32), MASK_VALUE)
    m_c = jnp.max(s_c, axis=-1, keepdims=True)
    e_c = jnp.where(valid_c, jnp.exp(s_c - m_c), 0.0)
    den = jnp.sum(e_c, axis=-1, keepdims=True)
    p_c = e_c / jnp.where(den > 0.0, den, 1.0)
    o_cmp = _bdot(p_c, vc_ref[...])

    p_sum = p_c[0:tq] + p_c[tq:2 * tq] + p_c[2 * tq:3 * tq] + p_c[3 * tq:4 * tq]
    ob = lax.broadcasted_iota(jnp.int32, (KAUG_POS_LANE, ncmp), 0) * SEL_BLOCK
    oj = lax.broadcasted_iota(jnp.int32, (KAUG_POS_LANE, ncmp), 1) * CMP_STRIDE
    overlap_t = ((oj < ob + SEL_BLOCK) & (oj + CMP_LEN > ob)).astype(F32)
    imp_t = lax.dot_general(overlap_t, p_sum, (((1,), (1,)), ((), ())), precision=HIGHEST,
                            preferred_element_type=F32)
    blk = lax.broadcasted_iota(jnp.int32, (KAUG_POS_LANE, tq), 0)
    cur = _div_pow2(q0 + lax.broadcasted_iota(jnp.int32, (KAUG_POS_LANE, tq), 1), SEL_BLOCK)
    forced = (blk == 0) | (blk == cur) | (blk == cur - 1)
    score = jnp.where(forced, SEL_FORCE, jnp.where(blk > cur, -SEL_FORCE, imp_t))
    score = jnp.where(blk < nblk, score, -jnp.inf)
    ahead = jnp.zeros((KAUG_POS_LANE, tq), F32)
    for j in range(nblk):
        sj = score[j:j + 1, :]
        wins_tie = jnp.where(blk > j, 1.0, 0.0)
        ahead = ahead + jnp.where(sj > score, 1.0, jnp.where(sj == score, wins_tie, 0.0))
    sel_t = jnp.where(ahead < float(min(N_SEL, nblk)), 1.0, 0.0)
    sel_t = jnp.concatenate([sel_t, jnp.zeros((LANES - KAUG_POS_LANE, tq), F32)], axis=0)
    sel = sel_t.T
    s_hi = slope.astype(BF16).astype(F32)
    s_mid = (slope - s_hi).astype(BF16).astype(F32)
    s_lo = slope - s_hi - s_mid
    piece = jnp.where(rlane < KAUG_POS_LANE + 2, s_hi, jnp.where(rlane < KAUG_POS_LANE + 4, s_mid, s_lo))
    bias_cols = jnp.where((rlane >= KAUG_POS_LANE) & (rlane < KAUG_POS_LANE + 6), piece, 0.0)
    sel_cols = (jnp.concatenate([sel] * NSA_HPG, axis=0) - 1.0) * SEL_MASK_BIG
    q_sel = jnp.concatenate([qb, jnp.where(rlane < KAUG_POS_LANE, sel_cols, bias_cols).astype(BF16)], axis=1)
    q_win = jnp.concatenate([qb, bias_cols.astype(BF16)], axis=1)

    pair = KEY_TILE
    rrow2 = lax.broadcasted_iota(jnp.int32, (rows, pair), 0)
    dist0 = (q0 + rrow2 - _div_pow2(rrow2, tq) * tq) - lax.broadcasted_iota(jnp.int32, (rows, pair), 1)
    ones = jnp.ones((pair, LANES), BF16)
    for ref in (m_ref, m2_ref):
        ref[...] = jnp.full_like(ref, MASK_VALUE)
    for ref in (l_ref, acc_ref, l2_ref, acc2_ref):
        ref[...] = jnp.zeros_like(ref)
    sel_branch = (q_sel, ks_ref, vs_ref, m_ref, l_ref, acc_ref)
    win_branch = (q_win, kw_ref, vw_ref, m2_ref, l2_ref, acc2_ref)

    def attend_pair(j, branches, masks):
        k0 = pl.multiple_of(j * pair, pair)
        kaug = kaug_ref[pl.ds(k0, pair), :]
        s = []
        for (qa, k_ref, _, _, _, _), mask in zip(branches, masks):
            kp = jnp.concatenate([k_ref[pl.ds(k0, pair), :].astype(BF16), kaug], axis=1)
            sc = lax.dot_general(qa, kp, (((1,), (1,)), ((), ())), preferred_element_type=F32)
            s.append(sc if mask is None else jnp.where(mask(k0), sc, MASK_VALUE))
        m_prev = [br[3][...] for br in branches]
        m_new = [jnp.maximum(mp, jnp.max(jnp.maximum(sc[:, :LANES], sc[:, LANES:]), axis=-1, keepdims=True))
                 for mp, sc in zip(m_prev, s)]
        p = [jnp.concatenate([jnp.exp(sc[:, :LANES] - mn), jnp.exp(sc[:, LANES:] - mn)], axis=1).astype(BF16)
             for sc, mn in zip(s, m_new)]
        alpha = [jnp.exp(mp - mn) for mp, mn in zip(m_prev, m_new)]
        pv = [jnp.dot(pb, jnp.concatenate([br[2][pl.ds(k0, pair), :].astype(BF16), ones], axis=1),
                      preferred_element_type=F32) for br, pb in zip(branches, p)]
        for (_, _, _, mr, lr, ar), r, al, mn in zip(branches, pv, alpha, m_new):
            ar[...] = al * ar[...] + r[:, :LANES]
            lr[...] = al * lr[...] + r[:, LANES:]
            mr[...] = mn

    causal = lambda k0: dist0 >= k0
    in_window = lambda k0: dist0 < k0 + WINDOW
    npair = qi + 1
    win_pairs = WINDOW // pair + 1

    def far_body(j, carry):
        attend_pair(j, (sel_branch,), (None,))
        return carry

    lax.fori_loop(0, jnp.maximum(npair - win_pairs, 0), far_body, 0)
    for back in range(win_pairs, 0, -1):
        j = npair - back
        sel_mask = causal if back == 1 else None
        win_mask = causal if back == 1 else (in_window if back == win_pairs else None)

        @pl.when(j >= 0)
        def _(j=j, sel_mask=sel_mask, win_mask=win_mask):
            attend_pair(j, (sel_branch, win_branch), (sel_mask, win_mask))

    o_slc = acc_ref[...] / l_ref[...]
    o_win = acc2_ref[...] / l2_ref[...]

    ngate = NSA_HPG * 3
    erow = lax.broadcasted_iota(jnp.int32, (LANES, LANES), 0)
    elane = lax.broadcasted_iota(jnp.int32, (LANES, LANES), 1)
    pick_gate = (erow == g * ngate + elane).astype(F32)
    gates = _hdot(_sigmoid(gl_ref[...]), pick_gate)
    for hp in range(NSA_HPG):
        rs = slice(hp * tq, (hp + 1) * tq)
        o = (gates[:, 3 * hp:3 * hp + 1] * o_cmp[rs] + gates[:, 3 * hp + 1:3 * hp + 2] * o_slc[rs]
             + gates[:, 3 * hp + 2:3 * hp + 3] * o_win[rs])
        o_ref[:, hp * HEAD_DIM:(hp + 1) * HEAD_DIM] = o.astype(o_ref.dtype)


def _nsa_attention(z, zs, k_cmp, v_cmp, batch, seq):
    tq = ATT_TILE
    hq = NSA_HPG * HEAD_DIM
    nblk = seq // CMP_STRIDE
    assert ATT_TILE == KEY_TILE and seq % KEY_TILE == 0 and seq // SEL_BLOCK <= KAUG_POS_LANE
    kv = lambda base: pl.BlockSpec((None, seq, HEAD_DIM), lambda b, g, i, base=base: (b, 0, base + g))
    cmp_spec = pl.BlockSpec((None, None, nblk, HEAD_DIM), lambda b, g, i: (b, g, 0, 0))
    return pl.pallas_call(
        functools.partial(_nsa_kernel, seq=seq),
        grid=(batch, NSA_GROUPS, seq // tq),
        in_specs=[pl.BlockSpec((None, tq, hq), lambda b, g, i: (b, i, g)),
                  cmp_spec, cmp_spec, kv(24), kv(28), kv(32), kv(36),
                  pl.BlockSpec((None, tq, LANES), lambda b, g, i: (b, i, 0)),
                  pl.BlockSpec((seq, LANES), lambda b, g, i: (0, 0))],
        out_specs=pl.BlockSpec((None, tq, hq), lambda b, g, i: (b, i, g)),
        out_shape=jax.ShapeDtypeStruct((batch, seq, NSA_HEADS * HEAD_DIM), BF16),
        scratch_shapes=[pltpu.VMEM((NSA_HPG * tq, LANES), F32)] * 6,
        compiler_params=_params(("parallel", "parallel", "arbitrary")),
        name="nsa_attention",
    )(z, k_cmp, v_cmp, z, z, z, z, zs, _key_aug(seq))


def _key_aug(seq):
    pair = KEY_TILE
    kpos = np.arange(seq)
    aug = np.zeros((seq, LANES), np.float32)
    aug[kpos, kpos // SEL_BLOCK] = 1.0
    for c in range(3):
        aug[:, KAUG_POS_LANE + 2 * c] = kpos - kpos % pair
        aug[:, KAUG_POS_LANE + 2 * c + 1] = kpos % pair
    return jnp.asarray(aug, BF16)


def _split_w_in(w_in, n_main):
    tail = w_in[:, n_main:]
    tail = jnp.pad(tail, ((0, 0), (0, LANES - tail.shape[1])))
    return w_in[:, :n_main].astype(BF16), tail.astype(BF16)


def _pad_lanes(v, offset=0):
    out = jnp.zeros((1, LANES), F32)
    return out.at[0, offset:offset + v.shape[0]].set(v.astype(F32))


def _even_layer(h, g_in, g_out, w_in, lb, conv_w, a_log, dt_bias, hgrn_g, gdn_g, w_out, batch, seq):
    w_main, w_tail = _split_w_in(w_in, 8 * EVEN_WIDTH)
    z, zs = _proj_in(h, g_in, w_main, w_tail)
    o_a = _hgrn(z, lb.reshape(1, EVEN_WIDTH), hgrn_g.reshape(1, HEAD_DIM).astype(F32), batch, seq)
    o_b = _gdn(z, zs, conv_w.astype(F32), _pad_lanes(a_log), _pad_lanes(dt_bias),
               gdn_g.reshape(1, HEAD_DIM).astype(F32), batch, seq)
    o = jnp.concatenate([o_a, o_b], axis=-1)
    return _proj_out(o, w_out.astype(BF16), h, g_out)


def _odd_layer(h, g_in, g_out, w_in, cmp_pos, cmp_w1, cmp_w2, w_out, batch, seq):
    n_main = NSA_HEADS * HEAD_DIM + 6 * NSA_GROUPS * HEAD_DIM
    w_main, w_tail = _split_w_in(w_in, n_main)
    z, zs = _proj_in(h, g_in, w_main, w_tail)
    z3 = z.reshape(batch, seq, n_main)
    k_cmp, v_cmp = _compress(z3, cmp_pos.astype(F32), cmp_w1.astype(BF16), cmp_w2.astype(BF16), batch, seq)
    o = _nsa_attention(z3, zs.reshape(batch, seq, LANES), k_cmp, v_cmp, batch, seq)
    return _proj_out(o.reshape(batch * seq, -1), w_out.astype(BF16), h, g_out)


def kernel(x, norm_g, ab_w_in, hgrn_lb_logits, gdn_conv_w, gdn_a_log, gdn_dt_bias, hgrn_norm_g, gdn_norm_g,
           ab_w_out, nsa_w_in, nsa_cmp_pos, nsa_cmp_w1, nsa_cmp_w2, nsa_w_out, ffn_w_gu, ffn_w_down):
    batch, seq, d = x.shape
    depth = norm_g.shape[0]
    lb_all = jnp.cumsum(jax.nn.softmax(hgrn_lb_logits.astype(F32), axis=0), axis=0)
    lb_all = lb_all - lb_all[:1]
    g = norm_g.astype(F32).reshape(depth, 4, 1, d)
    h = x.reshape(batch * seq, d)
    for layer in range(depth):
        j = layer // 2
        if layer % 2 == 0:
            h = _even_layer(h, g[layer, 0], g[layer, 1], ab_w_in[j], lb_all[j], gdn_conv_w[j], gdn_a_log[j],
                            gdn_dt_bias[j], hgrn_norm_g[j], gdn_norm_g[j], ab_w_out[j], batch, seq)
        else:
            h = _odd_layer(h, g[layer, 0], g[layer, 1], nsa_w_in[j], nsa_cmp_pos[j], nsa_cmp_w1[j],
                           nsa_cmp_w2[j], nsa_w_out[j], batch, seq)
        h = _ffn(h, g[layer, 2], ffn_w_gu[layer].astype(BF16), ffn_w_down[layer].astype(BF16), g[layer, 3])
    return h.reshape(batch, seq, d)
```

```python
import functools

import numpy as np
import jax
import jax.numpy as jnp
from jax import lax
from jax.experimental import pallas as pl
from jax.experimental.pallas import tpu as pltpu

F32 = jnp.float32
BF16 = jnp.bfloat16
HIGHEST = lax.Precision.HIGHEST

NORM_EPS = 1e-6
MASK_VALUE = -1e30

LANES = 128
HEAD_DIM = 128
CHUNK = 64
HGRN_BAND = 16
CONV_K = 4
EVEN_HEADS = 8
EVEN_WIDTH = EVEN_HEADS * HEAD_DIM

NSA_GROUPS = 4
NSA_HPG = 4
NSA_HEADS = NSA_GROUPS * NSA_HPG
CMP_LEN = 32
CMP_STRIDE = 16
SEL_BLOCK = 64
N_SEL = 8
WINDOW = 512
SEL_FORCE = 1e4
ATT_TILE = 256
KEY_TILE = 256
KAUG_POS_LANE = 32
SEL_MASK_BIG = 2.0 ** 100
LOG2_E = float(np.log2(np.e))

VMEM_LIMIT = 56 * 1024 * 1024


def _params(semantics):
    return pltpu.CompilerParams(dimension_semantics=semantics, vmem_limit_bytes=VMEM_LIMIT)


def _rms(x, g):
    return x * lax.rsqrt(jnp.mean(x * x, axis=-1, keepdims=True) + NORM_EPS) * g


def _sigmoid(x):
    return 1.0 / (1.0 + jnp.exp(-x))


def _silu(x):
    return x * _sigmoid(x)


def _bdot(a, b):
    return jnp.dot(a.astype(BF16), b.astype(BF16), preferred_element_type=F32)


def _bdot_nt(a, b):
    return lax.dot_general(a.astype(BF16), b.astype(BF16), (((1,), (1,)), ((), ())),
                           preferred_element_type=F32)


def _hdot(a, b):
    return jnp.dot(a, b, precision=HIGHEST, preferred_element_type=F32)


def _split(x):
    hi = x.astype(BF16)
    return hi, (x - hi.astype(F32)).astype(BF16)


def _dot3(a, b):
    (a_hi, a_lo), (b_hi, b_lo) = a, b
    dot = lambda x, y: jnp.dot(x, y, preferred_element_type=F32)
    return dot(a_hi, b_hi) + (dot(a_lo, b_hi) + dot(a_hi, b_lo))


def _div_pow2(x, n):
    return jnp.right_shift(x, int(n).bit_length() - 1)


def _proj_in_kernel(h_ref, g_ref, w_ref, ws_ref, z_ref, zs_ref, xn_ref):
    @pl.when(pl.program_id(1) == 0)
    def _():
        xn = _rms(h_ref[...], g_ref[...]).astype(BF16)
        xn_ref[...] = xn
        zs_ref[...] = jnp.dot(xn, ws_ref[...], preferred_element_type=F32)

    z_ref[...] = jnp.dot(xn_ref[...], w_ref[...], preferred_element_type=F32).astype(z_ref.dtype)


def _proj_in(h, g, w, layer, ws, n, tm=512, tn=1024):
    m, d = h.shape
    assert m % tm == 0 and n % tn == 0
    return pl.pallas_call(
        _proj_in_kernel,
        grid=(m // tm, n // tn),
        in_specs=[
            pl.BlockSpec((tm, d), lambda i, j: (i, 0)),
            pl.BlockSpec((1, d), lambda i, j: (0, 0)),
            pl.BlockSpec((None, d, tn), lambda i, j: (layer, 0, j)),
            pl.BlockSpec((d, LANES), lambda i, j: (0, 0)),
        ],
        out_specs=[
            pl.BlockSpec((tm, tn), lambda i, j: (i, j)),
            pl.BlockSpec((tm, LANES), lambda i, j: (i, 0)),
        ],
        out_shape=[jax.ShapeDtypeStruct((m, n), BF16), jax.ShapeDtypeStruct((m, LANES), F32)],
        scratch_shapes=[pltpu.VMEM((tm, d), BF16)],
        compiler_params=_params(("parallel", "arbitrary")),
        name="proj_in",
    )(h, g, w, ws)


def _proj_out_kernel(o_ref, w_ref, h_ref, g_ref, out_ref):
    m = jnp.dot(o_ref[...], w_ref[...], preferred_element_type=F32)
    out_ref[...] = h_ref[...] + _rms(m, g_ref[...])


def _proj_out(o, w, layer, h, g, tm=512):
    m, k = o.shape
    d = w.shape[2]
    return pl.pallas_call(
        _proj_out_kernel,
        grid=(m // tm,),
        in_specs=[
            pl.BlockSpec((tm, k), lambda i: (i, 0)),
            pl.BlockSpec((None, k, d), lambda i: (layer, 0, 0)),
            pl.BlockSpec((tm, d), lambda i: (i, 0)),
            pl.BlockSpec((1, d), lambda i: (0, 0)),
        ],
        out_specs=pl.BlockSpec((tm, d), lambda i: (i, 0)),
        out_shape=jax.ShapeDtypeStruct((m, d), F32),
        compiler_params=_params(("parallel",)),
        name="proj_out",
    )(o, w, h, g)


def _ffn_kernel(h_ref, gin_ref, wg_ref, wu_ref, wd_ref, gout_ref, out_ref, xn_ref, acc_ref):
    f = pl.program_id(1)

    @pl.when(f == 0)
    def _():
        xn_ref[...] = _rms(h_ref[...], gin_ref[...]).astype(BF16)
        acc_ref[...] = jnp.zeros_like(acc_ref)

    xn = xn_ref[...]
    gate = jnp.dot(xn, wg_ref[...], preferred_element_type=F32)
    up = jnp.dot(xn, wu_ref[...], preferred_element_type=F32)
    act = (_silu(gate) * up).astype(BF16)
    acc_ref[...] += jnp.dot(act, wd_ref[...], preferred_element_type=F32)

    @pl.when(f == pl.num_programs(1) - 1)
    def _():
        out_ref[...] = h_ref[...] + _rms(acc_ref[...], gout_ref[...])


def _ffn(h, g_in, w_gu, w_down, layer, g_out, tm=512, tf=512):
    m, d = h.shape
    d_ff = w_down.shape[1]
    nf = d_ff // tf
    return pl.pallas_call(
        _ffn_kernel,
        grid=(m // tm, nf),
        in_specs=[
            pl.BlockSpec((tm, d), lambda i, f: (i, 0)),
            pl.BlockSpec((1, d), lambda i, f: (0, 0)),
            pl.BlockSpec((None, d, tf), lambda i, f: (layer, 0, f)),
            pl.BlockSpec((None, d, tf), lambda i, f: (layer, 0, nf + f)),
            pl.BlockSpec((None, tf, d), lambda i, f: (layer, f, 0)),
            pl.BlockSpec((1, d), lambda i, f: (0, 0)),
        ],
        out_specs=pl.BlockSpec((tm, d), lambda i, f: (i, 0)),
        out_shape=jax.ShapeDtypeStruct((m, d), F32),
        scratch_shapes=[pltpu.VMEM((tm, d), BF16), pltpu.VMEM((tm, d), F32)],
        compiler_params=_params(("parallel", "arbitrary")),
        name="ffn",
    )(h, g_in, w_gu, w_gu, w_down, g_out)


def _hgrn_kernel(q_ref, f_ref, i_ref, gate_ref, lb_ref, ng_ref, o_ref, st_ref, bpad_ref, kpad_ref):
    band = HGRN_BAND
    w = EVEN_WIDTH

    @pl.when(pl.program_id(1) == 0)
    def _():
        st_ref[...] = jnp.zeros_like(st_ref)
        bpad_ref[0:band, :] = jnp.zeros((band, w), F32)
        kpad_ref[0:band, :] = jnp.zeros((band, w), F32)

    r64 = lax.broadcasted_iota(jnp.int32, (CHUNK, CHUNK), 0)
    c64 = lax.broadcasted_iota(jnp.int32, (CHUNK, CHUNK), 1)
    tril = (r64 >= c64).astype(F32)
    offset = (lax.broadcasted_iota(jnp.int32, (CHUNK, LANES), 0)
              - lax.broadcasted_iota(jnp.int32, (CHUNK, LANES), 1))
    ones = jnp.ones((LANES, LANES), BF16)

    lb = lb_ref[...]
    f_all = lb + (1.0 - lb) * _sigmoid(f_ref[...].astype(F32))
    b_all = _hdot(tril, jnp.log(f_all))
    bpad_ref[band:band + CHUNK, :] = b_all
    kpad_ref[band:band + CHUNK, :] = 1.0 - f_all

    for hd in range(EVEN_HEADS):
        sl = slice(hd * HEAD_DIM, (hd + 1) * HEAD_DIM)
        q = q_ref[:, sl].astype(F32)
        b = b_all[:, sl]
        k = kpad_ref[band:band + CHUNK, sl]
        v = i_ref[:, sl].astype(F32)

        a = jnp.zeros((CHUNK, LANES), F32)
        for d in range(band):
            bs = bpad_ref[band - d:band - d + CHUNK, sl]
            ks = kpad_ref[band - d:band - d + CHUNK, sl]
            e = jnp.exp(b - bs) * (q * ks)
            r = jnp.dot(e.astype(BF16), ones, preferred_element_type=F32)
            a = jnp.where(offset == d, r, a)

        far = [jnp.zeros((band, CHUNK), F32)]
        for blk in range(1, CHUNK // band):
            r0 = blk * band
            br = b[r0:r0 + 1, :]
            qi = q[r0:r0 + band, :] * jnp.exp(b[r0:r0 + band, :] - br)
            ki = k * jnp.exp(jnp.minimum(br - b, 0.0))
            far.append(_bdot_nt(qi, ki))
        a_far = jnp.where(r64 - c64 >= band, jnp.concatenate(far, axis=0), 0.0)

        st = st_ref[hd]
        o = _bdot(a[:, :CHUNK] + a_far, v) + _bdot_nt(q * jnp.exp(b), st)
        b_end = b[CHUNK - 1:CHUNK, :]
        kd = k * jnp.exp(b_end - b)
        st_ref[hd] = jnp.exp(b_end) * st + _bdot(v.T, kd)
        o_ref[:, sl] = (_rms(o, ng_ref[...]) * _silu(gate_ref[:, sl].astype(F32))).astype(o_ref.dtype)


def _unit_lower_inverses(l_mats):
    r = lax.broadcasted_iota(jnp.int32, (CHUNK, CHUNK), 0)
    c = lax.broadcasted_iota(jnp.int32, (CHUNK, CHUNK), 1)
    eye = (r == c).astype(F32)
    same16 = _div_pow2(r, 16) == _div_pow2(c, 16)
    same32 = _div_pow2(r, 32) == _div_pow2(c, 32)
    m = [_split(jnp.where(same16, -l, 0.0)) for l in l_mats]
    p = [eye + jnp.where(same16, -l, 0.0) for l in l_mats]
    for _ in range(3):
        m = [_split(_dot3(x, x)) for x in m]
        p = [y + _dot3(_split(y), x) for y, x in zip(p, m)]
    for off_diag in (same32 & jnp.logical_not(same16), jnp.logical_not(same32)):
        ps = [_split(y) for y in p]
        t = [_dot3(ys, _split(jnp.where(off_diag, l, 0.0))) for ys, l in zip(ps, l_mats)]
        p = [y - _dot3(_split(x), ys) for y, ys, x in zip(p, ps, t)]
    return p


def _gdn_kernel(xq_ref, xk_ref, xv_ref, gate_ref, zs_ref, cw_ref, alog_ref, dtb_ref, ng_ref,
                o_ref, s_ref, xbuf_ref):
    n = pl.program_id(1)
    w = EVEN_WIDTH

    @pl.when(n == 0)
    def _():
        s_ref[...] = jnp.zeros_like(s_ref)
        xbuf_ref[:, 0:8, :] = jnp.zeros((3, 8, w), F32)

    conv = []
    for a, x_ref in enumerate((xq_ref, xk_ref, xv_ref)):
        x = x_ref[...].astype(F32)
        xbuf_ref[a, 8:8 + CHUNK, :] = x
        y = jnp.zeros((CHUNK, w), F32)
        for j in range(CONV_K):
            off = 8 - (CONV_K - 1) + j
            y = y + xbuf_ref[a, off:off + CHUNK, :] * cw_ref[j:j + 1, a * w:(a + 1) * w]
        xbuf_ref[a, 0:8, :] = x[CHUNK - 8:, :]
        conv.append(_silu(y))
    cq, ck, cv = conv

    zs = zs_ref[...]
    log_decay = -jnp.exp(alog_ref[...]) * jax.nn.softplus(zs + dtb_ref[...])
    r64 = lax.broadcasted_iota(jnp.int32, (CHUNK, CHUNK), 0)
    c64 = lax.broadcasted_iota(jnp.int32, (CHUNK, CHUNK), 1)
    tril = r64 >= c64
    gcum = _hdot(tril.astype(F32), log_decay)
    gcum_t = gcum.T
    beta_all = _sigmoid(zs)

    heads = range(EVEN_HEADS)
    sls = [slice(hd * HEAD_DIM, (hd + 1) * HEAD_DIM) for hd in heads]
    l2 = lambda x: x * lax.rsqrt(jnp.sum(x * x, axis=-1, keepdims=True) + NORM_EPS)
    q = [l2(cq[:, sl]) * (HEAD_DIM ** -0.5) for sl in sls]
    k = [l2(ck[:, sl]) for sl in sls]
    v = [cv[:, sl] for sl in sls]
    g_col = [gcum[:, hd:hd + 1] for hd in heads]
    beta = [beta_all[:, EVEN_HEADS + hd:EVEN_HEADS + hd + 1] for hd in heads]
    decay = [jnp.exp(jnp.where(tril, g_col[hd] - gcum_t[hd:hd + 1, :], -jnp.inf)) for hd in heads]
    kb = [k[hd] * beta[hd] for hd in heads]
    l_mat = [jnp.where(r64 > c64, _bdot_nt(kb[hd], k[hd]) * decay[hd], 0.0) for hd in heads]
    qk = [_bdot_nt(q[hd], k[hd]) * decay[hd] for hd in heads]
    eg = [jnp.exp(g_col[hd]) for hd in heads]
    rhs = [jnp.concatenate([v[hd] * beta[hd], kb[hd] * eg[hd]], axis=-1) for hd in heads]
    inv = _unit_lower_inverses(l_mat)
    sol = [_dot3(_split(inv[hd]), _split(rhs[hd])) for hd in heads]
    s = [s_ref[hd] for hd in heads]
    v_new = [sol[hd][:, :HEAD_DIM] - _bdot(sol[hd][:, HEAD_DIM:], s[hd]) for hd in heads]
    o = [_bdot(q[hd] * eg[hd], s[hd]) + _bdot(qk[hd], v_new[hd]) for hd in heads]
    for hd in heads:
        g_end = g_col[hd][CHUNK - 1:CHUNK, :]
        kd = k[hd] * jnp.exp(g_end - g_col[hd])
        s_ref[hd] = jnp.exp(g_end) * s[hd] + _bdot(kd.T, v_new[hd])
        o_ref[:, sls[hd]] = (_rms(o[hd], ng_ref[...])
                             * _silu(gate_ref[:, sls[hd]].astype(F32))).astype(o_ref.dtype)


def _even_mixer_kernel(hq_ref, hf_ref, hi_ref, hgate_ref, xq_ref, xk_ref, xv_ref, xgate_ref, zs_ref,
                       lb_ref, hng_ref, cw_ref, alog_ref, dtb_ref, gng_ref, o_ref,
                       st_ref, bpad_ref, kpad_ref, s_ref, xbuf_ref):
    w = EVEN_WIDTH
    _hgrn_kernel(hq_ref, hf_ref, hi_ref, hgate_ref, lb_ref, hng_ref, o_ref.at[:, 0:w],
                 st_ref, bpad_ref, kpad_ref)
    _gdn_kernel(xq_ref, xk_ref, xv_ref, xgate_ref, zs_ref, cw_ref, alog_ref, dtb_ref, gng_ref,
                o_ref.at[:, w:2 * w], s_ref, xbuf_ref)


def _even_mixer(z, zs, lb, hgrn_g, conv_w, alog, dtb, gdn_g, batch, seq):
    nchunk = seq // CHUNK
    w = EVEN_WIDTH
    col = lambda c: pl.BlockSpec((CHUNK, w), lambda b, n, c=c: (b * nchunk + n, c))
    vec = lambda width: pl.BlockSpec((1, width), lambda b, n: (0, 0))
    state = pltpu.VMEM((EVEN_HEADS, HEAD_DIM, HEAD_DIM), F32)
    return pl.pallas_call(
        _even_mixer_kernel,
        grid=(batch, nchunk),
        in_specs=[col(c) for c in range(8)] + [
            pl.BlockSpec((CHUNK, LANES), lambda b, n: (b * nchunk + n, 0)),
            vec(w), vec(HEAD_DIM),
            pl.BlockSpec((CONV_K, 3 * w), lambda b, n: (0, 0)),
            vec(LANES), vec(LANES), vec(HEAD_DIM)],
        out_specs=pl.BlockSpec((CHUNK, 2 * w), lambda b, n: (b * nchunk + n, 0)),
        out_shape=jax.ShapeDtypeStruct((batch * seq, 2 * w), BF16),
        scratch_shapes=[state, pltpu.VMEM((HGRN_BAND + CHUNK, w), F32), pltpu.VMEM((HGRN_BAND + CHUNK, w), F32),
                        state, pltpu.VMEM((3, 8 + CHUNK, w), F32)],
        compiler_params=_params(("parallel", "arbitrary")),
        name="even_mixer",
    )(*([z] * 8), zs, lb, hgrn_g, conv_w, alog, dtb, gdn_g)


def _compress_kernel(kc_ref, vc_ref, pos_ref, w1_ref, w2_ref, ko_ref, vo_ref, x_ref):
    nblk = ko_ref.shape[0]
    row = lax.broadcasted_iota(jnp.int32, (nblk, HEAD_DIM), 0)
    half = CMP_STRIDE * HEAD_DIM
    for a, (src_ref, out_ref) in enumerate(((kc_ref, ko_ref), (vc_ref, vo_ref))):
        x_ref[...] = src_ref[...].astype(F32)
        p1 = jnp.zeros((nblk, HEAD_DIM), F32)
        p2 = jnp.zeros((nblk, HEAD_DIM), F32)
        for l in range(CMP_STRIDE):
            rows = x_ref[pl.ds(l, nblk, stride=CMP_STRIDE), :]
            p1 = p1 + _bdot(rows + pos_ref[a, l:l + 1, :], w1_ref[a, l * HEAD_DIM:(l + 1) * HEAD_DIM, :])
            p2 = p2 + _bdot(rows + pos_ref[a, CMP_STRIDE + l:CMP_STRIDE + l + 1, :],
                            w1_ref[a, half + l * HEAD_DIM:half + (l + 1) * HEAD_DIM, :])
        pre = p1 + pltpu.roll(p2, nblk - 1, 0)
        out = _bdot(_silu(pre), w2_ref[a])
        out_ref[...] = jnp.where(row < nblk - 1, out, 0.0)


def _compress(z, pos, w1, w2, batch, seq):
    nblk = seq // CMP_STRIDE
    src = lambda base: pl.BlockSpec((None, seq, HEAD_DIM), lambda b, g, base=base: (b, 0, base + g))
    out = pl.BlockSpec((None, None, nblk, HEAD_DIM), lambda b, g: (b, g, 0, 0))
    full = lambda a: pl.BlockSpec(a.shape, lambda b, g: (0,) * a.ndim)
    shape = jax.ShapeDtypeStruct((batch, NSA_GROUPS, nblk, HEAD_DIM), F32)
    return pl.pallas_call(
        _compress_kernel,
        grid=(batch, NSA_GROUPS),
        in_specs=[src(16), src(20), full(pos), full(w1), full(w2)],
        out_specs=[out, out],
        out_shape=[shape, shape],
        scratch_shapes=[pltpu.VMEM((seq, HEAD_DIM), F32)],
        compiler_params=_params(("parallel", "parallel")),
        name="nsa_compress",
    )(z, z, pos, w1, w2)


def _nsa_kernel(q_ref, kc_ref, vc_ref, ks_ref, vs_ref, kw_ref, vw_ref, gl_ref, kaug_ref, caug_ref, o_ref,
                m_ref, l_ref, acc_ref, m2_ref, l2_ref, acc2_ref, *, seq):
    g = pl.program_id(1)
    qi = pl.program_id(2)
    tq = ATT_TILE
    rows = NSA_HPG * tq
    q0 = qi * tq
    ncmp = seq // CMP_STRIDE
    nblk = seq // SEL_BLOCK

    qh = jnp.concatenate([q_ref[:, hp * HEAD_DIM:(hp + 1) * HEAD_DIM] for hp in range(NSA_HPG)], axis=0)
    qb = (qh.astype(F32) * (HEAD_DIM ** -0.5 * LOG2_E)).astype(BF16)

    rrow = lax.broadcasted_iota(jnp.int32, (rows, LANES), 0)
    rlane = lax.broadcasted_iota(jnp.int32, (rows, LANES), 1)
    hp_idx = _div_pow2(rrow, tq)
    tpos = q0 + (rrow - hp_idx * tq)
    head = (g * NSA_HPG + hp_idx).astype(F32)
    slope = jnp.exp((-8.0 * np.log(2.0) / NSA_HEADS) * (head + 1.0)) * LOG2_E

    s_hi = slope.astype(BF16).astype(F32)
    s_mid = (slope - s_hi).astype(BF16).astype(F32)
    s_lo = slope - s_hi - s_mid
    piece = jnp.where(rlane < KAUG_POS_LANE + 2, s_hi, jnp.where(rlane < KAUG_POS_LANE + 4, s_mid, s_lo))
    bias_cols = jnp.where((rlane >= KAUG_POS_LANE) & (rlane < KAUG_POS_LANE + 6), piece, 0.0)
    q_win = jnp.concatenate([qb, bias_cols.astype(BF16)], axis=1)

    cpos = lax.broadcasted_iota(jnp.int32, (rows, ncmp), 1) * CMP_STRIDE + (CMP_LEN - 1)
    valid_c = tpos[:, :ncmp] >= cpos
    kc_aug = jnp.concatenate([kc_ref[...].astype(BF16), caug_ref[...]], axis=1)
    s_c = lax.dot_general(q_win, kc_aug, (((1,), (1,)), ((), ())), preferred_element_type=F32)
    s_c = jnp.where(valid_c, s_c, MASK_VALUE)
    m_c = jnp.max(s_c, axis=-1, keepdims=True)
    e_c = jnp.where(valid_c, jnp.exp2(s_c - m_c), 0.0)
    den = jnp.sum(e_c, axis=-1, keepdims=True)
    p_c = e_c / jnp.where(den > 0.0, den, 1.0)
    o_cmp = _bdot(p_c, vc_ref[...])

    p_sum = p_c[0:tq] + p_c[tq:2 * tq] + p_c[2 * tq:3 * tq] + p_c[3 * tq:4 * tq]
    ob = lax.broadcasted_iota(jnp.int32, (KAUG_POS_LANE, ncmp), 0) * SEL_BLOCK
    oj = lax.broadcasted_iota(jnp.int32, (KAUG_POS_LANE, ncmp), 1) * CMP_STRIDE
    overlap_t = ((oj < ob + SEL_BLOCK) & (oj + CMP_LEN > ob)).astype(F32)
    imp_t = lax.dot_general(overlap_t, p_sum, (((1,), (1,)), ((), ())), precision=HIGHEST,
                            preferred_element_type=F32)
    blk = lax.broadcasted_iota(jnp.int32, (KAUG_POS_LANE, tq), 0)
    cur = _div_pow2(q0 + lax.broadcasted_iota(jnp.int32, (KAUG_POS_LANE, tq), 1), SEL_BLOCK)
    forced = (blk == 0) | (blk == cur) | (blk == cur - 1)
    score = jnp.where(forced, SEL_FORCE, jnp.where(blk > cur, -SEL_FORCE, imp_t))
    score = jnp.where(blk < nblk, score, -jnp.inf)
    ahead = jnp.zeros((KAUG_POS_LANE, tq), F32)
    for j in range(nblk):
        sj = score[j:j + 1, :]
        wins_tie = jnp.where(blk > j, 1.0, 0.0)
        ahead = ahead + jnp.where(sj > score, 1.0, jnp.where(sj == score, wins_tie, 0.0))
    sel_t = jnp.where(ahead < float(min(N_SEL, nblk)), 1.0, 0.0)
    sel_t = jnp.concatenate([sel_t, jnp.zeros((LANES - KAUG_POS_LANE, tq), F32)], axis=0)
    sel = sel_t.T
    sel_cols = (jnp.concatenate([sel] * NSA_HPG, axis=0) - 1.0) * SEL_MASK_BIG
    q_sel = jnp.concatenate([qb, jnp.where(rlane < KAUG_POS_LANE, sel_cols, bias_cols).astype(BF16)], axis=1)

    pair = KEY_TILE
    below_diag = (lax.broadcasted_iota(jnp.int32, (rows, pair), 1)
                  <= jnp.bitwise_and(lax.broadcasted_iota(jnp.int32, (rows, pair), 0), tq - 1))
    causal = lambda sc: jnp.where(below_diag, sc, MASK_VALUE)
    in_window = lambda sc: jnp.where(below_diag, MASK_VALUE, sc)
    ones = jnp.ones((pair, LANES), BF16)
    for ref in (m_ref, m2_ref):
        ref[...] = jnp.full_like(ref, MASK_VALUE)
    for ref in (l_ref, acc_ref, l2_ref, acc2_ref):
        ref[...] = jnp.zeros_like(ref)
    sel_branch = (q_sel, ks_ref, vs_ref, m_ref, l_ref, acc_ref)
    win_branch = (q_win, kw_ref, vw_ref, m2_ref, l2_ref, acc2_ref)

    def attend_pair(j, branches, masks):
        k0 = pl.multiple_of(j * pair, pair)
        kaug = kaug_ref[pl.ds(k0, pair), :]
        s = []
        for (qa, k_ref, _, _, _, _), mask in zip(branches, masks):
            kp = jnp.concatenate([k_ref[pl.ds(k0, pair), :].astype(BF16), kaug], axis=1)
            sc = lax.dot_general(qa, kp, (((1,), (1,)), ((), ())), preferred_element_type=F32)
            s.append(sc if mask is None else mask(sc))
        m_prev = [br[3][...] for br in branches]
        m_new = [jnp.maximum(mp, jnp.max(jnp.maximum(sc[:, :LANES], sc[:, LANES:]), axis=-1, keepdims=True))
                 for mp, sc in zip(m_prev, s)]
        p = [jnp.concatenate([jnp.exp2(sc[:, :LANES] - mn), jnp.exp2(sc[:, LANES:] - mn)], axis=1).astype(BF16)
             for sc, mn in zip(s, m_new)]
        alpha = [jnp.exp2(mp - mn) for mp, mn in zip(m_prev, m_new)]
        pv = [jnp.dot(pb, jnp.concatenate([br[2][pl.ds(k0, pair), :].astype(BF16), ones], axis=1),
                      preferred_element_type=F32) for br, pb in zip(branches, p)]
        for (_, _, _, mr, lr, ar), r, al, mn in zip(branches, pv, alpha, m_new):
            ar[...] = al * ar[...] + r[:, :LANES]
            lr[...] = al * lr[...] + r[:, LANES:]
            mr[...] = mn

    npair = qi + 1
    win_pairs = WINDOW // pair + 1

    def far_body(j, carry):
        attend_pair(j, (sel_branch,), (None,))
        return carry

    lax.fori_loop(0, jnp.maximum(npair - win_pairs, 0), far_body, 0)
    for back in range(win_pairs, 0, -1):
        j = npair - back
        sel_mask = causal if back == 1 else None
        win_mask = causal if back == 1 else (in_window if back == win_pairs else None)

        @pl.when(j >= 0)
        def _(j=j, sel_mask=sel_mask, win_mask=win_mask):
            attend_pair(j, (sel_branch, win_branch), (sel_mask, win_mask))

    o_slc = acc_ref[...] / l_ref[...]
    o_win = acc2_ref[...] / l2_ref[...]

    ngate = NSA_HPG * 3
    erow = lax.broadcasted_iota(jnp.int32, (LANES, LANES), 0)
    elane = lax.broadcasted_iota(jnp.int32, (LANES, LANES), 1)
    pick_gate = (erow == g * ngate + elane).astype(F32)
    gates = _hdot(_sigmoid(gl_ref[...]), pick_gate)
    for hp in range(NSA_HPG):
        rs = slice(hp * tq, (hp + 1) * tq)
        o = (gates[:, 3 * hp:3 * hp + 1] * o_cmp[rs] + gates[:, 3 * hp + 1:3 * hp + 2] * o_slc[rs]
             + gates[:, 3 * hp + 2:3 * hp + 3] * o_win[rs])
        o_ref[:, hp * HEAD_DIM:(hp + 1) * HEAD_DIM] = o.astype(o_ref.dtype)


def _nsa_attention(z, zs, k_cmp, v_cmp, batch, seq):
    tq = ATT_TILE
    hq = NSA_HPG * HEAD_DIM
    nblk = seq // CMP_STRIDE
    assert ATT_TILE == KEY_TILE and seq % KEY_TILE == 0 and seq // SEL_BLOCK <= KAUG_POS_LANE
    assert WINDOW % KEY_TILE == 0
    kv = lambda base: pl.BlockSpec((None, seq, HEAD_DIM), lambda b, g, i, base=base: (b, 0, base + g))
    cmp_spec = pl.BlockSpec((None, None, nblk, HEAD_DIM), lambda b, g, i: (b, g, 0, 0))
    return pl.pallas_call(
        functools.partial(_nsa_kernel, seq=seq),
        grid=(batch, NSA_GROUPS, seq // tq),
        in_specs=[pl.BlockSpec((None, tq, hq), lambda b, g, i: (b, i, g)),
                  cmp_spec, cmp_spec, kv(24), kv(28), kv(32), kv(36),
                  pl.BlockSpec((None, tq, LANES), lambda b, g, i: (b, i, 0)),
                  pl.BlockSpec((seq, LANES), lambda b, g, i: (0, 0)),
                  pl.BlockSpec((nblk, LANES), lambda b, g, i: (0, 0))],
        out_specs=pl.BlockSpec((None, tq, hq), lambda b, g, i: (b, i, g)),
        out_shape=jax.ShapeDtypeStruct((batch, seq, NSA_HEADS * HEAD_DIM), BF16),
        scratch_shapes=[pltpu.VMEM((NSA_HPG * tq, LANES), F32)] * 6,
        compiler_params=_params(("parallel", "parallel", "arbitrary")),
        name="nsa_attention",
    )(z, k_cmp, v_cmp, z, z, z, z, zs, _key_aug(np.arange(seq), True),
      _key_aug(np.arange(nblk) * CMP_STRIDE + CMP_LEN - 1, False))


def _key_aug(kpos, with_block):
    aug = np.zeros((len(kpos), LANES), np.float32)
    if with_block:
        aug[np.arange(len(kpos)), kpos // SEL_BLOCK] = 1.0
    for c in range(3):
        aug[:, KAUG_POS_LANE + 2 * c] = kpos - kpos % 256
        aug[:, KAUG_POS_LANE + 2 * c + 1] = kpos % 256
    return jnp.asarray(aug, BF16)


def _w_in_tail(w_in, n_main):
    tail = w_in[:, n_main:]
    return jnp.pad(tail, ((0, 0), (0, LANES - tail.shape[1]))).astype(BF16)


def _pad_lanes(v):
    return jnp.pad(v.astype(F32), (0, LANES - v.shape[0])).reshape(1, LANES)


def _even_layer(h, g_in, g_out, w_in_bf, w_in, j, lb, conv_w, a_log, dt_bias, hgrn_g, gdn_g, w_out_bf, batch, seq):
    n_main = 8 * EVEN_WIDTH
    z, zs = _proj_in(h, g_in, w_in_bf, j, _w_in_tail(w_in, n_main), n_main)
    o = _even_mixer(z, zs, lb.reshape(1, EVEN_WIDTH), hgrn_g.reshape(1, HEAD_DIM).astype(F32),
                    conv_w.astype(F32), _pad_lanes(a_log), _pad_lanes(dt_bias),
                    gdn_g.reshape(1, HEAD_DIM).astype(F32), batch, seq)
    return _proj_out(o, w_out_bf, j, h, g_out)


def _odd_layer(h, g_in, g_out, w_in_bf, w_in, j, cmp_pos, cmp_w1, cmp_w2, w_out_bf, batch, seq):
    n_main = NSA_HEADS * HEAD_DIM + 6 * NSA_GROUPS * HEAD_DIM
    z, zs = _proj_in(h, g_in, w_in_bf, j, _w_in_tail(w_in, n_main), n_main)
    z3 = z.reshape(batch, seq, n_main)
    k_cmp, v_cmp = _compress(z3, cmp_pos.astype(F32), cmp_w1.astype(BF16), cmp_w2.astype(BF16), batch, seq)
    o = _nsa_attention(z3, zs.reshape(batch, seq, LANES), k_cmp, v_cmp, batch, seq)
    return _proj_out(o.reshape(batch * seq, -1), w_out_bf, j, h, g_out)


def kernel(x, norm_g, ab_w_in, hgrn_lb_logits, gdn_conv_w, gdn_a_log, gdn_dt_bias, hgrn_norm_g, gdn_norm_g,
           ab_w_out, nsa_w_in, nsa_cmp_pos, nsa_cmp_w1, nsa_cmp_w2, nsa_w_out, ffn_w_gu, ffn_w_down):
    batch, seq, d = x.shape
    depth = norm_g.shape[0]
    lb_all = jnp.cumsum(jax.nn.softmax(hgrn_lb_logits.astype(F32), axis=0), axis=0)
    lb_all = lb_all - lb_all[:1]
    g = norm_g.astype(F32).reshape(depth, 4, 1, d)
    ab_w_in_bf, ab_w_out_bf = ab_w_in.astype(BF16), ab_w_out.astype(BF16)
    nsa_w_in_bf, nsa_w_out_bf = nsa_w_in.astype(BF16), nsa_w_out.astype(BF16)
    ffn_w_gu_bf, ffn_w_down_bf = ffn_w_gu.astype(BF16), ffn_w_down.astype(BF16)
    h = x.reshape(batch * seq, d)
    for layer in range(depth):
        j = layer // 2
        if layer % 2 == 0:
            h = _even_layer(h, g[layer, 0], g[layer, 1], ab_w_in_bf, ab_w_in[j], j, lb_all[j], gdn_conv_w[j],
                            gdn_a_log[j], gdn_dt_bias[j], hgrn_norm_g[j], gdn_norm_g[j], ab_w_out_bf, batch, seq)
        else:
            h = _odd_layer(h, g[layer, 0], g[layer, 1], nsa_w_in_bf, nsa_w_in[j], j, nsa_cmp_pos[j],
                           nsa_cmp_w1[j], nsa_cmp_w2[j], nsa_w_out_bf, batch, seq)
        h = _ffn(h, g[layer, 2], ffn_w_gu_bf, ffn_w_down_bf, layer, g[layer, 3])
    return h.reshape(batch, seq, d)
```

```python
import functools

import numpy as np
import jax
import jax.numpy as jnp
from jax import lax
from jax.experimental import pallas as pl
from jax.experimental.pallas import tpu as pltpu

F32 = jnp.float32
BF16 = jnp.bfloat16
HIGHEST = lax.Precision.HIGHEST

NORM_EPS = 1e-6
MASK_VALUE = -1e30

LANES = 128
HEAD_DIM = 128
CHUNK = 64
HGRN_BAND = 16
CONV_K = 4
EVEN_HEADS = 8
EVEN_WIDTH = EVEN_HEADS * HEAD_DIM

NSA_GROUPS = 4
NSA_HPG = 4
NSA_HEADS = NSA_GROUPS * NSA_HPG
CMP_LEN = 32
CMP_STRIDE = 16
SEL_BLOCK = 64
N_SEL = 8
WINDOW = 512
SEL_FORCE = 1e4
ATT_TILE = 256
KEY_TILE = 256
KAUG_POS_LANE = 32
SEL_MASK_BIG = 2.0 ** 100
LOG2_E = float(np.log2(np.e))

VMEM_LIMIT = 56 * 1024 * 1024


def _params(semantics):
    return pltpu.CompilerParams(dimension_semantics=semantics, vmem_limit_bytes=VMEM_LIMIT)


def _rms(x, g):
    return x * lax.rsqrt(jnp.mean(x * x, axis=-1, keepdims=True) + NORM_EPS) * g


def _sigmoid(x):
    return 1.0 / (1.0 + jnp.exp(-x))


def _silu(x):
    return x * _sigmoid(x)


def _bdot(a, b):
    return jnp.dot(a.astype(BF16), b.astype(BF16), preferred_element_type=F32)


def _bdot_nt(a, b):
    return lax.dot_general(a.astype(BF16), b.astype(BF16), (((1,), (1,)), ((), ())),
                           preferred_element_type=F32)


def _hdot(a, b):
    return jnp.dot(a, b, precision=HIGHEST, preferred_element_type=F32)


def _split(x):
    hi = x.astype(BF16)
    return hi, (x - hi.astype(F32)).astype(BF16)


def _dot3(a, b):
    (a_hi, a_lo), (b_hi, b_lo) = a, b
    dot = lambda x, y: jnp.dot(x, y, preferred_element_type=F32)
    return dot(a_hi, b_hi) + (dot(a_lo, b_hi) + dot(a_hi, b_lo))


def _div_pow2(x, n):
    return jnp.right_shift(x, int(n).bit_length() - 1)


def _proj_in_kernel(h_ref, g_ref, w_ref, ws_ref, cs_ref, z_ref, zs_ref, xn_ref):
    @pl.when(pl.program_id(1) == 0)
    def _():
        xn = _rms(h_ref[...], g_ref[...]).astype(BF16)
        xn_ref[...] = xn
        zs_ref[...] = jnp.dot(xn, ws_ref[...], preferred_element_type=F32)

    z = jnp.dot(xn_ref[...], w_ref[...].astype(BF16), preferred_element_type=F32)
    z_ref[...] = (z * cs_ref[...]).astype(z_ref.dtype)


def _proj_in(h, g, w, layer, ws, col_scale, tm=1024, tn=1024):
    m, d = h.shape
    n = col_scale.shape[1]
    assert m % tm == 0 and n % tn == 0
    return pl.pallas_call(
        _proj_in_kernel,
        grid=(m // tm, n // tn),
        in_specs=[
            pl.BlockSpec((tm, d), lambda i, j: (i, 0)),
            pl.BlockSpec((1, d), lambda i, j: (0, 0)),
            pl.BlockSpec((None, d, tn), lambda i, j: (layer, 0, j)),
            pl.BlockSpec((d, LANES), lambda i, j: (0, 0)),
            pl.BlockSpec((1, tn), lambda i, j: (0, j)),
        ],
        out_specs=[
            pl.BlockSpec((tm, tn), lambda i, j: (i, j)),
            pl.BlockSpec((tm, LANES), lambda i, j: (i, 0)),
        ],
        out_shape=[jax.ShapeDtypeStruct((m, n), BF16), jax.ShapeDtypeStruct((m, LANES), F32)],
        scratch_shapes=[pltpu.VMEM((tm, d), BF16)],
        compiler_params=_params(("parallel", "arbitrary")),
        name="proj_in",
    )(h, g, w, ws, col_scale)


def _proj_out_kernel(o_ref, w_ref, h_ref, g_ref, out_ref):
    m = jnp.dot(o_ref[...], w_ref[...], preferred_element_type=F32)
    out_ref[...] = h_ref[...] + _rms(m, g_ref[...])


def _proj_out(o, w, layer, h, g, tm=512):
    m, k = o.shape
    d = w.shape[2]
    return pl.pallas_call(
        _proj_out_kernel,
        grid=(m // tm,),
        in_specs=[
            pl.BlockSpec((tm, k), lambda i: (i, 0)),
            pl.BlockSpec((None, k, d), lambda i: (layer, 0, 0)),
            pl.BlockSpec((tm, d), lambda i: (i, 0)),
            pl.BlockSpec((1, d), lambda i: (0, 0)),
        ],
        out_specs=pl.BlockSpec((tm, d), lambda i: (i, 0)),
        out_shape=jax.ShapeDtypeStruct((m, d), F32),
        compiler_params=_params(("parallel",)),
        name="proj_out",
    )(o, w, h, g)


def _ffn_kernel(h_ref, gin_ref, wg_ref, wu_ref, wd_ref, gout_ref, out_ref, xn_ref):
    f = pl.program_id(1)

    @pl.when(f == 0)
    def _():
        xn_ref[...] = _rms(h_ref[...], gin_ref[...]).astype(BF16)
        out_ref[...] = jnp.zeros_like(out_ref)

    xn = xn_ref[...]
    gate = jnp.dot(xn, wg_ref[...].astype(BF16), preferred_element_type=F32)
    up = jnp.dot(xn, wu_ref[...].astype(BF16), preferred_element_type=F32)
    act = (_silu(gate) * up).astype(BF16)
    out_ref[...] += jnp.dot(act, wd_ref[...].astype(BF16), preferred_element_type=F32)

    @pl.when(f == pl.num_programs(1) - 1)
    def _():
        out_ref[...] = h_ref[...] + _rms(out_ref[...], gout_ref[...])


def _ffn(h, g_in, w_gu, w_down, layer, g_out, tm=1024, tf=256):
    m, d = h.shape
    d_ff = w_down.shape[1]
    nf = d_ff // tf
    assert m % tm == 0 and d_ff % tf == 0
    return pl.pallas_call(
        _ffn_kernel,
        grid=(m // tm, nf),
        in_specs=[
            pl.BlockSpec((tm, d), lambda i, f: (i, 0), pipeline_mode=pl.Buffered(1)),
            pl.BlockSpec((1, d), lambda i, f: (0, 0)),
            pl.BlockSpec((None, d, tf), lambda i, f: (layer, 0, f)),
            pl.BlockSpec((None, d, tf), lambda i, f: (layer, 0, nf + f)),
            pl.BlockSpec((None, tf, d), lambda i, f: (layer, f, 0)),
            pl.BlockSpec((1, d), lambda i, f: (0, 0)),
        ],
        out_specs=pl.BlockSpec((tm, d), lambda i, f: (i, 0)),
        out_shape=jax.ShapeDtypeStruct((m, d), F32),
        scratch_shapes=[pltpu.VMEM((tm, d), BF16)],
        compiler_params=_params(("parallel", "arbitrary")),
        name="ffn",
    )(h, g_in, w_gu, w_gu, w_down, g_out)


def _hgrn_kernel(q_ref, f_ref, i_ref, gate_ref, lb_ref, ng_ref, o_ref, st_ref, bpad_ref, kpad_ref):
    band = HGRN_BAND
    w = EVEN_WIDTH

    @pl.when(pl.program_id(1) == 0)
    def _():
        st_ref[...] = jnp.zeros_like(st_ref)
        bpad_ref[0:band, :] = jnp.zeros((band, w), F32)
        kpad_ref[0:band, :] = jnp.zeros((band, w), F32)

    r64 = lax.broadcasted_iota(jnp.int32, (CHUNK, CHUNK), 0)
    c64 = lax.broadcasted_iota(jnp.int32, (CHUNK, CHUNK), 1)
    tril = (r64 >= c64).astype(F32)
    offset = (lax.broadcasted_iota(jnp.int32, (CHUNK, LANES), 0)
              - lax.broadcasted_iota(jnp.int32, (CHUNK, LANES), 1))
    ones = jnp.ones((LANES, LANES), BF16)

    lb = lb_ref[...]
    f_all = lb + (1.0 - lb) * _sigmoid(f_ref[...].astype(F32))
    b_all = _hdot(tril, jnp.log(f_all) * LOG2_E)
    bpad_ref[band:band + CHUNK, :] = b_all
    kpad_ref[band:band + CHUNK, :] = 1.0 - f_all

    for hd in range(EVEN_HEADS):
        sl = slice(hd * HEAD_DIM, (hd + 1) * HEAD_DIM)
        q = q_ref[:, sl].astype(F32)
        b = b_all[:, sl]
        k = kpad_ref[band:band + CHUNK, sl]
        v = i_ref[:, sl].astype(F32)

        a = jnp.zeros((CHUNK, LANES), F32)
        b_ext = bpad_ref[:, sl]
        k_ext = kpad_ref[:, sl]
        for d in range(band):
            bs = b if d == 0 else pltpu.roll(b_ext, d, 0)[band:, :]
            ks = k if d == 0 else pltpu.roll(k_ext, d, 0)[band:, :]
            e = jnp.exp2(b - bs) * (q * ks)
            r = jnp.dot(e.astype(BF16), ones, preferred_element_type=F32)
            a = jnp.where(offset == d, r, a)

        far = [jnp.zeros((band, CHUNK), F32)]
        for blk in range(1, CHUNK // band):
            r0 = blk * band
            br = b[r0:r0 + 1, :]
            qi = q[r0:r0 + band, :] * jnp.exp2(b[r0:r0 + band, :] - br)
            ki = k * jnp.exp2(jnp.minimum(br - b, 0.0))
            far.append(_bdot_nt(qi, ki))
        a_far = jnp.where(r64 - c64 >= band, jnp.concatenate(far, axis=0), 0.0)

        st = st_ref[hd]
        o = _bdot(a[:, :CHUNK] + a_far, v) + _bdot_nt(q * jnp.exp2(b), st)
        b_end = b[CHUNK - 1:CHUNK, :]
        kd = k * jnp.exp2(b_end - b)
        st_ref[hd] = jnp.exp2(b_end) * st + _bdot(v.T, kd)
        o_ref[:, sl] = (_rms(o, ng_ref[...]) * _silu(gate_ref[:, sl].astype(F32))).astype(o_ref.dtype)


def _unit_lower_inverses(l_mats):
    r = lax.broadcasted_iota(jnp.int32, (CHUNK, CHUNK), 0)
    c = lax.broadcasted_iota(jnp.int32, (CHUNK, CHUNK), 1)
    eye = (r == c).astype(F32)
    same16 = _div_pow2(r, 16) == _div_pow2(c, 16)
    same32 = _div_pow2(r, 32) == _div_pow2(c, 32)
    m = [_split(jnp.where(same16, -l, 0.0)) for l in l_mats]
    p = [eye + jnp.where(same16, -l, 0.0) for l in l_mats]
    for _ in range(3):
        m = [_split(_dot3(x, x)) for x in m]
        p = [y + _dot3(_split(y), x) for y, x in zip(p, m)]
    for off_diag in (same32 & jnp.logical_not(same16), jnp.logical_not(same32)):
        ps = [_split(y) for y in p]
        t = [_dot3(ys, _split(jnp.where(off_diag, l, 0.0))) for ys, l in zip(ps, l_mats)]
        p = [y - _dot3(_split(x), ys) for y, ys, x in zip(p, ps, t)]
    return p


def _gdn_kernel(xq_ref, xk_ref, xv_ref, gate_ref, zs_ref, cw_ref, alog_ref, dtb_ref, ng_ref,
                o_ref, s_ref, xbuf_ref):
    n = pl.program_id(1)
    w = EVEN_WIDTH

    @pl.when(n == 0)
    def _():
        s_ref[...] = jnp.zeros_like(s_ref)
        xbuf_ref[:, 0:8, :] = jnp.zeros((3, 8, w), F32)

    conv = []
    for a, x_ref in enumerate((xq_ref, xk_ref, xv_ref)):
        x = x_ref[...].astype(F32)
        xbuf_ref[a, 8:8 + CHUNK, :] = x
        y = jnp.zeros((CHUNK, w), F32)
        for j in range(CONV_K):
            off = 8 - (CONV_K - 1) + j
            y = y + xbuf_ref[a, off:off + CHUNK, :] * cw_ref[j:j + 1, a * w:(a + 1) * w]
        xbuf_ref[a, 0:8, :] = x[CHUNK - 8:, :]
        conv.append(_silu(y))
    cq, ck, cv = conv

    zs = zs_ref[...]
    log_decay = -jnp.exp(alog_ref[...]) * jax.nn.softplus(zs + dtb_ref[...])
    r64 = lax.broadcasted_iota(jnp.int32, (CHUNK, CHUNK), 0)
    c64 = lax.broadcasted_iota(jnp.int32, (CHUNK, CHUNK), 1)
    tril = r64 >= c64
    gcum = _hdot(tril.astype(F32), log_decay)
    gcum_t = gcum.T
    beta_all = _sigmoid(zs)

    heads = range(EVEN_HEADS)
    sls = [slice(hd * HEAD_DIM, (hd + 1) * HEAD_DIM) for hd in heads]
    l2 = lambda x: x * lax.rsqrt(jnp.sum(x * x, axis=-1, keepdims=True) + NORM_EPS)
    q = [l2(cq[:, sl]) * (HEAD_DIM ** -0.5) for sl in sls]
    k = [l2(ck[:, sl]) for sl in sls]
    v = [cv[:, sl] for sl in sls]
    g_col = [gcum[:, hd:hd + 1] for hd in heads]
    beta = [beta_all[:, EVEN_HEADS + hd:EVEN_HEADS + hd + 1] for hd in heads]
    decay = [jnp.exp(jnp.where(tril, g_col[hd] - gcum_t[hd:hd + 1, :], -jnp.inf)) for hd in heads]
    kb = [k[hd] * beta[hd] for hd in heads]
    l_mat = [jnp.where(r64 > c64, _bdot_nt(kb[hd], k[hd]) * decay[hd], 0.0) for hd in heads]
    qk = [_bdot_nt(q[hd], k[hd]) * decay[hd] for hd in heads]
    eg = [jnp.exp(g_col[hd]) for hd in heads]
    rhs = [jnp.concatenate([v[hd] * beta[hd], kb[hd] * eg[hd]], axis=-1) for hd in heads]
    inv = _unit_lower_inverses(l_mat)
    sol = [_dot3(_split(inv[hd]), _split(rhs[hd])) for hd in heads]
    s = [s_ref[hd] for hd in heads]
    v_new = [sol[hd][:, :HEAD_DIM] - _bdot(sol[hd][:, HEAD_DIM:], s[hd]) for hd in heads]
    o = [_bdot(q[hd] * eg[hd], s[hd]) + _bdot(qk[hd], v_new[hd]) for hd in heads]
    for hd in heads:
        g_end = g_col[hd][CHUNK - 1:CHUNK, :]
        kd = k[hd] * jnp.exp(g_end - g_col[hd])
        s_ref[hd] = jnp.exp(g_end) * s[hd] + _bdot(kd.T, v_new[hd])
        o_ref[:, sls[hd]] = (_rms(o[hd], ng_ref[...])
                             * _silu(gate_ref[:, sls[hd]].astype(F32))).astype(o_ref.dtype)


def _even_mixer_kernel(hq_ref, hf_ref, hi_ref, hgate_ref, xq_ref, xk_ref, xv_ref, xgate_ref, zs_ref,
                       lb_ref, hng_ref, cw_ref, alog_ref, dtb_ref, gng_ref, o_ref,
                       st_ref, bpad_ref, kpad_ref, s_ref, xbuf_ref):
    w = EVEN_WIDTH
    _hgrn_kernel(hq_ref, hf_ref, hi_ref, hgate_ref, lb_ref, hng_ref, o_ref.at[:, 0:w],
                 st_ref, bpad_ref, kpad_ref)
    _gdn_kernel(xq_ref, xk_ref, xv_ref, xgate_ref, zs_ref, cw_ref, alog_ref, dtb_ref, gng_ref,
                o_ref.at[:, w:2 * w], s_ref, xbuf_ref)


def _even_mixer(z, zs, lb, hgrn_g, conv_w, alog, dtb, gdn_g, batch, seq):
    nchunk = seq // CHUNK
    w = EVEN_WIDTH
    col = lambda c: pl.BlockSpec((CHUNK, w), lambda b, n, c=c: (b * nchunk + n, c))
    vec = lambda width: pl.BlockSpec((1, width), lambda b, n: (0, 0))
    state = pltpu.VMEM((EVEN_HEADS, HEAD_DIM, HEAD_DIM), F32)
    return pl.pallas_call(
        _even_mixer_kernel,
        grid=(batch, nchunk),
        in_specs=[col(c) for c in range(8)] + [
            pl.BlockSpec((CHUNK, LANES), lambda b, n: (b * nchunk + n, 0)),
            vec(w), vec(HEAD_DIM),
            pl.BlockSpec((CONV_K, 3 * w), lambda b, n: (0, 0)),
            vec(LANES), vec(LANES), vec(HEAD_DIM)],
        out_specs=pl.BlockSpec((CHUNK, 2 * w), lambda b, n: (b * nchunk + n, 0)),
        out_shape=jax.ShapeDtypeStruct((batch * seq, 2 * w), BF16),
        scratch_shapes=[state, pltpu.VMEM((HGRN_BAND + CHUNK, w), F32), pltpu.VMEM((HGRN_BAND + CHUNK, w), F32),
                        state, pltpu.VMEM((3, 8 + CHUNK, w), F32)],
        compiler_params=_params(("parallel", "arbitrary")),
        name="even_mixer",
    )(*([z] * 8), zs, lb, hgrn_g, conv_w, alog, dtb, gdn_g)


def _compress_kernel(kc_ref, vc_ref, pos_ref, w1_ref, w2_ref, ko_ref, vo_ref, x_ref):
    nblk = ko_ref.shape[0]
    row = lax.broadcasted_iota(jnp.int32, (nblk, HEAD_DIM), 0)
    half = CMP_STRIDE * HEAD_DIM
    for a, (src_ref, out_ref) in enumerate(((kc_ref, ko_ref), (vc_ref, vo_ref))):
        x_ref[...] = src_ref[...].astype(F32)
        p1 = jnp.zeros((nblk, HEAD_DIM), F32)
        p2 = jnp.zeros((nblk, HEAD_DIM), F32)
        for l in range(CMP_STRIDE):
            rows = x_ref[pl.ds(l, nblk, stride=CMP_STRIDE), :]
            p1 = p1 + _bdot(rows + pos_ref[a, l:l + 1, :], w1_ref[a, l * HEAD_DIM:(l + 1) * HEAD_DIM, :])
            p2 = p2 + _bdot(rows + pos_ref[a, CMP_STRIDE + l:CMP_STRIDE + l + 1, :],
                            w1_ref[a, half + l * HEAD_DIM:half + (l + 1) * HEAD_DIM, :])
        pre = p1 + pltpu.roll(p2, nblk - 1, 0)
        out = _bdot(_silu(pre), w2_ref[a])
        out_ref[...] = jnp.where(row < nblk - 1, out, 0.0)


def _compress(z, pos, w1, w2, batch, seq):
    nblk = seq // CMP_STRIDE
    src = lambda base: pl.BlockSpec((None, seq, HEAD_DIM), lambda b, g, base=base: (b, 0, base + g))
    out = pl.BlockSpec((None, None, nblk, HEAD_DIM), lambda b, g: (b, g, 0, 0))
    full = lambda a: pl.BlockSpec(a.shape, lambda b, g: (0,) * a.ndim)
    shape = jax.ShapeDtypeStruct((batch, NSA_GROUPS, nblk, HEAD_DIM), F32)
    return pl.pallas_call(
        _compress_kernel,
        grid=(batch, NSA_GROUPS),
        in_specs=[src(16), src(20), full(pos), full(w1), full(w2)],
        out_specs=[out, out],
        out_shape=[shape, shape],
        scratch_shapes=[pltpu.VMEM((seq, HEAD_DIM), F32)],
        compiler_params=_params(("parallel", "parallel")),
        name="nsa_compress",
    )(z, z, pos, w1, w2)


def _nsa_kernel(q_ref, kc_ref, vc_ref, ks_ref, vs_ref, kw_ref, vw_ref, gl_ref, kaug_ref, caug_ref, o_ref,
                m_ref, l_ref, acc_ref, m2_ref, l2_ref, acc2_ref, *, seq):
    g = pl.program_id(1)
    qi = pl.program_id(2)
    tq = ATT_TILE
    rows = NSA_HPG * tq
    q0 = qi * tq
    ncmp = seq // CMP_STRIDE
    nblk = seq // SEL_BLOCK

    qb = jnp.concatenate([q_ref[:, hp * HEAD_DIM:(hp + 1) * HEAD_DIM] for hp in range(NSA_HPG)], axis=0)

    rrow = lax.broadcasted_iota(jnp.int32, (rows, LANES), 0)
    rlane = lax.broadcasted_iota(jnp.int32, (rows, LANES), 1)
    hp_idx = _div_pow2(rrow, tq)
    tpos = q0 + (rrow - hp_idx * tq)
    head = (g * NSA_HPG + hp_idx).astype(F32)
    slope = jnp.exp((-8.0 * np.log(2.0) / NSA_HEADS) * (head + 1.0)) * LOG2_E

    s_hi = slope.astype(BF16).astype(F32)
    s_mid = (slope - s_hi).astype(BF16).astype(F32)
    s_lo = slope - s_hi - s_mid
    piece = jnp.where(rlane < KAUG_POS_LANE + 2, s_hi, jnp.where(rlane < KAUG_POS_LANE + 4, s_mid, s_lo))
    bias_cols = jnp.where((rlane >= KAUG_POS_LANE) & (rlane < KAUG_POS_LANE + 6), piece, 0.0)
    q_win = jnp.concatenate([qb, bias_cols.astype(BF16)], axis=1)

    cpos = lax.broadcasted_iota(jnp.int32, (rows, ncmp), 1) * CMP_STRIDE + (CMP_LEN - 1)
    valid_c = tpos[:, :ncmp] >= cpos
    kc_aug = jnp.concatenate([kc_ref[...].astype(BF16), caug_ref[...]], axis=1)
    s_c = lax.dot_general(q_win, kc_aug, (((1,), (1,)), ((), ())), preferred_element_type=F32)
    s_c = jnp.where(valid_c, s_c, MASK_VALUE)
    m_c = jnp.max(s_c, axis=-1, keepdims=True)
    e_c = jnp.where(valid_c, jnp.exp2(s_c - m_c), 0.0)
    den = jnp.sum(e_c, axis=-1, keepdims=True)
    p_c = e_c / jnp.where(den > 0.0, den, 1.0)
    o_cmp = _bdot(p_c, vc_ref[...])

    p_sum = p_c[0:tq] + p_c[tq:2 * tq] + p_c[2 * tq:3 * tq] + p_c[3 * tq:4 * tq]
    ob = lax.broadcasted_iota(jnp.int32, (KAUG_POS_LANE, ncmp), 0) * SEL_BLOCK
    oj = lax.broadcasted_iota(jnp.int32, (KAUG_POS_LANE, ncmp), 1) * CMP_STRIDE
    overlap_t = ((oj < ob + SEL_BLOCK) & (oj + CMP_LEN > ob)).astype(F32)
    imp_t = lax.dot_general(overlap_t, p_sum, (((1,), (1,)), ((), ())), precision=HIGHEST,
                            preferred_element_type=F32)
    blk = lax.broadcasted_iota(jnp.int32, (KAUG_POS_LANE, tq), 0)
    cur = _div_pow2(q0 + lax.broadcasted_iota(jnp.int32, (KAUG_POS_LANE, tq), 1), SEL_BLOCK)
    forced = (blk == 0) | (blk == cur) | (blk == cur - 1)
    score = jnp.where(forced, SEL_FORCE, jnp.where(blk > cur, -SEL_FORCE, imp_t))
    score = jnp.where(blk < nblk, score, -jnp.inf)
    ahead = jnp.zeros((KAUG_POS_LANE, tq), F32)
    for j in range(nblk):
        sj = score[j:j + 1, :]
        wins_tie = jnp.where(blk > j, 1.0, 0.0)
        ahead = ahead + jnp.where(sj > score, 1.0, jnp.where(sj == score, wins_tie, 0.0))
    sel_t = jnp.where(ahead < float(min(N_SEL, nblk)), 1.0, 0.0)
    sel_t = jnp.concatenate([sel_t, jnp.zeros((LANES - KAUG_POS_LANE, tq), F32)], axis=0)
    sel = sel_t.T
    sel_cols = (jnp.concatenate([sel] * NSA_HPG, axis=0) - 1.0) * SEL_MASK_BIG
    q_sel = jnp.concatenate([qb, jnp.where(rlane < KAUG_POS_LANE, sel_cols, bias_cols).astype(BF16)], axis=1)

    pair = KEY_TILE
    below_diag = (lax.broadcasted_iota(jnp.int32, (rows, pair), 1)
                  <= jnp.bitwise_and(lax.broadcasted_iota(jnp.int32, (rows, pair), 0), tq - 1))
    causal = lambda sc: jnp.where(below_diag, sc, MASK_VALUE)
    in_window = lambda sc: jnp.where(below_diag, MASK_VALUE, sc)
    ones = jnp.ones((pair, LANES), BF16)
    for ref in (m_ref, m2_ref):
        ref[...] = jnp.full_like(ref, MASK_VALUE)
    for ref in (l_ref, acc_ref, l2_ref, acc2_ref):
        ref[...] = jnp.zeros_like(ref)
    sel_branch = (q_sel, ks_ref, vs_ref, m_ref, l_ref, acc_ref)
    win_branch = (q_win, kw_ref, vw_ref, m2_ref, l2_ref, acc2_ref)

    def attend_pair(j, branches, masks):
        k0 = pl.multiple_of(j * pair, pair)
        kaug = kaug_ref[pl.ds(k0, pair), :]
        s = []
        for (qa, k_ref, _, _, _, _), mask in zip(branches, masks):
            kp = jnp.concatenate([k_ref[pl.ds(k0, pair), :].astype(BF16), kaug], axis=1)
            sc = lax.dot_general(qa, kp, (((1,), (1,)), ((), ())), preferred_element_type=F32)
            s.append(sc if mask is None else mask(sc))
        m_prev = [br[3][...] for br in branches]
        m_new = [jnp.maximum(mp, jnp.max(jnp.maximum(sc[:, :LANES], sc[:, LANES:]), axis=-1, keepdims=True))
                 for mp, sc in zip(m_prev, s)]
        p = [jnp.concatenate([jnp.exp2(sc[:, :LANES] - mn), jnp.exp2(sc[:, LANES:] - mn)], axis=1).astype(BF16)
             for sc, mn in zip(s, m_new)]
        alpha = [jnp.exp2(mp - mn) for mp, mn in zip(m_prev, m_new)]
        pv = [jnp.dot(pb, jnp.concatenate([br[2][pl.ds(k0, pair), :].astype(BF16), ones], axis=1),
                      preferred_element_type=F32) for br, pb in zip(branches, p)]
        for (_, _, _, mr, lr, ar), r, al, mn in zip(branches, pv, alpha, m_new):
            ar[...] = al * ar[...] + r[:, :LANES]
            lr[...] = al * lr[...] + r[:, LANES:]
            mr[...] = mn

    npair = qi + 1
    win_pairs = WINDOW // pair + 1

    def far_body(j, carry):
        attend_pair(j, (sel_branch,), (None,))
        return carry

    lax.fori_loop(0, jnp.maximum(npair - win_pairs, 0), far_body, 0)
    for back in range(win_pairs, 0, -1):
        j = npair - back
        sel_mask = causal if back == 1 else None
        win_mask = causal if back == 1 else (in_window if back == win_pairs else None)

        @pl.when(j >= 0)
        def _(j=j, sel_mask=sel_mask, win_mask=win_mask):
            attend_pair(j, (sel_branch, win_branch), (sel_mask, win_mask))

    o_slc = acc_ref[...] / l_ref[...]
    o_win = acc2_ref[...] / l2_ref[...]

    ngate = NSA_HPG * 3
    erow = lax.broadcasted_iota(jnp.int32, (LANES, LANES), 0)
    elane = lax.broadcasted_iota(jnp.int32, (LANES, LANES), 1)
    pick_gate = (erow == g * ngate + elane).astype(F32)
    gates = _hdot(_sigmoid(gl_ref[...]), pick_gate)
    for hp in range(NSA_HPG):
        rs = slice(hp * tq, (hp + 1) * tq)
        o = (gates[:, 3 * hp:3 * hp + 1] * o_cmp[rs] + gates[:, 3 * hp + 1:3 * hp + 2] * o_slc[rs]
             + gates[:, 3 * hp + 2:3 * hp + 3] * o_win[rs])
        o_ref[:, hp * HEAD_DIM:(hp + 1) * HEAD_DIM] = o.astype(o_ref.dtype)


def _nsa_attention(z, zs, k_cmp, v_cmp, batch, seq):
    tq = ATT_TILE
    hq = NSA_HPG * HEAD_DIM
    nblk = seq // CMP_STRIDE
    assert ATT_TILE == KEY_TILE and seq % KEY_TILE == 0 and seq // SEL_BLOCK <= KAUG_POS_LANE
    assert WINDOW % KEY_TILE == 0
    kv = lambda base: pl.BlockSpec((None, seq, HEAD_DIM), lambda b, g, i, base=base: (b, 0, base + g))
    cmp_spec = pl.BlockSpec((None, None, nblk, HEAD_DIM), lambda b, g, i: (b, g, 0, 0))
    return pl.pallas_call(
        functools.partial(_nsa_kernel, seq=seq),
        grid=(batch, NSA_GROUPS, seq // tq),
        in_specs=[pl.BlockSpec((None, tq, hq), lambda b, g, i: (b, i, g)),
                  cmp_spec, cmp_spec, kv(24), kv(28), kv(32), kv(36),
                  pl.BlockSpec((None, tq, LANES), lambda b, g, i: (b, i, 0)),
                  pl.BlockSpec((seq, LANES), lambda b, g, i: (0, 0)),
                  pl.BlockSpec((nblk, LANES), lambda b, g, i: (0, 0))],
        out_specs=pl.BlockSpec((None, tq, hq), lambda b, g, i: (b, i, g)),
        out_shape=jax.ShapeDtypeStruct((batch, seq, NSA_HEADS * HEAD_DIM), BF16),
        scratch_shapes=[pltpu.VMEM((NSA_HPG * tq, LANES), F32)] * 6,
        compiler_params=_params(("parallel", "parallel", "arbitrary")),
        name="nsa_attention",
    )(z, k_cmp, v_cmp, z, z, z, z, zs, _key_aug(np.arange(seq), True),
      _key_aug(np.arange(nblk) * CMP_STRIDE + CMP_LEN - 1, False))


def _key_aug(kpos, with_block):
    aug = np.zeros((len(kpos), LANES), np.float32)
    if with_block:
        aug[np.arange(len(kpos)), kpos // SEL_BLOCK] = 1.0
    for c in range(3):
        aug[:, KAUG_POS_LANE + 2 * c] = kpos - kpos % 256
        aug[:, KAUG_POS_LANE + 2 * c + 1] = kpos % 256
    return jnp.asarray(aug, BF16)


def _w_in_tail(w_in, n_main):
    tail = w_in[:, n_main:]
    return jnp.pad(tail, ((0, 0), (0, LANES - tail.shape[1]))).astype(BF16)


def _pad_lanes(v):
    return jnp.pad(v.astype(F32), (0, LANES - v.shape[0])).reshape(1, LANES)


def _even_layer(h, g_in, g_out, w_in, j, lb, conv_w, a_log, dt_bias, hgrn_g, gdn_g, w_out_bf, batch, seq):
    n_main = 8 * EVEN_WIDTH
    z, zs = _proj_in(h, g_in, w_in, j, _w_in_tail(w_in[j], n_main), jnp.ones((1, n_main), F32))
    o = _even_mixer(z, zs, lb.reshape(1, EVEN_WIDTH), hgrn_g.reshape(1, HEAD_DIM).astype(F32),
                    conv_w.astype(F32), _pad_lanes(a_log), _pad_lanes(dt_bias),
                    gdn_g.reshape(1, HEAD_DIM).astype(F32), batch, seq)
    return _proj_out(o, w_out_bf, j, h, g_out)


def _odd_layer(h, g_in, g_out, w_in, j, cmp_pos, cmp_w1, cmp_w2, w_out_bf, batch, seq):
    n_q = NSA_HEADS * HEAD_DIM
    n_main = n_q + 6 * NSA_GROUPS * HEAD_DIM
    col_scale = jnp.where(jnp.arange(n_main) < n_q, HEAD_DIM ** -0.5 * LOG2_E, 1.0).astype(F32).reshape(1, n_main)
    z, zs = _proj_in(h, g_in, w_in, j, _w_in_tail(w_in[j], n_main), col_scale)
    z3 = z.reshape(batch, seq, n_main)
    k_cmp, v_cmp = _compress(z3, cmp_pos.astype(F32), cmp_w1.astype(BF16), cmp_w2.astype(BF16), batch, seq)
    o = _nsa_attention(z3, zs.reshape(batch, seq, LANES), k_cmp, v_cmp, batch, seq)
    return _proj_out(o.reshape(batch * seq, -1), w_out_bf, j, h, g_out)


def kernel(x, norm_g, ab_w_in, hgrn_lb_logits, gdn_conv_w, gdn_a_log, gdn_dt_bias, hgrn_norm_g, gdn_norm_g,
           ab_w_out, nsa_w_in, nsa_cmp_pos, nsa_cmp_w1, nsa_cmp_w2, nsa_w_out, ffn_w_gu, ffn_w_down):
    batch, seq, d = x.shape
    depth = norm_g.shape[0]
    lb_all = jnp.cumsum(jax.nn.softmax(hgrn_lb_logits.astype(F32), axis=0), axis=0)
    lb_all = lb_all - lb_all[:1]
    g = norm_g.astype(F32).reshape(depth, 4, 1, d)
    ab_w_out_bf, nsa_w_out_bf = ab_w_out.astype(BF16), nsa_w_out.astype(BF16)
    h = x.reshape(batch * seq, d)
    for layer in range(depth):
        j = layer // 2
        if layer % 2 == 0:
            h = _even_layer(h, g[layer, 0], g[layer, 1], ab_w_in, j, lb_all[j], gdn_conv_w[j],
                            gdn_a_log[j], gdn_dt_bias[j], hgrn_norm_g[j], gdn_norm_g[j], ab_w_out_bf, batch, seq)
        else:
            h = _odd_layer(h, g[layer, 0], g[layer, 1], nsa_w_in, j, nsa_cmp_pos[j],
                           nsa_cmp_w1[j], nsa_cmp_w2[j], nsa_w_out_bf, batch, seq)
        h = _ffn(h, g[layer, 2], ffn_w_gu, ffn_w_down, layer, g[layer, 3])
    return h.reshape(batch, seq, d)
```

```python
import functools

import numpy as np
import jax
import jax.numpy as jnp
from jax import lax
from jax.experimental import pallas as pl
from jax.experimental.pallas import tpu as pltpu

F32 = jnp.float32
BF16 = jnp.bfloat16
HIGHEST = lax.Precision.HIGHEST

NORM_EPS = 1e-6
MASK_VALUE = -1e30

LANES = 128
HEAD_DIM = 128
CHUNK = 64
HGRN_BAND = 16
CONV_K = 4
EVEN_HEADS = 8
EVEN_WIDTH = EVEN_HEADS * HEAD_DIM

NSA_GROUPS = 4
NSA_HPG = 4
NSA_HEADS = NSA_GROUPS * NSA_HPG
CMP_LEN = 32
CMP_STRIDE = 16
SEL_BLOCK = 64
N_SEL = 8
WINDOW = 512
SEL_FORCE = 1e4
ATT_TILE = 256
KEY_TILE = 256
KAUG_POS_LANE = 32
SEL_MASK_BIG = 2.0 ** 100
LOG2_E = float(np.log2(np.e))

VMEM_LIMIT = 56 * 1024 * 1024


def _params(semantics):
    return pltpu.CompilerParams(dimension_semantics=semantics, vmem_limit_bytes=VMEM_LIMIT)


def _rms(x, g):
    return x * lax.rsqrt(jnp.mean(x * x, axis=-1, keepdims=True) + NORM_EPS) * g


def _sigmoid(x):
    return 1.0 / (1.0 + jnp.exp(-x))


def _silu(x):
    return x * _sigmoid(x)


def _bdot(a, b):
    return jnp.dot(a.astype(BF16), b.astype(BF16), preferred_element_type=F32)


def _bdot_nt(a, b):
    return lax.dot_general(a.astype(BF16), b.astype(BF16), (((1,), (1,)), ((), ())),
                           preferred_element_type=F32)


def _hdot(a, b):
    return jnp.dot(a, b, precision=HIGHEST, preferred_element_type=F32)


def _split(x):
    hi = x.astype(BF16)
    return hi, (x - hi.astype(F32)).astype(BF16)


def _dot3(a, b):
    (a_hi, a_lo), (b_hi, b_lo) = a, b
    dot = lambda x, y: jnp.dot(x, y, preferred_element_type=F32)
    return dot(a_hi, b_hi) + (dot(a_lo, b_hi) + dot(a_hi, b_lo))


def _div_pow2(x, n):
    return jnp.right_shift(x, int(n).bit_length() - 1)


def _proj_in_kernel(h_ref, g_ref, w_ref, ws_ref, cs_ref, z_ref, zs_ref, xn_ref):
    @pl.when(pl.program_id(1) == 0)
    def _():
        xn = _rms(h_ref[...], g_ref[...]).astype(BF16)
        xn_ref[...] = xn
        zs_ref[...] = _bdot_nt(xn, ws_ref[...])

    z = _bdot_nt(xn_ref[...], w_ref[...])
    z_ref[...] = (z * cs_ref[...]).astype(z_ref.dtype)


def _proj_in(h, g, w_t, layer, ws_t, col_scale, tm=1024, tn=1024):
    m, d = h.shape
    n = col_scale.shape[1]
    assert m % tm == 0 and n % tn == 0
    return pl.pallas_call(
        _proj_in_kernel,
        grid=(m // tm, n // tn),
        in_specs=[
            pl.BlockSpec((tm, d), lambda i, j: (i, 0)),
            pl.BlockSpec((1, d), lambda i, j: (0, 0)),
            pl.BlockSpec((None, tn, d), lambda i, j: (layer, j, 0)),
            pl.BlockSpec((LANES, d), lambda i, j: (0, 0)),
            pl.BlockSpec((1, tn), lambda i, j: (0, j)),
        ],
        out_specs=[
            pl.BlockSpec((tm, tn), lambda i, j: (i, j)),
            pl.BlockSpec((tm, LANES), lambda i, j: (i, 0)),
        ],
        out_shape=[jax.ShapeDtypeStruct((m, n), BF16), jax.ShapeDtypeStruct((m, LANES), F32)],
        scratch_shapes=[pltpu.VMEM((tm, d), BF16)],
        compiler_params=_params(("parallel", "arbitrary")),
        name="proj_in",
    )(h, g, w_t, ws_t, col_scale)


def _proj_out_kernel(o_ref, w_ref, h_ref, g_ref, out_ref):
    m = jnp.dot(o_ref[...], w_ref[...], preferred_element_type=F32)
    out_ref[...] = h_ref[...] + _rms(m, g_ref[...])


def _proj_out(o, w, layer, h, g, tm=512):
    m, k = o.shape
    d = w.shape[2]
    return pl.pallas_call(
        _proj_out_kernel,
        grid=(m // tm,),
        in_specs=[
            pl.BlockSpec((tm, k), lambda i: (i, 0)),
            pl.BlockSpec((None, k, d), lambda i: (layer, 0, 0)),
            pl.BlockSpec((tm, d), lambda i: (i, 0)),
            pl.BlockSpec((1, d), lambda i: (0, 0)),
        ],
        out_specs=pl.BlockSpec((tm, d), lambda i: (i, 0)),
        out_shape=jax.ShapeDtypeStruct((m, d), F32),
        compiler_params=_params(("parallel",)),
        name="proj_out",
    )(o, w, h, g)


def _ffn_kernel(h_ref, gin_ref, wg_ref, wu_ref, wd_ref, gout_ref, out_ref, xn_ref):
    f = pl.program_id(1)

    @pl.when(f == 0)
    def _():
        xn_ref[...] = _rms(h_ref[...], gin_ref[...]).astype(BF16)
        out_ref[...] = jnp.zeros_like(out_ref)

    xn = xn_ref[...]
    gate = jnp.dot(xn, wg_ref[...].astype(BF16), preferred_element_type=F32)
    up = jnp.dot(xn, wu_ref[...].astype(BF16), preferred_element_type=F32)
    act = (_silu(gate) * up).astype(BF16)
    out_ref[...] += jnp.dot(act, wd_ref[...].astype(BF16), preferred_element_type=F32)

    @pl.when(f == pl.num_programs(1) - 1)
    def _():
        out_ref[...] = h_ref[...] + _rms(out_ref[...], gout_ref[...])


def _ffn(h, g_in, w_gu, w_down, layer, g_out, tm=1024, tf=256):
    m, d = h.shape
    d_ff = w_down.shape[1]
    nf = d_ff // tf
    assert m % tm == 0 and d_ff % tf == 0
    return pl.pallas_call(
        _ffn_kernel,
        grid=(m // tm, nf),
        in_specs=[
            pl.BlockSpec((tm, d), lambda i, f: (i, 0), pipeline_mode=pl.Buffered(1)),
            pl.BlockSpec((1, d), lambda i, f: (0, 0)),
            pl.BlockSpec((None, d, tf), lambda i, f: (layer, 0, f)),
            pl.BlockSpec((None, d, tf), lambda i, f: (layer, 0, nf + f)),
            pl.BlockSpec((None, tf, d), lambda i, f: (layer, f, 0)),
            pl.BlockSpec((1, d), lambda i, f: (0, 0)),
        ],
        out_specs=pl.BlockSpec((tm, d), lambda i, f: (i, 0)),
        out_shape=jax.ShapeDtypeStruct((m, d), F32),
        scratch_shapes=[pltpu.VMEM((tm, d), BF16)],
        compiler_params=_params(("parallel", "arbitrary")),
        name="ffn",
    )(h, g_in, w_gu, w_gu, w_down, g_out)


def _hgrn_kernel(q_ref, f_ref, i_ref, gate_ref, lb_ref, ng_ref, o_ref, st_ref, bpad_ref, kpad_ref):
    band = HGRN_BAND
    w = EVEN_WIDTH

    @pl.when(pl.program_id(1) == 0)
    def _():
        st_ref[...] = jnp.zeros_like(st_ref)
        bpad_ref[0:band, :] = jnp.zeros((band, w), F32)
        kpad_ref[0:band, :] = jnp.zeros((band, w), F32)

    r64 = lax.broadcasted_iota(jnp.int32, (CHUNK, CHUNK), 0)
    c64 = lax.broadcasted_iota(jnp.int32, (CHUNK, CHUNK), 1)
    tril = (r64 >= c64).astype(F32)
    offset = (lax.broadcasted_iota(jnp.int32, (CHUNK, LANES), 0)
              - lax.broadcasted_iota(jnp.int32, (CHUNK, LANES), 1))
    ones = jnp.ones((LANES, LANES), BF16)

    lb = lb_ref[...]
    f_all = lb + (1.0 - lb) * _sigmoid(f_ref[...].astype(F32))
    b_all = _hdot(tril, jnp.log(f_all) * LOG2_E)
    bpad_ref[band:band + CHUNK, :] = b_all
    kpad_ref[band:band + CHUNK, :] = 1.0 - f_all

    for hd in range(EVEN_HEADS):
        sl = slice(hd * HEAD_DIM, (hd + 1) * HEAD_DIM)
        q = q_ref[:, sl].astype(F32)
        b = b_all[:, sl]
        k = kpad_ref[band:band + CHUNK, sl]
        v = i_ref[:, sl].astype(F32)

        a = jnp.zeros((CHUNK, LANES), F32)
        b_ext = bpad_ref[:, sl]
        k_ext = kpad_ref[:, sl]
        for d in range(band):
            bs = b if d == 0 else pltpu.roll(b_ext, d, 0)[band:, :]
            ks = k if d == 0 else pltpu.roll(k_ext, d, 0)[band:, :]
            e = jnp.exp2(b - bs) * (q * ks)
            r = jnp.dot(e.astype(BF16), ones, preferred_element_type=F32)
            a = jnp.where(offset == d, r, a)

        far = [jnp.zeros((band, CHUNK), F32)]
        for blk in range(1, CHUNK // band):
            r0 = blk * band
            br = b[r0:r0 + 1, :]
            qi = q[r0:r0 + band, :] * jnp.exp2(b[r0:r0 + band, :] - br)
            ki = k * jnp.exp2(jnp.minimum(br - b, 0.0))
            far.append(_bdot_nt(qi, ki))
        a_far = jnp.where(r64 - c64 >= band, jnp.concatenate(far, axis=0), 0.0)

        st = st_ref[hd]
        o = _bdot(a[:, :CHUNK] + a_far, v) + _bdot_nt(q * jnp.exp2(b), st)
        b_end = b[CHUNK - 1:CHUNK, :]
        kd = k * jnp.exp2(b_end - b)
        st_ref[hd] = jnp.exp2(b_end) * st + _bdot(v.T, kd)
        o_ref[:, sl] = (_rms(o, ng_ref[...]) * _silu(gate_ref[:, sl].astype(F32))).astype(o_ref.dtype)


def _unit_lower_inverses(l_mats):
    r = lax.broadcasted_iota(jnp.int32, (CHUNK, CHUNK), 0)
    c = lax.broadcasted_iota(jnp.int32, (CHUNK, CHUNK), 1)
    eye = (r == c).astype(F32)
    same16 = _div_pow2(r, 16) == _div_pow2(c, 16)
    same32 = _div_pow2(r, 32) == _div_pow2(c, 32)
    m = [_split(jnp.where(same16, -l, 0.0)) for l in l_mats]
    p = [eye + jnp.where(same16, -l, 0.0) for l in l_mats]
    for _ in range(3):
        m = [_split(_dot3(x, x)) for x in m]
        p = [y + _dot3(_split(y), x) for y, x in zip(p, m)]
    for off_diag in (same32 & jnp.logical_not(same16), jnp.logical_not(same32)):
        ps = [_split(y) for y in p]
        t = [_dot3(ys, _split(jnp.where(off_diag, l, 0.0))) for ys, l in zip(ps, l_mats)]
        p = [y - _dot3(_split(x), ys) for y, ys, x in zip(p, ps, t)]
    return p


def _gdn_kernel(xq_ref, xk_ref, xv_ref, gate_ref, zs_ref, cw_ref, alog_ref, dtb_ref, ng_ref,
                o_ref, s_ref, xbuf_ref):
    n = pl.program_id(1)
    w = EVEN_WIDTH

    @pl.when(n == 0)
    def _():
        s_ref[...] = jnp.zeros_like(s_ref)
        xbuf_ref[:, 0:8, :] = jnp.zeros((3, 8, w), F32)

    conv = []
    for a, x_ref in enumerate((xq_ref, xk_ref, xv_ref)):
        x = x_ref[...].astype(F32)
        xbuf_ref[a, 8:8 + CHUNK, :] = x
        y = jnp.zeros((CHUNK, w), F32)
        for j in range(CONV_K):
            off = 8 - (CONV_K - 1) + j
            y = y + xbuf_ref[a, off:off + CHUNK, :] * cw_ref[j:j + 1, a * w:(a + 1) * w]
        xbuf_ref[a, 0:8, :] = x[CHUNK - 8:, :]
        conv.append(_silu(y))
    cq, ck, cv = conv

    zs = zs_ref[...]
    log_decay = -jnp.exp(alog_ref[...]) * jax.nn.softplus(zs + dtb_ref[...])
    r64 = lax.broadcasted_iota(jnp.int32, (CHUNK, CHUNK), 0)
    c64 = lax.broadcasted_iota(jnp.int32, (CHUNK, CHUNK), 1)
    tril = r64 >= c64
    gcum = _hdot(tril.astype(F32), log_decay)
    gcum_t = gcum.T
    beta_all = _sigmoid(zs)

    heads = range(EVEN_HEADS)
    sls = [slice(hd * HEAD_DIM, (hd + 1) * HEAD_DIM) for hd in heads]
    l2 = lambda x: x * lax.rsqrt(jnp.sum(x * x, axis=-1, keepdims=True) + NORM_EPS)
    q = [l2(cq[:, sl]) * (HEAD_DIM ** -0.5) for sl in sls]
    k = [l2(ck[:, sl]) for sl in sls]
    v = [cv[:, sl] for sl in sls]
    g_col = [gcum[:, hd:hd + 1] for hd in heads]
    beta = [beta_all[:, EVEN_HEADS + hd:EVEN_HEADS + hd + 1] for hd in heads]
    decay = [jnp.exp(jnp.where(tril, g_col[hd] - gcum_t[hd:hd + 1, :], -jnp.inf)) for hd in heads]
    kb = [k[hd] * beta[hd] for hd in heads]
    l_mat = [jnp.where(r64 > c64, _bdot_nt(kb[hd], k[hd]) * decay[hd], 0.0) for hd in heads]
    qk = [_bdot_nt(q[hd], k[hd]) * decay[hd] for hd in heads]
    eg = [jnp.exp(g_col[hd]) for hd in heads]
    rhs = [jnp.concatenate([v[hd] * beta[hd], kb[hd] * eg[hd]], axis=-1) for hd in heads]
    inv = _unit_lower_inverses(l_mat)
    sol = [_dot3(_split(inv[hd]), _split(rhs[hd])) for hd in heads]
    s = [s_ref[hd] for hd in heads]
    v_new = [sol[hd][:, :HEAD_DIM] - _bdot(sol[hd][:, HEAD_DIM:], s[hd]) for hd in heads]
    o = [_bdot(q[hd] * eg[hd], s[hd]) + _bdot(qk[hd], v_new[hd]) for hd in heads]
    for hd in heads:
        g_end = g_col[hd][CHUNK - 1:CHUNK, :]
        kd = k[hd] * jnp.exp(g_end - g_col[hd])
        s_ref[hd] = jnp.exp(g_end) * s[hd] + _bdot(kd.T, v_new[hd])
        o_ref[:, sls[hd]] = (_rms(o[hd], ng_ref[...])
                             * _silu(gate_ref[:, sls[hd]].astype(F32))).astype(o_ref.dtype)


def _even_mixer_kernel(hq_ref, hf_ref, hi_ref, hgate_ref, xq_ref, xk_ref, xv_ref, xgate_ref, zs_ref,
                       lb_ref, hng_ref, cw_ref, alog_ref, dtb_ref, gng_ref, o_ref,
                       st_ref, bpad_ref, kpad_ref, s_ref, xbuf_ref):
    w = EVEN_WIDTH
    _hgrn_kernel(hq_ref, hf_ref, hi_ref, hgate_ref, lb_ref, hng_ref, o_ref.at[:, 0:w],
                 st_ref, bpad_ref, kpad_ref)
    _gdn_kernel(xq_ref, xk_ref, xv_ref, xgate_ref, zs_ref, cw_ref, alog_ref, dtb_ref, gng_ref,
                o_ref.at[:, w:2 * w], s_ref, xbuf_ref)


def _even_mixer(z, zs, lb, hgrn_g, conv_w, alog, dtb, gdn_g, batch, seq):
    nchunk = seq // CHUNK
    w = EVEN_WIDTH
    col = lambda c: pl.BlockSpec((CHUNK, w), lambda b, n, c=c: (b * nchunk + n, c))
    vec = lambda width: pl.BlockSpec((1, width), lambda b, n: (0, 0))
    state = pltpu.VMEM((EVEN_HEADS, HEAD_DIM, HEAD_DIM), F32)
    return pl.pallas_call(
        _even_mixer_kernel,
        grid=(batch, nchunk),
        in_specs=[col(c) for c in range(8)] + [
            pl.BlockSpec((CHUNK, LANES), lambda b, n: (b * nchunk + n, 0)),
            vec(w), vec(HEAD_DIM),
            pl.BlockSpec((CONV_K, 3 * w), lambda b, n: (0, 0)),
            vec(LANES), vec(LANES), vec(HEAD_DIM)],
        out_specs=pl.BlockSpec((CHUNK, 2 * w), lambda b, n: (b * nchunk + n, 0)),
        out_shape=jax.ShapeDtypeStruct((batch * seq, 2 * w), BF16),
        scratch_shapes=[state, pltpu.VMEM((HGRN_BAND + CHUNK, w), F32), pltpu.VMEM((HGRN_BAND + CHUNK, w), F32),
                        state, pltpu.VMEM((3, 8 + CHUNK, w), F32)],
        compiler_params=_params(("parallel", "arbitrary")),
        name="even_mixer",
    )(*([z] * 8), zs, lb, hgrn_g, conv_w, alog, dtb, gdn_g)


def _compress_kernel(kc_ref, vc_ref, pos_ref, w1_ref, w2_ref, ko_ref, vo_ref, x_ref):
    nblk = ko_ref.shape[0]
    row = lax.broadcasted_iota(jnp.int32, (nblk, HEAD_DIM), 0)
    half = CMP_STRIDE * HEAD_DIM
    for a, (src_ref, out_ref) in enumerate(((kc_ref, ko_ref), (vc_ref, vo_ref))):
        x_ref[...] = src_ref[...].astype(F32)
        p1 = jnp.zeros((nblk, HEAD_DIM), F32)
        p2 = jnp.zeros((nblk, HEAD_DIM), F32)
        for l in range(CMP_STRIDE):
            rows = x_ref[pl.ds(l, nblk, stride=CMP_STRIDE), :]
            p1 = p1 + _bdot(rows + pos_ref[a, l:l + 1, :], w1_ref[a, l * HEAD_DIM:(l + 1) * HEAD_DIM, :])
            p2 = p2 + _bdot(rows + pos_ref[a, CMP_STRIDE + l:CMP_STRIDE + l + 1, :],
                            w1_ref[a, half + l * HEAD_DIM:half + (l + 1) * HEAD_DIM, :])
        pre = p1 + pltpu.roll(p2, nblk - 1, 0)
        out = _bdot(_silu(pre), w2_ref[a])
        out_ref[...] = jnp.where(row < nblk - 1, out, 0.0)


def _compress(z, pos, w1, w2, batch, seq):
    nblk = seq // CMP_STRIDE
    src = lambda base: pl.BlockSpec((None, seq, HEAD_DIM), lambda b, g, base=base: (b, 0, base + g))
    out = pl.BlockSpec((None, None, nblk, HEAD_DIM), lambda b, g: (b, g, 0, 0))
    full = lambda a: pl.BlockSpec(a.shape, lambda b, g: (0,) * a.ndim)
    shape = jax.ShapeDtypeStruct((batch, NSA_GROUPS, nblk, HEAD_DIM), F32)
    return pl.pallas_call(
        _compress_kernel,
        grid=(batch, NSA_GROUPS),
        in_specs=[src(16), src(20), full(pos), full(w1), full(w2)],
        out_specs=[out, out],
        out_shape=[shape, shape],
        scratch_shapes=[pltpu.VMEM((seq, HEAD_DIM), F32)],
        compiler_params=_params(("parallel", "parallel")),
        name="nsa_compress",
    )(z, z, pos, w1, w2)


def _nsa_kernel(q_ref, kc_ref, vc_ref, ks_ref, vs_ref, kw_ref, vw_ref, gl_ref, kaug_ref, caug_ref, o_ref,
                m_ref, l_ref, acc_ref, m2_ref, l2_ref, acc2_ref, *, seq):
    g = pl.program_id(1)
    qi = pl.program_id(2)
    tq = ATT_TILE
    rows = NSA_HPG * tq
    q0 = qi * tq
    ncmp = seq // CMP_STRIDE
    nblk = seq // SEL_BLOCK

    qb = jnp.concatenate([q_ref[:, hp * HEAD_DIM:(hp + 1) * HEAD_DIM] for hp in range(NSA_HPG)], axis=0)

    rrow = lax.broadcasted_iota(jnp.int32, (rows, LANES), 0)
    rlane = lax.broadcasted_iota(jnp.int32, (rows, LANES), 1)
    hp_idx = _div_pow2(rrow, tq)
    tpos = q0 + (rrow - hp_idx * tq)
    head = (g * NSA_HPG + hp_idx).astype(F32)
    slope = jnp.exp((-8.0 * np.log(2.0) / NSA_HEADS) * (head + 1.0)) * LOG2_E

    s_hi = slope.astype(BF16).astype(F32)
    s_mid = (slope - s_hi).astype(BF16).astype(F32)
    s_lo = slope - s_hi - s_mid
    piece = jnp.where(rlane < KAUG_POS_LANE + 2, s_hi, jnp.where(rlane < KAUG_POS_LANE + 4, s_mid, s_lo))
    bias_cols = jnp.where((rlane >= KAUG_POS_LANE) & (rlane < KAUG_POS_LANE + 6), piece, 0.0)
    q_win = jnp.concatenate([qb, bias_cols.astype(BF16)], axis=1)

    cpos = lax.broadcasted_iota(jnp.int32, (rows, ncmp), 1) * CMP_STRIDE + (CMP_LEN - 1)
    valid_c = tpos[:, :ncmp] >= cpos
    kc_aug = jnp.concatenate([kc_ref[...].astype(BF16), caug_ref[...]], axis=1)
    s_c = lax.dot_general(q_win, kc_aug, (((1,), (1,)), ((), ())), preferred_element_type=F32)
    s_c = jnp.where(valid_c, s_c, MASK_VALUE)
    m_c = jnp.max(s_c, axis=-1, keepdims=True)
    e_c = jnp.where(valid_c, jnp.exp2(s_c - m_c), 0.0)
    den = jnp.sum(e_c, axis=-1, keepdims=True)
    p_c = e_c / jnp.where(den > 0.0, den, 1.0)
    o_cmp = _bdot(p_c, vc_ref[...])

    p_sum = p_c[0:tq] + p_c[tq:2 * tq] + p_c[2 * tq:3 * tq] + p_c[3 * tq:4 * tq]
    ob = lax.broadcasted_iota(jnp.int32, (KAUG_POS_LANE, ncmp), 0) * SEL_BLOCK
    oj = lax.broadcasted_iota(jnp.int32, (KAUG_POS_LANE, ncmp), 1) * CMP_STRIDE
    overlap_t = ((oj < ob + SEL_BLOCK) & (oj + CMP_LEN > ob)).astype(F32)
    imp_t = lax.dot_general(overlap_t, p_sum, (((1,), (1,)), ((), ())), precision=HIGHEST,
                            preferred_element_type=F32)
    blk = lax.broadcasted_iota(jnp.int32, (KAUG_POS_LANE, tq), 0)
    cur = _div_pow2(q0 + lax.broadcasted_iota(jnp.int32, (KAUG_POS_LANE, tq), 1), SEL_BLOCK)
    forced = (blk == 0) | (blk == cur) | (blk == cur - 1)
    score = jnp.where(forced, SEL_FORCE, jnp.where(blk > cur, -SEL_FORCE, imp_t))
    score = jnp.where(blk < nblk, score, -jnp.inf)
    ahead = jnp.zeros((KAUG_POS_LANE, tq), F32)
    for j in range(nblk):
        sj = score[j:j + 1, :]
        wins_tie = jnp.where(blk > j, 1.0, 0.0)
        ahead = ahead + jnp.where(sj > score, 1.0, jnp.where(sj == score, wins_tie, 0.0))
    sel_t = jnp.where(ahead < float(min(N_SEL, nblk)), 1.0, 0.0)
    sel_t = jnp.concatenate([sel_t, jnp.zeros((LANES - KAUG_POS_LANE, tq), F32)], axis=0)
    sel = sel_t.T
    sel_cols = (jnp.concatenate([sel] * NSA_HPG, axis=0) - 1.0) * SEL_MASK_BIG
    q_sel = jnp.concatenate([qb, jnp.where(rlane < KAUG_POS_LANE, sel_cols, bias_cols).astype(BF16)], axis=1)

    pair = KEY_TILE
    below_diag = (lax.broadcasted_iota(jnp.int32, (rows, pair), 1)
                  <= jnp.bitwise_and(lax.broadcasted_iota(jnp.int32, (rows, pair), 0), tq - 1))
    causal = lambda sc: jnp.where(below_diag, sc, MASK_VALUE)
    in_window = lambda sc: jnp.where(below_diag, MASK_VALUE, sc)
    ones = jnp.ones((pair, LANES), BF16)
    for ref in (m_ref, m2_ref):
        ref[...] = jnp.full_like(ref, MASK_VALUE)
    for ref in (l_ref, acc_ref, l2_ref, acc2_ref):
        ref[...] = jnp.zeros_like(ref)
    sel_branch = (q_sel, ks_ref, vs_ref, m_ref, l_ref, acc_ref)
    win_branch = (q_win, kw_ref, vw_ref, m2_ref, l2_ref, acc2_ref)

    def attend_pair(j, branches, masks):
        k0 = pl.multiple_of(j * pair, pair)
        kaug = kaug_ref[pl.ds(k0, pair), :]
        s = []
        for (qa, k_ref, _, _, _, _), mask in zip(branches, masks):
            kp = jnp.concatenate([k_ref[pl.ds(k0, pair), :].astype(BF16), kaug], axis=1)
            sc = lax.dot_general(qa, kp, (((1,), (1,)), ((), ())), preferred_element_type=F32)
            s.append(sc if mask is None else mask(sc))
        m_prev = [br[3][...] for br in branches]
        m_new = [jnp.maximum(mp, jnp.max(jnp.maximum(sc[:, :LANES], sc[:, LANES:]), axis=-1, keepdims=True))
                 for mp, sc in zip(m_prev, s)]
        p = [jnp.concatenate([jnp.exp2(sc[:, :LANES] - mn), jnp.exp2(sc[:, LANES:] - mn)], axis=1).astype(BF16)
             for sc, mn in zip(s, m_new)]
        alpha = [jnp.exp2(mp - mn) for mp, mn in zip(m_prev, m_new)]
        pv = [jnp.dot(pb, jnp.concatenate([br[2][pl.ds(k0, pair), :].astype(BF16), ones], axis=1),
                      preferred_element_type=F32) for br, pb in zip(branches, p)]
        for (_, _, _, mr, lr, ar), r, al, mn in zip(branches, pv, alpha, m_new):
            ar[...] = al * ar[...] + r[:, :LANES]
            lr[...] = al * lr[...] + r[:, LANES:]
            mr[...] = mn

    npair = qi + 1
    win_pairs = WINDOW // pair + 1

    def far_body(j, carry):
        attend_pair(j, (sel_branch,), (None,))
        return carry

    lax.fori_loop(0, jnp.maximum(npair - win_pairs, 0), far_body, 0)
    for back in range(win_pairs, 0, -1):
        j = npair - back
        sel_mask = causal if back == 1 else None
        win_mask = causal if back == 1 else (in_window if back == win_pairs else None)

        @pl.when(j >= 0)
        def _(j=j, sel_mask=sel_mask, win_mask=win_mask):
            attend_pair(j, (sel_branch, win_branch), (sel_mask, win_mask))

    o_slc = acc_ref[...] / l_ref[...]
    o_win = acc2_ref[...] / l2_ref[...]

    ngate = NSA_HPG * 3
    erow = lax.broadcasted_iota(jnp.int32, (LANES, LANES), 0)
    elane = lax.broadcasted_iota(jnp.int32, (LANES, LANES), 1)
    pick_gate = (erow == g * ngate + elane).astype(F32)
    gates = _hdot(_sigmoid(gl_ref[...]), pick_gate)
    for hp in range(NSA_HPG):
        rs = slice(hp * tq, (hp + 1) * tq)
        o = (gates[:, 3 * hp:3 * hp + 1] * o_cmp[rs] + gates[:, 3 * hp + 1:3 * hp + 2] * o_slc[rs]
             + gates[:, 3 * hp + 2:3 * hp + 3] * o_win[rs])
        o_ref[:, hp * HEAD_DIM:(hp + 1) * HEAD_DIM] = o.astype(o_ref.dtype)


def _nsa_attention(z, zs, k_cmp, v_cmp, batch, seq):
    tq = ATT_TILE
    hq = NSA_HPG * HEAD_DIM
    nblk = seq // CMP_STRIDE
    assert ATT_TILE == KEY_TILE and seq % KEY_TILE == 0 and seq // SEL_BLOCK <= KAUG_POS_LANE
    assert WINDOW % KEY_TILE == 0
    kv = lambda base: pl.BlockSpec((None, seq, HEAD_DIM), lambda b, g, i, base=base: (b, 0, base + g))
    cmp_spec = pl.BlockSpec((None, None, nblk, HEAD_DIM), lambda b, g, i: (b, g, 0, 0))
    return pl.pallas_call(
        functools.partial(_nsa_kernel, seq=seq),
        grid=(batch, NSA_GROUPS, seq // tq),
        in_specs=[pl.BlockSpec((None, tq, hq), lambda b, g, i: (b, i, g)),
                  cmp_spec, cmp_spec, kv(24), kv(28), kv(32), kv(36),
                  pl.BlockSpec((None, tq, LANES), lambda b, g, i: (b, i, 0)),
                  pl.BlockSpec((seq, LANES), lambda b, g, i: (0, 0)),
                  pl.BlockSpec((nblk, LANES), lambda b, g, i: (0, 0))],
        out_specs=pl.BlockSpec((None, tq, hq), lambda b, g, i: (b, i, g)),
        out_shape=jax.ShapeDtypeStruct((batch, seq, NSA_HEADS * HEAD_DIM), BF16),
        scratch_shapes=[pltpu.VMEM((NSA_HPG * tq, LANES), F32)] * 6,
        compiler_params=_params(("parallel", "parallel", "arbitrary")),
        name="nsa_attention",
    )(z, k_cmp, v_cmp, z, z, z, z, zs, _key_aug(np.arange(seq), True),
      _key_aug(np.arange(nblk) * CMP_STRIDE + CMP_LEN - 1, False))


def _key_aug(kpos, with_block):
    aug = np.zeros((len(kpos), LANES), np.float32)
    if with_block:
        aug[np.arange(len(kpos)), kpos // SEL_BLOCK] = 1.0
    for c in range(3):
        aug[:, KAUG_POS_LANE + 2 * c] = kpos - kpos % 256
        aug[:, KAUG_POS_LANE + 2 * c + 1] = kpos % 256
    return jnp.asarray(aug, BF16)


def _w_in_tail(w_t, layer, n_main):
    tail = w_t[layer, n_main:, :]
    return jnp.pad(tail, ((0, LANES - tail.shape[0]), (0, 0)))


def _pad_lanes(v):
    return jnp.pad(v.astype(F32), (0, LANES - v.shape[0])).reshape(1, LANES)


def _even_layer(h, g_in, g_out, w_in, j, lb, conv_w, a_log, dt_bias, hgrn_g, gdn_g, w_out_bf, batch, seq):
    n_main = 8 * EVEN_WIDTH
    z, zs = _proj_in(h, g_in, w_in, j, _w_in_tail(w_in, j, n_main), jnp.ones((1, n_main), F32))
    o = _even_mixer(z, zs, lb.reshape(1, EVEN_WIDTH), hgrn_g.reshape(1, HEAD_DIM).astype(F32),
                    conv_w.astype(F32), _pad_lanes(a_log), _pad_lanes(dt_bias),
                    gdn_g.reshape(1, HEAD_DIM).astype(F32), batch, seq)
    return _proj_out(o, w_out_bf, j, h, g_out)


def _odd_layer(h, g_in, g_out, w_in, j, cmp_pos, cmp_w1, cmp_w2, w_out_bf, batch, seq):
    n_q = NSA_HEADS * HEAD_DIM
    n_main = n_q + 6 * NSA_GROUPS * HEAD_DIM
    col_scale = jnp.where(jnp.arange(n_main) < n_q, HEAD_DIM ** -0.5 * LOG2_E, 1.0).astype(F32).reshape(1, n_main)
    z, zs = _proj_in(h, g_in, w_in, j, _w_in_tail(w_in, j, n_main), col_scale)
    z3 = z.reshape(batch, seq, n_main)
    k_cmp, v_cmp = _compress(z3, cmp_pos.astype(F32), cmp_w1.astype(BF16), cmp_w2.astype(BF16), batch, seq)
    o = _nsa_attention(z3, zs.reshape(batch, seq, LANES), k_cmp, v_cmp, batch, seq)
    return _proj_out(o.reshape(batch * seq, -1), w_out_bf, j, h, g_out)


def kernel(x, norm_g, ab_w_in, hgrn_lb_logits, gdn_conv_w, gdn_a_log, gdn_dt_bias, hgrn_norm_g, gdn_norm_g,
           ab_w_out, nsa_w_in, nsa_cmp_pos, nsa_cmp_w1, nsa_cmp_w2, nsa_w_out, ffn_w_gu, ffn_w_down):
    batch, seq, d = x.shape
    depth = norm_g.shape[0]
    lb_all = jnp.cumsum(jax.nn.softmax(hgrn_lb_logits.astype(F32), axis=0), axis=0)
    lb_all = lb_all - lb_all[:1]
    g = norm_g.astype(F32).reshape(depth, 4, 1, d)
    ab_w_out_bf, nsa_w_out_bf = ab_w_out.astype(BF16), nsa_w_out.astype(BF16)
    ab_w_in_t, nsa_w_in_t = jnp.swapaxes(ab_w_in, 1, 2), jnp.swapaxes(nsa_w_in, 1, 2)
    h = x.reshape(batch * seq, d)
    for layer in range(depth):
        j = layer // 2
        if layer % 2 == 0:
            h = _even_layer(h, g[layer, 0], g[layer, 1], ab_w_in_t, j, lb_all[j], gdn_conv_w[j],
                            gdn_a_log[j], gdn_dt_bias[j], hgrn_norm_g[j], gdn_norm_g[j], ab_w_out_bf, batch, seq)
        else:
            h = _odd_layer(h, g[layer, 0], g[layer, 1], nsa_w_in_t, j, nsa_cmp_pos[j],
                           nsa_cmp_w1[j], nsa_cmp_w2[j], nsa_w_out_bf, batch, seq)
        h = _ffn(h, g[layer, 2], ffn_w_gu, ffn_w_down, layer, g[layer, 3])
    return h.reshape(batch, seq, d)
```

```python
import functools

import numpy as np
import jax
import jax.numpy as jnp
from jax import lax
from jax.experimental import pallas as pl
from jax.experimental.pallas import tpu as pltpu

F32 = jnp.float32
BF16 = jnp.bfloat16
HIGHEST = lax.Precision.HIGHEST

NORM_EPS = 1e-6
MASK_VALUE = -1e30

LANES = 128
HEAD_DIM = 128
CHUNK = 64
HGRN_BAND = 16
CONV_K = 4
EVEN_HEADS = 8
EVEN_WIDTH = EVEN_HEADS * HEAD_DIM

NSA_GROUPS = 4
NSA_HPG = 4
NSA_HEADS = NSA_GROUPS * NSA_HPG
CMP_LEN = 32
CMP_STRIDE = 16
SEL_BLOCK = 64
N_SEL = 8
WINDOW = 512
SEL_FORCE = 1e4
ATT_TILE = 256
KEY_TILE = 256
KAUG_POS_LANE = 32
SEL_MASK_BIG = 2.0 ** 100
LOG2_E = float(np.log2(np.e))

VMEM_LIMIT = 56 * 1024 * 1024


def _params(semantics):
    return pltpu.CompilerParams(dimension_semantics=semantics, vmem_limit_bytes=VMEM_LIMIT)


def _rms(x, g):
    return x * lax.rsqrt(jnp.mean(x * x, axis=-1, keepdims=True) + NORM_EPS) * g


def _sigmoid(x):
    return 1.0 / (1.0 + jnp.exp(-x))


def _silu(x):
    return x * _sigmoid(x)


def _bdot(a, b):
    return jnp.dot(a.astype(BF16), b.astype(BF16), preferred_element_type=F32)


def _bdot_nt(a, b):
    return lax.dot_general(a.astype(BF16), b.astype(BF16), (((1,), (1,)), ((), ())),
                           preferred_element_type=F32)


def _hdot(a, b):
    return jnp.dot(a, b, precision=HIGHEST, preferred_element_type=F32)


def _split(x):
    hi = x.astype(BF16)
    return hi, (x - hi.astype(F32)).astype(BF16)


def _dot3(a, b):
    (a_hi, a_lo), (b_hi, b_lo) = a, b
    dot = lambda x, y: jnp.dot(x, y, preferred_element_type=F32)
    return dot(a_hi, b_hi) + (dot(a_lo, b_hi) + dot(a_hi, b_lo))


def _div_pow2(x, n):
    return jnp.right_shift(x, int(n).bit_length() - 1)


def _proj_in_kernel(h_ref, g_ref, w_ref, ws_ref, cs_ref, z_ref, zs_ref, xn_ref):
    @pl.when(pl.program_id(1) == 0)
    def _():
        xn = _rms(h_ref[...], g_ref[...]).astype(BF16)
        xn_ref[...] = xn
        zs_ref[...] = _bdot_nt(xn, ws_ref[...])

    z = _bdot_nt(xn_ref[...], w_ref[...])
    z_ref[...] = (z * cs_ref[...]).astype(z_ref.dtype)


def _proj_in(h, g, w_t, layer, ws_t, col_scale, tm=1024, tn=1024):
    m, d = h.shape
    n = col_scale.shape[1]
    assert m % tm == 0 and n % tn == 0
    return pl.pallas_call(
        _proj_in_kernel,
        grid=(m // tm, n // tn),
        in_specs=[
            pl.BlockSpec((tm, d), lambda i, j: (i, 0)),
            pl.BlockSpec((1, d), lambda i, j: (0, 0)),
            pl.BlockSpec((None, tn, d), lambda i, j: (layer, j, 0)),
            pl.BlockSpec((LANES, d), lambda i, j: (0, 0)),
            pl.BlockSpec((1, tn), lambda i, j: (0, j)),
        ],
        out_specs=[
            pl.BlockSpec((tm, tn), lambda i, j: (i, j)),
            pl.BlockSpec((tm, LANES), lambda i, j: (i, 0)),
        ],
        out_shape=[jax.ShapeDtypeStruct((m, n), BF16), jax.ShapeDtypeStruct((m, LANES), F32)],
        scratch_shapes=[pltpu.VMEM((tm, d), BF16)],
        compiler_params=_params(("parallel", "arbitrary")),
        name="proj_in",
    )(h, g, w_t, ws_t, col_scale)


def _proj_out_kernel(o_ref, w_ref, h_ref, g_ref, out_ref):
    m = jnp.dot(o_ref[...], w_ref[...], preferred_element_type=F32)
    out_ref[...] = h_ref[...] + _rms(m, g_ref[...])


def _proj_out(o, w, layer, h, g, tm=512):
    m, k = o.shape
    d = w.shape[2]
    return pl.pallas_call(
        _proj_out_kernel,
        grid=(m // tm,),
        in_specs=[
            pl.BlockSpec((tm, k), lambda i: (i, 0)),
            pl.BlockSpec((None, k, d), lambda i: (layer, 0, 0)),
            pl.BlockSpec((tm, d), lambda i: (i, 0)),
            pl.BlockSpec((1, d), lambda i: (0, 0)),
        ],
        out_specs=pl.BlockSpec((tm, d), lambda i: (i, 0)),
        out_shape=jax.ShapeDtypeStruct((m, d), F32),
        compiler_params=_params(("parallel",)),
        name="proj_out",
    )(o, w, h, g)


def _ffn_kernel(h_ref, gin_ref, wg_ref, wu_ref, wd_ref, gout_ref, out_ref, xn_ref):
    f = pl.program_id(1)

    @pl.when(f == 0)
    def _():
        xn_ref[...] = _rms(h_ref[...], gin_ref[...]).astype(BF16)
        out_ref[...] = jnp.zeros_like(out_ref)

    xn = xn_ref[...]
    gate = jnp.dot(xn, wg_ref[...].astype(BF16), preferred_element_type=F32)
    up = jnp.dot(xn, wu_ref[...].astype(BF16), preferred_element_type=F32)
    act = (_silu(gate) * up).astype(BF16)
    out_ref[...] += jnp.dot(act, wd_ref[...].astype(BF16), preferred_element_type=F32)

    @pl.when(f == pl.num_programs(1) - 1)
    def _():
        out_ref[...] = h_ref[...] + _rms(out_ref[...], gout_ref[...])


def _ffn(h, g_in, w_gu, w_down, layer, g_out, tm=1024, tf=256):
    m, d = h.shape
    d_ff = w_down.shape[1]
    nf = d_ff // tf
    assert m % tm == 0 and d_ff % tf == 0
    return pl.pallas_call(
        _ffn_kernel,
        grid=(m // tm, nf),
        in_specs=[
            pl.BlockSpec((tm, d), lambda i, f: (i, 0), pipeline_mode=pl.Buffered(1)),
            pl.BlockSpec((1, d), lambda i, f: (0, 0)),
            pl.BlockSpec((None, d, tf), lambda i, f: (layer, 0, f)),
            pl.BlockSpec((None, d, tf), lambda i, f: (layer, 0, nf + f)),
            pl.BlockSpec((None, tf, d), lambda i, f: (layer, f, 0)),
            pl.BlockSpec((1, d), lambda i, f: (0, 0)),
        ],
        out_specs=pl.BlockSpec((tm, d), lambda i, f: (i, 0)),
        out_shape=jax.ShapeDtypeStruct((m, d), F32),
        scratch_shapes=[pltpu.VMEM((tm, d), BF16)],
        compiler_params=_params(("parallel", "arbitrary")),
        name="ffn",
    )(h, g_in, w_gu, w_gu, w_down, g_out)


def _hgrn_kernel(q_ref, f_ref, i_ref, gate_ref, lb_ref, ng_ref, o_ref, st_ref, bpad_ref, kpad_ref):
    band = HGRN_BAND
    w = EVEN_WIDTH

    @pl.when(pl.program_id(1) == 0)
    def _():
        st_ref[...] = jnp.zeros_like(st_ref)
        bpad_ref[0:band, :] = jnp.zeros((band, w), F32)
        kpad_ref[0:band, :] = jnp.zeros((band, w), F32)

    r64 = lax.broadcasted_iota(jnp.int32, (CHUNK, CHUNK), 0)
    c64 = lax.broadcasted_iota(jnp.int32, (CHUNK, CHUNK), 1)
    tril = (r64 >= c64).astype(F32)
    offset = (lax.broadcasted_iota(jnp.int32, (CHUNK, LANES), 0)
              - lax.broadcasted_iota(jnp.int32, (CHUNK, LANES), 1))
    ones = jnp.ones((LANES, LANES), BF16)

    lb = lb_ref[...]
    f_all = lb + (1.0 - lb) * _sigmoid(f_ref[...].astype(F32))
    b_all = _hdot(tril, jnp.log(f_all) * LOG2_E)
    bpad_ref[band:band + CHUNK, :] = b_all
    kpad_ref[band:band + CHUNK, :] = 1.0 - f_all

    for hd in range(EVEN_HEADS):
        sl = slice(hd * HEAD_DIM, (hd + 1) * HEAD_DIM)
        q = q_ref[:, sl].astype(F32)
        b = b_all[:, sl]
        k = kpad_ref[band:band + CHUNK, sl]
        v = i_ref[:, sl].astype(F32)

        a = jnp.zeros((CHUNK, LANES), F32)
        b_ext = bpad_ref[:, sl]
        k_ext = kpad_ref[:, sl]
        for d in range(band):
            bs = b if d == 0 else pltpu.roll(b_ext, d, 0)[band:, :]
            ks = k if d == 0 else pltpu.roll(k_ext, d, 0)[band:, :]
            e = jnp.exp2(b - bs) * (q * ks)
            r = jnp.dot(e.astype(BF16), ones, preferred_element_type=F32)
            a = jnp.where(offset == d, r, a)

        far = [jnp.zeros((band, CHUNK), F32)]
        for blk in range(1, CHUNK // band):
            r0 = blk * band
            br = b[r0:r0 + 1, :]
            qi = q[r0:r0 + band, :] * jnp.exp2(b[r0:r0 + band, :] - br)
            ki = k * jnp.exp2(jnp.minimum(br - b, 0.0))
            far.append(_bdot_nt(qi, ki))
        a_far = jnp.where(r64 - c64 >= band, jnp.concatenate(far, axis=0), 0.0)

        st = st_ref[hd]
        o = _bdot(a[:, :CHUNK] + a_far, v) + _bdot_nt(q * jnp.exp2(b), st)
        b_end = b[CHUNK - 1:CHUNK, :]
        kd = k * jnp.exp2(b_end - b)
        st_ref[hd] = jnp.exp2(b_end) * st + _bdot(v.T, kd)
        o_ref[:, sl] = (_rms(o, ng_ref[...]) * _silu(gate_ref[:, sl].astype(F32))).astype(o_ref.dtype)


def _unit_lower_inverses(l_mats):
    r = lax.broadcasted_iota(jnp.int32, (CHUNK, CHUNK), 0)
    c = lax.broadcasted_iota(jnp.int32, (CHUNK, CHUNK), 1)
    eye = (r == c).astype(F32)
    same16 = _div_pow2(r, 16) == _div_pow2(c, 16)
    same32 = _div_pow2(r, 32) == _div_pow2(c, 32)
    m = [_split(jnp.where(same16, -l, 0.0)) for l in l_mats]
    p = [eye + jnp.where(same16, -l, 0.0) for l in l_mats]
    for _ in range(3):
        m = [_split(_dot3(x, x)) for x in m]
        p = [y + _dot3(_split(y), x) for y, x in zip(p, m)]
    for off_diag in (same32 & jnp.logical_not(same16), jnp.logical_not(same32)):
        ps = [_split(y) for y in p]
        t = [_dot3(ys, _split(jnp.where(off_diag, l, 0.0))) for ys, l in zip(ps, l_mats)]
        p = [y - _dot3(_split(x), ys) for y, ys, x in zip(p, ps, t)]
    return p


def _gdn_kernel(xq_ref, xk_ref, xv_ref, gate_ref, zs_ref, cw_ref, alog_ref, dtb_ref, ng_ref,
                o_ref, s_ref, xbuf_ref):
    n = pl.program_id(1)
    w = EVEN_WIDTH

    @pl.when(n == 0)
    def _():
        s_ref[...] = jnp.zeros_like(s_ref)
        xbuf_ref[:, 0:8, :] = jnp.zeros((3, 8, w), F32)

    conv = []
    for a, x_ref in enumerate((xq_ref, xk_ref, xv_ref)):
        x = x_ref[...].astype(F32)
        xbuf_ref[a, 8:8 + CHUNK, :] = x
        y = jnp.zeros((CHUNK, w), F32)
        for j in range(CONV_K):
            off = 8 - (CONV_K - 1) + j
            y = y + xbuf_ref[a, off:off + CHUNK, :] * cw_ref[j:j + 1, a * w:(a + 1) * w]
        xbuf_ref[a, 0:8, :] = x[CHUNK - 8:, :]
        conv.append(_silu(y))
    cq, ck, cv = conv

    zs = zs_ref[...]
    log_decay = -jnp.exp(alog_ref[...]) * jax.nn.softplus(zs + dtb_ref[...])
    r64 = lax.broadcasted_iota(jnp.int32, (CHUNK, CHUNK), 0)
    c64 = lax.broadcasted_iota(jnp.int32, (CHUNK, CHUNK), 1)
    tril = r64 >= c64
    gcum = _hdot(tril.astype(F32), log_decay)
    gcum_t = gcum.T
    beta_all = _sigmoid(zs)

    heads = range(EVEN_HEADS)
    sls = [slice(hd * HEAD_DIM, (hd + 1) * HEAD_DIM) for hd in heads]
    l2 = lambda x: x * lax.rsqrt(jnp.sum(x * x, axis=-1, keepdims=True) + NORM_EPS)
    q = [l2(cq[:, sl]) * (HEAD_DIM ** -0.5) for sl in sls]
    k = [l2(ck[:, sl]) for sl in sls]
    v = [cv[:, sl] for sl in sls]
    g_col = [gcum[:, hd:hd + 1] for hd in heads]
    beta = [beta_all[:, EVEN_HEADS + hd:EVEN_HEADS + hd + 1] for hd in heads]
    decay = [jnp.exp(jnp.where(tril, g_col[hd] - gcum_t[hd:hd + 1, :], -jnp.inf)) for hd in heads]
    kb = [k[hd] * beta[hd] for hd in heads]
    l_mat = [jnp.where(r64 > c64, _bdot_nt(kb[hd], k[hd]) * decay[hd], 0.0) for hd in heads]
    qk = [_bdot_nt(q[hd], k[hd]) * decay[hd] for hd in heads]
    eg = [jnp.exp(g_col[hd]) for hd in heads]
    rhs = [jnp.concatenate([v[hd] * beta[hd], kb[hd] * eg[hd]], axis=-1) for hd in heads]
    inv = _unit_lower_inverses(l_mat)
    sol = [_dot3(_split(inv[hd]), _split(rhs[hd])) for hd in heads]
    s = [s_ref[hd] for hd in heads]
    v_new = [sol[hd][:, :HEAD_DIM] - _bdot(sol[hd][:, HEAD_DIM:], s[hd]) for hd in heads]
    o = [_bdot(q[hd] * eg[hd], s[hd]) + _bdot(qk[hd], v_new[hd]) for hd in heads]
    for hd in heads:
        g_end = g_col[hd][CHUNK - 1:CHUNK, :]
        kd = k[hd] * jnp.exp(g_end - g_col[hd])
        s_ref[hd] = jnp.exp(g_end) * s[hd] + _bdot(kd.T, v_new[hd])
        o_ref[:, sls[hd]] = (_rms(o[hd], ng_ref[...])
                             * _silu(gate_ref[:, sls[hd]].astype(F32))).astype(o_ref.dtype)


def _even_mixer_kernel(hq_ref, hf_ref, hi_ref, hgate_ref, xq_ref, xk_ref, xv_ref, xgate_ref, zs_ref,
                       lb_ref, hng_ref, cw_ref, alog_ref, dtb_ref, gng_ref, o_ref,
                       st_ref, bpad_ref, kpad_ref, s_ref, xbuf_ref):
    w = EVEN_WIDTH
    _hgrn_kernel(hq_ref, hf_ref, hi_ref, hgate_ref, lb_ref, hng_ref, o_ref.at[:, 0:w],
                 st_ref, bpad_ref, kpad_ref)
    _gdn_kernel(xq_ref, xk_ref, xv_ref, xgate_ref, zs_ref, cw_ref, alog_ref, dtb_ref, gng_ref,
                o_ref.at[:, w:2 * w], s_ref, xbuf_ref)


def _even_mixer(z, zs, lb, hgrn_g, conv_w, alog, dtb, gdn_g, batch, seq):
    nchunk = seq // CHUNK
    w = EVEN_WIDTH
    col = lambda c: pl.BlockSpec((CHUNK, w), lambda b, n, c=c: (b * nchunk + n, c))
    vec = lambda width: pl.BlockSpec((1, width), lambda b, n: (0, 0))
    state = pltpu.VMEM((EVEN_HEADS, HEAD_DIM, HEAD_DIM), F32)
    return pl.pallas_call(
        _even_mixer_kernel,
        grid=(batch, nchunk),
        in_specs=[col(c) for c in range(8)] + [
            pl.BlockSpec((CHUNK, LANES), lambda b, n: (b * nchunk + n, 0)),
            vec(w), vec(HEAD_DIM),
            pl.BlockSpec((CONV_K, 3 * w), lambda b, n: (0, 0)),
            vec(LANES), vec(LANES), vec(HEAD_DIM)],
        out_specs=pl.BlockSpec((CHUNK, 2 * w), lambda b, n: (b * nchunk + n, 0)),
        out_shape=jax.ShapeDtypeStruct((batch * seq, 2 * w), BF16),
        scratch_shapes=[state, pltpu.VMEM((HGRN_BAND + CHUNK, w), F32), pltpu.VMEM((HGRN_BAND + CHUNK, w), F32),
                        state, pltpu.VMEM((3, 8 + CHUNK, w), F32)],
        compiler_params=_params(("parallel", "arbitrary")),
        name="even_mixer",
    )(*([z] * 8), zs, lb, hgrn_g, conv_w, alog, dtb, gdn_g)


def _compress_kernel(kc_ref, vc_ref, pos_ref, w1_ref, w2_ref, ko_ref, vo_ref, x_ref):
    nblk = ko_ref.shape[0]
    row = lax.broadcasted_iota(jnp.int32, (nblk, HEAD_DIM), 0)
    half = CMP_STRIDE * HEAD_DIM
    for a, (src_ref, out_ref) in enumerate(((kc_ref, ko_ref), (vc_ref, vo_ref))):
        x_ref[...] = src_ref[...].astype(F32)
        p1 = jnp.zeros((nblk, HEAD_DIM), F32)
        p2 = jnp.zeros((nblk, HEAD_DIM), F32)
        for l in range(CMP_STRIDE):
            rows = x_ref[pl.ds(l, nblk, stride=CMP_STRIDE), :]
            p1 = p1 + _bdot(rows + pos_ref[a, l:l + 1, :], w1_ref[a, l * HEAD_DIM:(l + 1) * HEAD_DIM, :])
            p2 = p2 + _bdot(rows + pos_ref[a, CMP_STRIDE + l:CMP_STRIDE + l + 1, :],
                            w1_ref[a, half + l * HEAD_DIM:half + (l + 1) * HEAD_DIM, :])
        pre = p1 + pltpu.roll(p2, nblk - 1, 0)
        out = _bdot(_silu(pre), w2_ref[a])
        out_ref[...] = jnp.where(row < nblk - 1, out, 0.0)


def _compress(z, pos, w1, w2, batch, seq):
    nblk = seq // CMP_STRIDE
    src = lambda base: pl.BlockSpec((None, seq, HEAD_DIM), lambda b, g, base=base: (b, 0, base + g))
    out = pl.BlockSpec((None, None, nblk, HEAD_DIM), lambda b, g: (b, g, 0, 0))
    full = lambda a: pl.BlockSpec(a.shape, lambda b, g: (0,) * a.ndim)
    shape = jax.ShapeDtypeStruct((batch, NSA_GROUPS, nblk, HEAD_DIM), F32)
    return pl.pallas_call(
        _compress_kernel,
        grid=(batch, NSA_GROUPS),
        in_specs=[src(16), src(20), full(pos), full(w1), full(w2)],
        out_specs=[out, out],
        out_shape=[shape, shape],
        scratch_shapes=[pltpu.VMEM((seq, HEAD_DIM), F32)],
        compiler_params=_params(("parallel", "parallel")),
        name="nsa_compress",
    )(z, z, pos, w1, w2)


def _nsa_kernel(q_ref, kc_ref, vc_ref, ks_ref, vs_ref, kw_ref, vw_ref, gl_ref, kaug_ref, caug_ref, o_ref,
                m_ref, l_ref, acc_ref, acc2_ref, *, seq):
    g = pl.program_id(1)
    qi = pl.program_id(2)
    tq = ATT_TILE
    rows = NSA_HPG * tq
    q0 = qi * tq
    ncmp = seq // CMP_STRIDE
    nblk = seq // SEL_BLOCK

    qb = jnp.concatenate([q_ref[:, hp * HEAD_DIM:(hp + 1) * HEAD_DIM] for hp in range(NSA_HPG)], axis=0)

    rrow = lax.broadcasted_iota(jnp.int32, (rows, LANES), 0)
    rlane = lax.broadcasted_iota(jnp.int32, (rows, LANES), 1)
    hp_idx = _div_pow2(rrow, tq)
    tpos = q0 + (rrow - hp_idx * tq)
    head = (g * NSA_HPG + hp_idx).astype(F32)
    slope = jnp.exp((-8.0 * np.log(2.0) / NSA_HEADS) * (head + 1.0)) * LOG2_E

    s_hi = slope.astype(BF16).astype(F32)
    s_mid = (slope - s_hi).astype(BF16).astype(F32)
    s_lo = slope - s_hi - s_mid
    piece = jnp.where(rlane < KAUG_POS_LANE + 2, s_hi, jnp.where(rlane < KAUG_POS_LANE + 4, s_mid, s_lo))
    bias_cols = jnp.where((rlane >= KAUG_POS_LANE) & (rlane < KAUG_POS_LANE + 6), piece, 0.0)
    q_win = jnp.concatenate([qb, bias_cols.astype(BF16)], axis=1)

    cpos = lax.broadcasted_iota(jnp.int32, (rows, ncmp), 1) * CMP_STRIDE + (CMP_LEN - 1)
    valid_c = tpos[:, :ncmp] >= cpos
    kc_aug = jnp.concatenate([kc_ref[...].astype(BF16), caug_ref[...]], axis=1)
    s_c = lax.dot_general(q_win, kc_aug, (((1,), (1,)), ((), ())), preferred_element_type=F32)
    s_c = jnp.where(valid_c, s_c, MASK_VALUE)
    m_c = jnp.max(s_c, axis=-1, keepdims=True)
    e_c = jnp.where(valid_c, jnp.exp2(s_c - m_c), 0.0)
    den = jnp.sum(e_c, axis=-1, keepdims=True)
    p_c = e_c / jnp.where(den > 0.0, den, 1.0)
    o_cmp = _bdot(p_c, vc_ref[...])

    p_sum = p_c[0:tq] + p_c[tq:2 * tq] + p_c[2 * tq:3 * tq] + p_c[3 * tq:4 * tq]
    ob = lax.broadcasted_iota(jnp.int32, (KAUG_POS_LANE, ncmp), 0) * SEL_BLOCK
    oj = lax.broadcasted_iota(jnp.int32, (KAUG_POS_LANE, ncmp), 1) * CMP_STRIDE
    overlap_t = ((oj < ob + SEL_BLOCK) & (oj + CMP_LEN > ob)).astype(F32)
    imp_t = lax.dot_general(overlap_t, p_sum, (((1,), (1,)), ((), ())), precision=HIGHEST,
                            preferred_element_type=F32)
    blk = lax.broadcasted_iota(jnp.int32, (KAUG_POS_LANE, tq), 0)
    cur = _div_pow2(q0 + lax.broadcasted_iota(jnp.int32, (KAUG_POS_LANE, tq), 1), SEL_BLOCK)
    forced = (blk == 0) | (blk == cur) | (blk == cur - 1)
    score = jnp.where(forced, SEL_FORCE, jnp.where(blk > cur, -SEL_FORCE, imp_t))
    score = jnp.where(blk < nblk, score, -jnp.inf)
    ahead = jnp.zeros((KAUG_POS_LANE, tq), F32)
    for j in range(nblk):
        sj = score[j:j + 1, :]
        wins_tie = jnp.where(blk > j, 1.0, 0.0)
        ahead = ahead + jnp.where(sj > score, 1.0, jnp.where(sj == score, wins_tie, 0.0))
    sel_t = jnp.where(ahead < float(min(N_SEL, nblk)), 1.0, 0.0)
    sel_t = jnp.concatenate([sel_t, jnp.zeros((LANES - KAUG_POS_LANE, tq), F32)], axis=0)
    sel = sel_t.T
    sel_cols = (jnp.concatenate([sel] * NSA_HPG, axis=0) - 1.0) * SEL_MASK_BIG
    q_sel = jnp.concatenate([qb, jnp.where(rlane < KAUG_POS_LANE, sel_cols, bias_cols).astype(BF16)], axis=1)

    pair = KEY_TILE
    below_diag = (lax.broadcasted_iota(jnp.int32, (rows, pair), 1)
                  <= jnp.bitwise_and(lax.broadcasted_iota(jnp.int32, (rows, pair), 0), tq - 1))
    causal = lambda sc: jnp.where(below_diag, sc, MASK_VALUE)
    in_window = lambda sc: jnp.where(below_diag, MASK_VALUE, sc)
    fresh = (jnp.full((rows, LANES), MASK_VALUE, F32), jnp.zeros((rows, LANES), F32),
             jnp.zeros((rows, LANES), F32))

    def key_span(j, ntile):
        k0 = j * pair if isinstance(j, int) else pl.multiple_of(j * pair, pair)
        return pl.ds(k0, ntile * pair)

    def scores(qa, k_ref, j, ntile, mask):
        span = key_span(j, ntile)
        kp = jnp.concatenate([k_ref[span, :].astype(BF16), kaug_ref[span, :]], axis=1)
        sc = lax.dot_general(qa, kp, (((1,), (1,)), ((), ())), preferred_element_type=F32)
        return sc if mask is None else mask(sc)

    def advance(states, tiles, v_refs, j, ntile):
        span = key_span(j, ntile)
        nchunk = ntile * pair // LANES
        chunks = [[sc[:, c * LANES:(c + 1) * LANES] for c in range(nchunk)] for sc in tiles]
        m_new = [jnp.maximum(st[0], jnp.max(functools.reduce(jnp.maximum, ch), axis=-1, keepdims=True))
                 for st, ch in zip(states, chunks)]
        p = [jnp.concatenate([jnp.exp2(c - mn) for c in ch], axis=1).astype(BF16) for ch, mn in zip(chunks, m_new)]
        alpha = [jnp.exp2(st[0] - mn) for st, mn in zip(states, m_new)]
        ones = jnp.ones((ntile * pair, LANES), BF16)
        pv = [jnp.dot(pb, jnp.concatenate([v_ref[span, :].astype(BF16), ones], axis=1),
                      preferred_element_type=F32) for v_ref, pb in zip(v_refs, p)]
        return [(mn, al * st[1] + r[:, LANES:], al * st[2] + r[:, :LANES])
                for st, r, al, mn in zip(states, pv, alpha, m_new)]

    npair = qi + 1
    win_pairs = WINDOW // pair + 1

    for ref, init in zip((m_ref, l_ref, acc_ref), fresh):
        ref[...] = init

    def far_step(j, ntile):
        state, = advance([(m_ref[...], l_ref[...], acc_ref[...])], [scores(q_sel, ks_ref, j, ntile, None)],
                         [vs_ref], j, ntile)
        for ref, val in zip((m_ref, l_ref, acc_ref), state):
            ref[...] = val

    nfar = jnp.maximum(npair - win_pairs, 0)
    odd = jnp.bitwise_and(nfar, 1)
    pl.when(odd == 1)(lambda: far_step(0, 1))

    def far_body(i, carry):
        far_step(odd + 2 * i, 2)
        return carry

    lax.fori_loop(0, jnp.right_shift(nfar, 1), far_body, 0)

    def near(count):
        steps = ([(npair - count, count - 1)] if count > 1 else []) + [(npair - 1, 1)]
        window_edge = (lambda sc: jnp.concatenate([in_window(sc[:, :pair]), sc[:, pair:]], axis=1))
        before_diag_mask = window_edge if count == win_pairs else None
        last = len(steps) - 1
        s_sel = [scores(q_sel, ks_ref, j, n, causal if i == last else None) for i, (j, n) in enumerate(steps)]
        s_win = [scores(q_win, kw_ref, j, n, causal if i == last else before_diag_mask)
                 for i, (j, n) in enumerate(steps)]
        st_sel, st_win = (m_ref[...], l_ref[...], acc_ref[...]), fresh
        for (j, n), ts, tw in zip(steps, s_sel, s_win):
            st_sel, st_win = advance([st_sel, st_win], [ts, tw], [vs_ref, vw_ref], j, n)
        acc_ref[...] = st_sel[2] / st_sel[1]
        acc2_ref[...] = st_win[2] / st_win[1]

    pl.when(npair >= win_pairs)(functools.partial(near, win_pairs))
    for count in range(1, win_pairs):
        pl.when(npair == count)(functools.partial(near, count))

    o_slc = acc_ref[...]
    o_win = acc2_ref[...]

    ngate = NSA_HPG * 3
    erow = lax.broadcasted_iota(jnp.int32, (LANES, LANES), 0)
    elane = lax.broadcasted_iota(jnp.int32, (LANES, LANES), 1)
    pick_gate = (erow == g * ngate + elane).astype(F32)
    gates = _hdot(_sigmoid(gl_ref[...]), pick_gate)
    for hp in range(NSA_HPG):
        rs = slice(hp * tq, (hp + 1) * tq)
        o = (gates[:, 3 * hp:3 * hp + 1] * o_cmp[rs] + gates[:, 3 * hp + 1:3 * hp + 2] * o_slc[rs]
             + gates[:, 3 * hp + 2:3 * hp + 3] * o_win[rs])
        o_ref[:, hp * HEAD_DIM:(hp + 1) * HEAD_DIM] = o.astype(o_ref.dtype)


def _nsa_attention(z, zs, k_cmp, v_cmp, batch, seq):
    tq = ATT_TILE
    hq = NSA_HPG * HEAD_DIM
    nblk = seq // CMP_STRIDE
    assert ATT_TILE == KEY_TILE and seq % KEY_TILE == 0 and seq // SEL_BLOCK <= KAUG_POS_LANE
    assert WINDOW % KEY_TILE == 0
    kv = lambda base: pl.BlockSpec((None, seq, HEAD_DIM), lambda b, g, i, base=base: (b, 0, base + g))
    cmp_spec = pl.BlockSpec((None, None, nblk, HEAD_DIM), lambda b, g, i: (b, g, 0, 0))
    return pl.pallas_call(
        functools.partial(_nsa_kernel, seq=seq),
        grid=(batch, NSA_GROUPS, seq // tq),
        in_specs=[pl.BlockSpec((None, tq, hq), lambda b, g, i: (b, i, g)),
                  cmp_spec, cmp_spec, kv(24), kv(28), kv(32), kv(36),
                  pl.BlockSpec((None, tq, LANES), lambda b, g, i: (b, i, 0)),
                  pl.BlockSpec((seq, LANES), lambda b, g, i: (0, 0)),
                  pl.BlockSpec((nblk, LANES), lambda b, g, i: (0, 0))],
        out_specs=pl.BlockSpec((None, tq, hq), lambda b, g, i: (b, i, g)),
        out_shape=jax.ShapeDtypeStruct((batch, seq, NSA_HEADS * HEAD_DIM), BF16),
        scratch_shapes=[pltpu.VMEM((NSA_HPG * tq, LANES), F32)] * 4,
        compiler_params=_params(("parallel", "parallel", "arbitrary")),
        name="nsa_attention",
    )(z, k_cmp, v_cmp, z, z, z, z, zs, _key_aug(np.arange(seq), True),
      _key_aug(np.arange(nblk) * CMP_STRIDE + CMP_LEN - 1, False))


def _key_aug(kpos, with_block):
    aug = np.zeros((len(kpos), LANES), np.float32)
    if with_block:
        aug[np.arange(len(kpos)), kpos // SEL_BLOCK] = 1.0
    for c in range(3):
        aug[:, KAUG_POS_LANE + 2 * c] = kpos - kpos % 256
        aug[:, KAUG_POS_LANE + 2 * c + 1] = kpos % 256
    return jnp.asarray(aug, BF16)


def _w_in_tail(w_t, layer, n_main):
    tail = w_t[layer, n_main:, :]
    return jnp.pad(tail, ((0, LANES - tail.shape[0]), (0, 0)))


def _pad_lanes(v):
    return jnp.pad(v.astype(F32), (0, LANES - v.shape[0])).reshape(1, LANES)


def _even_layer(h, g_in, g_out, w_in, j, lb, conv_w, a_log, dt_bias, hgrn_g, gdn_g, w_out_bf, batch, seq):
    n_main = 8 * EVEN_WIDTH
    z, zs = _proj_in(h, g_in, w_in, j, _w_in_tail(w_in, j, n_main), jnp.ones((1, n_main), F32))
    o = _even_mixer(z, zs, lb.reshape(1, EVEN_WIDTH), hgrn_g.reshape(1, HEAD_DIM).astype(F32),
                    conv_w.astype(F32), _pad_lanes(a_log), _pad_lanes(dt_bias),
                    gdn_g.reshape(1, HEAD_DIM).astype(F32), batch, seq)
    return _proj_out(o, w_out_bf, j, h, g_out)


def _odd_layer(h, g_in, g_out, w_in, j, cmp_pos, cmp_w1, cmp_w2, w_out_bf, batch, seq):
    n_q = NSA_HEADS * HEAD_DIM
    n_main = n_q + 6 * NSA_GROUPS * HEAD_DIM
    col_scale = jnp.where(jnp.arange(n_main) < n_q, HEAD_DIM ** -0.5 * LOG2_E, 1.0).astype(F32).reshape(1, n_main)
    z, zs = _proj_in(h, g_in, w_in, j, _w_in_tail(w_in, j, n_main), col_scale)
    z3 = z.reshape(batch, seq, n_main)
    k_cmp, v_cmp = _compress(z3, cmp_pos.astype(F32), cmp_w1.astype(BF16), cmp_w2.astype(BF16), batch, seq)
    o = _nsa_attention(z3, zs.reshape(batch, seq, LANES), k_cmp, v_cmp, batch, seq)
    return _proj_out(o.reshape(batch * seq, -1), w_out_bf, j, h, g_out)


def kernel(x, norm_g, ab_w_in, hgrn_lb_logits, gdn_conv_w, gdn_a_log, gdn_dt_bias, hgrn_norm_g, gdn_norm_g,
           ab_w_out, nsa_w_in, nsa_cmp_pos, nsa_cmp_w1, nsa_cmp_w2, nsa_w_out, ffn_w_gu, ffn_w_down):
    batch, seq, d = x.shape
    depth = norm_g.shape[0]
    lb_all = jnp.cumsum(jax.nn.softmax(hgrn_lb_logits.astype(F32), axis=0), axis=0)
    lb_all = lb_all - lb_all[:1]
    g = norm_g.astype(F32).reshape(depth, 4, 1, d)
    ab_w_out_bf, nsa_w_out_bf = ab_w_out.astype(BF16), nsa_w_out.astype(BF16)
    ab_w_in_t, nsa_w_in_t = jnp.swapaxes(ab_w_in, 1, 2), jnp.swapaxes(nsa_w_in, 1, 2)
    h = x.reshape(batch * seq, d)
    for layer in range(depth):
        j = layer // 2
        if layer % 2 == 0:
            h = _even_layer(h, g[layer, 0], g[layer, 1], ab_w_in_t, j, lb_all[j], gdn_conv_w[j],
                            gdn_a_log[j], gdn_dt_bias[j], hgrn_norm_g[j], gdn_norm_g[j], ab_w_out_bf, batch, seq)
        else:
            h = _odd_layer(h, g[layer, 0], g[layer, 1], nsa_w_in_t, j, nsa_cmp_pos[j],
                           nsa_cmp_w1[j], nsa_cmp_w2[j], nsa_w_out_bf, batch, seq)
        h = _ffn(h, g[layer, 2], ffn_w_gu, ffn_w_down, layer, g[layer, 3])
    return h.reshape(batch, seq, d)
```

```python
import functools

import numpy as np
import jax
import jax.numpy as jnp
from jax import lax
from jax.experimental import pallas as pl
from jax.experimental.pallas import tpu as pltpu

F32 = jnp.float32
BF16 = jnp.bfloat16
HIGHEST = lax.Precision.HIGHEST

NORM_EPS = 1e-6
MASK_VALUE = -1e30

LANES = 128
HEAD_DIM = 128
CHUNK = 64
HGRN_BAND = 16
CONV_K = 4
EVEN_HEADS = 8
EVEN_WIDTH = EVEN_HEADS * HEAD_DIM

NSA_GROUPS = 4
NSA_HPG = 4
NSA_HEADS = NSA_GROUPS * NSA_HPG
CMP_LEN = 32
CMP_STRIDE = 16
SEL_BLOCK = 64
N_SEL = 8
WINDOW = 512
SEL_FORCE = 1e4
ATT_TILE = 256
KEY_TILE = 256
KAUG_POS_LANE = 32
SEL_MASK_BIG = 2.0 ** 100
LOG2_E = float(np.log2(np.e))

VMEM_LIMIT = 56 * 1024 * 1024


def _params(semantics):
    return pltpu.CompilerParams(dimension_semantics=semantics, vmem_limit_bytes=VMEM_LIMIT)


def _rms(x, g):
    return x * lax.rsqrt(jnp.mean(x * x, axis=-1, keepdims=True) + NORM_EPS) * g


def _sigmoid(x):
    return 1.0 / (1.0 + jnp.exp(-x))


def _silu(x):
    return x * _sigmoid(x)


def _bdot(a, b):
    return jnp.dot(a.astype(BF16), b.astype(BF16), preferred_element_type=F32)


def _bdot_nt(a, b):
    return lax.dot_general(a.astype(BF16), b.astype(BF16), (((1,), (1,)), ((), ())),
                           preferred_element_type=F32)


def _hdot(a, b):
    return jnp.dot(a, b, precision=HIGHEST, preferred_element_type=F32)


def _split(x):
    hi = x.astype(BF16)
    return hi, (x - hi.astype(F32)).astype(BF16)


def _dot3(a, b):
    (a_hi, a_lo), (b_hi, b_lo) = a, b
    dot = lambda x, y: jnp.dot(x, y, preferred_element_type=F32)
    return dot(a_hi, b_hi) + (dot(a_lo, b_hi) + dot(a_hi, b_lo))


def _div_pow2(x, n):
    return jnp.right_shift(x, int(n).bit_length() - 1)


def _proj_in_kernel(h_ref, g_ref, w_ref, ws_ref, cs_ref, z_ref, zs_ref, xn_ref):
    @pl.when(pl.program_id(1) == 0)
    def _():
        xn = _rms(h_ref[...], g_ref[...]).astype(BF16)
        xn_ref[...] = xn
        zs_ref[...] = _bdot_nt(xn, ws_ref[...])

    z = _bdot_nt(xn_ref[...], w_ref[...])
    z_ref[...] = (z * cs_ref[...]).astype(z_ref.dtype)


def _proj_in(h, g, w_t, layer, ws_t, col_scale, tm=1024, tn=1024):
    m, d = h.shape
    n = col_scale.shape[1]
    assert m % tm == 0 and n % tn == 0
    return pl.pallas_call(
        _proj_in_kernel,
        grid=(m // tm, n // tn),
        in_specs=[
            pl.BlockSpec((tm, d), lambda i, j: (i, 0)),
            pl.BlockSpec((1, d), lambda i, j: (0, 0)),
            pl.BlockSpec((None, tn, d), lambda i, j: (layer, j, 0)),
            pl.BlockSpec((LANES, d), lambda i, j: (0, 0)),
            pl.BlockSpec((1, tn), lambda i, j: (0, j)),
        ],
        out_specs=[
            pl.BlockSpec((tm, tn), lambda i, j: (i, j)),
            pl.BlockSpec((tm, LANES), lambda i, j: (i, 0)),
        ],
        out_shape=[jax.ShapeDtypeStruct((m, n), BF16), jax.ShapeDtypeStruct((m, LANES), F32)],
        scratch_shapes=[pltpu.VMEM((tm, d), BF16)],
        compiler_params=_params(("parallel", "arbitrary")),
        name="proj_in",
    )(h, g, w_t, ws_t, col_scale)


def _proj_out_kernel(o_ref, w_ref, h_ref, g_ref, out_ref):
    m = jnp.dot(o_ref[...], w_ref[...], preferred_element_type=F32)
    out_ref[...] = h_ref[...] + _rms(m, g_ref[...])


def _proj_out(o, w, layer, h, g, tm=512):
    m, k = o.shape
    d = w.shape[2]
    return pl.pallas_call(
        _proj_out_kernel,
        grid=(m // tm,),
        in_specs=[
            pl.BlockSpec((tm, k), lambda i: (i, 0)),
            pl.BlockSpec((None, k, d), lambda i: (layer, 0, 0)),
            pl.BlockSpec((tm, d), lambda i: (i, 0)),
            pl.BlockSpec((1, d), lambda i: (0, 0)),
        ],
        out_specs=pl.BlockSpec((tm, d), lambda i: (i, 0)),
        out_shape=jax.ShapeDtypeStruct((m, d), F32),
        compiler_params=_params(("parallel",)),
        name="proj_out",
    )(o, w, h, g)


def _ffn_kernel(h_ref, gin_ref, wg_ref, wu_ref, wd_ref, gout_ref, out_ref, xn_ref):
    f = pl.program_id(1)

    @pl.when(f == 0)
    def _():
        xn_ref[...] = _rms(h_ref[...], gin_ref[...]).astype(BF16)
        out_ref[...] = jnp.zeros_like(out_ref)

    xn = xn_ref[...]
    gate = jnp.dot(xn, wg_ref[...].astype(BF16), preferred_element_type=F32)
    up = jnp.dot(xn, wu_ref[...].astype(BF16), preferred_element_type=F32)
    act = (_silu(gate) * up).astype(BF16)
    out_ref[...] += jnp.dot(act, wd_ref[...].astype(BF16), preferred_element_type=F32)

    @pl.when(f == pl.num_programs(1) - 1)
    def _():
        out_ref[...] = h_ref[...] + _rms(out_ref[...], gout_ref[...])


def _ffn(h, g_in, w_gu, w_down, layer, g_out, tm=1024, tf=256):
    m, d = h.shape
    d_ff = w_down.shape[1]
    nf = d_ff // tf
    assert m % tm == 0 and d_ff % tf == 0
    return pl.pallas_call(
        _ffn_kernel,
        grid=(m // tm, nf),
        in_specs=[
            pl.BlockSpec((tm, d), lambda i, f: (i, 0), pipeline_mode=pl.Buffered(1)),
            pl.BlockSpec((1, d), lambda i, f: (0, 0)),
            pl.BlockSpec((None, d, tf), lambda i, f: (layer, 0, f)),
            pl.BlockSpec((None, d, tf), lambda i, f: (layer, 0, nf + f)),
            pl.BlockSpec((None, tf, d), lambda i, f: (layer, f, 0)),
            pl.BlockSpec((1, d), lambda i, f: (0, 0)),
        ],
        out_specs=pl.BlockSpec((tm, d), lambda i, f: (i, 0)),
        out_shape=jax.ShapeDtypeStruct((m, d), F32),
        scratch_shapes=[pltpu.VMEM((tm, d), BF16)],
        compiler_params=_params(("parallel", "arbitrary")),
        name="ffn",
    )(h, g_in, w_gu, w_gu, w_down, g_out)


def _hgrn_chunk(q_ref, f_ref, i_ref, gate_ref, lb_ref, ng_ref, o_ref, st_ref, bpad_ref, kpad_ref):
    band = HGRN_BAND

    r64 = lax.broadcasted_iota(jnp.int32, (CHUNK, CHUNK), 0)
    c64 = lax.broadcasted_iota(jnp.int32, (CHUNK, CHUNK), 1)
    tril = (r64 >= c64).astype(F32)
    offset = (lax.broadcasted_iota(jnp.int32, (CHUNK, LANES), 0)
              - lax.broadcasted_iota(jnp.int32, (CHUNK, LANES), 1))
    ones = jnp.ones((LANES, LANES), BF16)

    lb = lb_ref[...]
    f_all = lb + (1.0 - lb) * _sigmoid(f_ref[...].astype(F32))
    b_all = _hdot(tril, jnp.log(f_all) * LOG2_E)
    bpad_ref[band:band + CHUNK, :] = b_all
    kpad_ref[band:band + CHUNK, :] = 1.0 - f_all
    yield

    for hd in range(EVEN_HEADS):
        sl = slice(hd * HEAD_DIM, (hd + 1) * HEAD_DIM)
        q = q_ref[:, sl].astype(F32)
        b = b_all[:, sl]
        k = kpad_ref[band:band + CHUNK, sl]
        v = i_ref[:, sl].astype(F32)

        a = jnp.zeros((CHUNK, LANES), F32)
        b_ext = bpad_ref[:, sl]
        k_ext = kpad_ref[:, sl]
        for d in range(band):
            bs = b if d == 0 else pltpu.roll(b_ext, d, 0)[band:, :]
            ks = k if d == 0 else pltpu.roll(k_ext, d, 0)[band:, :]
            e = jnp.exp2(b - bs) * (q * ks)
            r = jnp.dot(e.astype(BF16), ones, preferred_element_type=F32)
            a = jnp.where(offset == d, r, a)

        far = [jnp.zeros((band, CHUNK), F32)]
        for blk in range(1, CHUNK // band):
            r0 = blk * band
            br = b[r0:r0 + 1, :]
            qi = q[r0:r0 + band, :] * jnp.exp2(b[r0:r0 + band, :] - br)
            ki = k * jnp.exp2(jnp.minimum(br - b, 0.0))
            far.append(_bdot_nt(qi, ki))
        a_far = jnp.where(r64 - c64 >= band, jnp.concatenate(far, axis=0), 0.0)

        st = st_ref[hd]
        o = _bdot(a[:, :CHUNK] + a_far, v) + _bdot_nt(q * jnp.exp2(b), st)
        b_end = b[CHUNK - 1:CHUNK, :]
        kd = k * jnp.exp2(b_end - b)
        st_ref[hd] = jnp.exp2(b_end) * st + _bdot(v.T, kd)
        o_ref[:, sl] = (_rms(o, ng_ref[...]) * _silu(gate_ref[:, sl].astype(F32))).astype(o_ref.dtype)
        yield


def _unit_lower_inverses(l_mats):
    r = lax.broadcasted_iota(jnp.int32, (CHUNK, CHUNK), 0)
    c = lax.broadcasted_iota(jnp.int32, (CHUNK, CHUNK), 1)
    eye = (r == c).astype(F32)
    same16 = _div_pow2(r, 16) == _div_pow2(c, 16)
    same32 = _div_pow2(r, 32) == _div_pow2(c, 32)
    m = [_split(jnp.where(same16, -l, 0.0)) for l in l_mats]
    p = [eye + jnp.where(same16, -l, 0.0) for l in l_mats]
    for _ in range(3):
        m = [_split(_dot3(x, x)) for x in m]
        yield
        p = [y + _dot3(_split(y), x) for y, x in zip(p, m)]
        yield
    for off_diag in (same32 & jnp.logical_not(same16), jnp.logical_not(same32)):
        ps = [_split(y) for y in p]
        t = [_dot3(ys, _split(jnp.where(off_diag, l, 0.0))) for ys, l in zip(ps, l_mats)]
        yield
        p = [y - _dot3(_split(x), ys) for y, ys, x in zip(p, ps, t)]
        yield
    return p


def _gdn_chunk(xq_ref, xk_ref, xv_ref, gate_ref, zs_ref, cw_ref, alog_ref, dtb_ref, ng_ref,
               o_ref, s_ref, xbuf_ref):
    w = EVEN_WIDTH

    conv = []
    for a, x_ref in enumerate((xq_ref, xk_ref, xv_ref)):
        x = x_ref[...].astype(F32)
        xbuf_ref[a, 8:8 + CHUNK, :] = x
        y = jnp.zeros((CHUNK, w), F32)
        for j in range(CONV_K):
            off = 8 - (CONV_K - 1) + j
            y = y + xbuf_ref[a, off:off + CHUNK, :] * cw_ref[j:j + 1, a * w:(a + 1) * w]
        xbuf_ref[a, 0:8, :] = x[CHUNK - 8:, :]
        conv.append(_silu(y))
        yield
    cq, ck, cv = conv

    zs = zs_ref[...]
    log_decay = -jnp.exp(alog_ref[...]) * jax.nn.softplus(zs + dtb_ref[...])
    r64 = lax.broadcasted_iota(jnp.int32, (CHUNK, CHUNK), 0)
    c64 = lax.broadcasted_iota(jnp.int32, (CHUNK, CHUNK), 1)
    tril = r64 >= c64
    gcum = _hdot(tril.astype(F32), log_decay)
    gcum_t = gcum.T
    beta_all = _sigmoid(zs)

    heads = range(EVEN_HEADS)
    sls = [slice(hd * HEAD_DIM, (hd + 1) * HEAD_DIM) for hd in heads]
    l2 = lambda x: x * lax.rsqrt(jnp.sum(x * x, axis=-1, keepdims=True) + NORM_EPS)
    q = [l2(cq[:, sl]) * (HEAD_DIM ** -0.5) for sl in sls]
    k = [l2(ck[:, sl]) for sl in sls]
    v = [cv[:, sl] for sl in sls]
    yield
    g_col = [gcum[:, hd:hd + 1] for hd in heads]
    beta = [beta_all[:, EVEN_HEADS + hd:EVEN_HEADS + hd + 1] for hd in heads]
    decay = [jnp.exp(jnp.where(tril, g_col[hd] - gcum_t[hd:hd + 1, :], -jnp.inf)) for hd in heads]
    kb = [k[hd] * beta[hd] for hd in heads]
    l_mat = [jnp.where(r64 > c64, _bdot_nt(kb[hd], k[hd]) * decay[hd], 0.0) for hd in heads]
    yield
    qk = [_bdot_nt(q[hd], k[hd]) * decay[hd] for hd in heads]
    eg = [jnp.exp(g_col[hd]) for hd in heads]
    rhs = [jnp.concatenate([v[hd] * beta[hd], kb[hd] * eg[hd]], axis=-1) for hd in heads]
    inv = yield from _unit_lower_inverses(l_mat)
    sol = [_dot3(_split(inv[hd]), _split(rhs[hd])) for hd in heads]
    yield
    s = [s_ref[hd] for hd in heads]
    v_new = [sol[hd][:, :HEAD_DIM] - _bdot(sol[hd][:, HEAD_DIM:], s[hd]) for hd in heads]
    yield
    o = [_bdot(q[hd] * eg[hd], s[hd]) + _bdot(qk[hd], v_new[hd]) for hd in heads]
    yield
    for hd in heads:
        g_end = g_col[hd][CHUNK - 1:CHUNK, :]
        kd = k[hd] * jnp.exp(g_end - g_col[hd])
        s_ref[hd] = jnp.exp(g_end) * s[hd] + _bdot(kd.T, v_new[hd])
        o_ref[:, sls[hd]] = (_rms(o[hd], ng_ref[...])
                             * _silu(gate_ref[:, sls[hd]].astype(F32))).astype(o_ref.dtype)
        if hd % 2:
            yield


def _even_mixer_kernel(hq_ref, hf_ref, hi_ref, hgate_ref, xq_ref, xk_ref, xv_ref, xgate_ref, zs_ref,
                       lb_ref, hng_ref, cw_ref, alog_ref, dtb_ref, gng_ref, o_ref,
                       st_ref, bpad_ref, kpad_ref, s_ref, xbuf_ref):
    w = EVEN_WIDTH

    @pl.when(pl.program_id(1) == 0)
    def _():
        st_ref[...] = jnp.zeros_like(st_ref)
        s_ref[...] = jnp.zeros_like(s_ref)
        bpad_ref[0:HGRN_BAND, :] = jnp.zeros((HGRN_BAND, w), F32)
        kpad_ref[0:HGRN_BAND, :] = jnp.zeros((HGRN_BAND, w), F32)
        xbuf_ref[:, 0:8, :] = jnp.zeros((3, 8, w), F32)

    hgrn = _hgrn_chunk(hq_ref, hf_ref, hi_ref, hgate_ref, lb_ref, hng_ref, o_ref.at[:, 0:w],
                       st_ref, bpad_ref, kpad_ref)
    gdn = _gdn_chunk(xq_ref, xk_ref, xv_ref, xgate_ref, zs_ref, cw_ref, alog_ref, dtb_ref, gng_ref,
                     o_ref.at[:, w:2 * w], s_ref, xbuf_ref)
    pending = [gdn, gdn, hgrn]
    while pending:
        for gen in list(pending):
            if gen in pending and next(gen, StopIteration) is StopIteration:
                pending = [x for x in pending if x is not gen]


def _even_mixer(z, zs, lb, hgrn_g, conv_w, alog, dtb, gdn_g, batch, seq):
    nchunk = seq // CHUNK
    w = EVEN_WIDTH
    col = lambda c: pl.BlockSpec((CHUNK, w), lambda b, n, c=c: (b * nchunk + n, c))
    vec = lambda width: pl.BlockSpec((1, width), lambda b, n: (0, 0))
    state = pltpu.VMEM((EVEN_HEADS, HEAD_DIM, HEAD_DIM), F32)
    return pl.pallas_call(
        _even_mixer_kernel,
        grid=(batch, nchunk),
        in_specs=[col(c) for c in range(8)] + [
            pl.BlockSpec((CHUNK, LANES), lambda b, n: (b * nchunk + n, 0)),
            vec(w), vec(HEAD_DIM),
            pl.BlockSpec((CONV_K, 3 * w), lambda b, n: (0, 0)),
            vec(LANES), vec(LANES), vec(HEAD_DIM)],
        out_specs=pl.BlockSpec((CHUNK, 2 * w), lambda b, n: (b * nchunk + n, 0)),
        out_shape=jax.ShapeDtypeStruct((batch * seq, 2 * w), BF16),
        scratch_shapes=[state, pltpu.VMEM((HGRN_BAND + CHUNK, w), F32), pltpu.VMEM((HGRN_BAND + CHUNK, w), F32),
                        state, pltpu.VMEM((3, 8 + CHUNK, w), F32)],
        compiler_params=_params(("parallel", "arbitrary")),
        name="even_mixer",
    )(*([z] * 8), zs, lb, hgrn_g, conv_w, alog, dtb, gdn_g)


def _compress_kernel(kc_ref, vc_ref, pos_ref, w1_ref, w2_ref, ko_ref, vo_ref, x_ref):
    nblk = ko_ref.shape[0]
    row = lax.broadcasted_iota(jnp.int32, (nblk, HEAD_DIM), 0)
    half = CMP_STRIDE * HEAD_DIM
    for a, (src_ref, out_ref) in enumerate(((kc_ref, ko_ref), (vc_ref, vo_ref))):
        x_ref[...] = src_ref[...].astype(F32)
        p1 = jnp.zeros((nblk, HEAD_DIM), F32)
        p2 = jnp.zeros((nblk, HEAD_DIM), F32)
        for l in range(CMP_STRIDE):
            rows = x_ref[pl.ds(l, nblk, stride=CMP_STRIDE), :]
            p1 = p1 + _bdot(rows + pos_ref[a, l:l + 1, :], w1_ref[a, l * HEAD_DIM:(l + 1) * HEAD_DIM, :])
            p2 = p2 + _bdot(rows + pos_ref[a, CMP_STRIDE + l:CMP_STRIDE + l + 1, :],
                            w1_ref[a, half + l * HEAD_DIM:half + (l + 1) * HEAD_DIM, :])
        pre = p1 + pltpu.roll(p2, nblk - 1, 0)
        out = _bdot(_silu(pre), w2_ref[a])
        out_ref[...] = jnp.where(row < nblk - 1, out, 0.0)


def _compress(z, pos, w1, w2, batch, seq):
    nblk = seq // CMP_STRIDE
    src = lambda base: pl.BlockSpec((None, seq, HEAD_DIM), lambda b, g, base=base: (b, 0, base + g))
    out = pl.BlockSpec((None, None, nblk, HEAD_DIM), lambda b, g: (b, g, 0, 0))
    full = lambda a: pl.BlockSpec(a.shape, lambda b, g: (0,) * a.ndim)
    shape = jax.ShapeDtypeStruct((batch, NSA_GROUPS, nblk, HEAD_DIM), F32)
    return pl.pallas_call(
        _compress_kernel,
        grid=(batch, NSA_GROUPS),
        in_specs=[src(16), src(20), full(pos), full(w1), full(w2)],
        out_specs=[out, out],
        out_shape=[shape, shape],
        scratch_shapes=[pltpu.VMEM((seq, HEAD_DIM), F32)],
        compiler_params=_params(("parallel", "parallel")),
        name="nsa_compress",
    )(z, z, pos, w1, w2)


def _nsa_kernel(q_ref, kc_ref, vc_ref, ks_ref, vs_ref, kw_ref, vw_ref, gl_ref, kaug_ref, caug_ref, o_ref,
                m_ref, l_ref, acc_ref, acc2_ref, *, seq):
    g = pl.program_id(1)
    qi = pl.program_id(2)
    tq = ATT_TILE
    rows = NSA_HPG * tq
    q0 = qi * tq
    ncmp = seq // CMP_STRIDE
    nblk = seq // SEL_BLOCK

    qb = jnp.concatenate([q_ref[:, hp * HEAD_DIM:(hp + 1) * HEAD_DIM] for hp in range(NSA_HPG)], axis=0)

    rrow = lax.broadcasted_iota(jnp.int32, (rows, LANES), 0)
    rlane = lax.broadcasted_iota(jnp.int32, (rows, LANES), 1)
    hp_idx = _div_pow2(rrow, tq)
    tpos = q0 + (rrow - hp_idx * tq)
    head = (g * NSA_HPG + hp_idx).astype(F32)
    slope = jnp.exp((-8.0 * np.log(2.0) / NSA_HEADS) * (head + 1.0)) * LOG2_E

    s_hi = slope.astype(BF16).astype(F32)
    s_mid = (slope - s_hi).astype(BF16).astype(F32)
    s_lo = slope - s_hi - s_mid
    piece = jnp.where(rlane < KAUG_POS_LANE + 2, s_hi, jnp.where(rlane < KAUG_POS_LANE + 4, s_mid, s_lo))
    bias_cols = jnp.where((rlane >= KAUG_POS_LANE) & (rlane < KAUG_POS_LANE + 6), piece, 0.0)
    q_win = jnp.concatenate([qb, bias_cols.astype(BF16)], axis=1)

    cpos = lax.broadcasted_iota(jnp.int32, (rows, ncmp), 1) * CMP_STRIDE + (CMP_LEN - 1)
    valid_c = tpos[:, :ncmp] >= cpos
    kc_aug = jnp.concatenate([kc_ref[...].astype(BF16), caug_ref[...]], axis=1)
    s_c = lax.dot_general(q_win, kc_aug, (((1,), (1,)), ((), ())), preferred_element_type=F32)
    s_c = jnp.where(valid_c, s_c, MASK_VALUE)
    m_c = jnp.max(s_c, axis=-1, keepdims=True)
    e_c = jnp.where(valid_c, jnp.exp2(s_c - m_c), 0.0)
    den = jnp.sum(e_c, axis=-1, keepdims=True)
    p_c = e_c / jnp.where(den > 0.0, den, 1.0)
    o_cmp = _bdot(p_c, vc_ref[...])

    p_sum = p_c[0:tq] + p_c[tq:2 * tq] + p_c[2 * tq:3 * tq] + p_c[3 * tq:4 * tq]
    ob = lax.broadcasted_iota(jnp.int32, (KAUG_POS_LANE, ncmp), 0) * SEL_BLOCK
    oj = lax.broadcasted_iota(jnp.int32, (KAUG_POS_LANE, ncmp), 1) * CMP_STRIDE
    overlap_t = ((oj < ob + SEL_BLOCK) & (oj + CMP_LEN > ob)).astype(F32)
    imp_t = lax.dot_general(overlap_t, p_sum, (((1,), (1,)), ((), ())), precision=HIGHEST,
                            preferred_element_type=F32)
    blk = lax.broadcasted_iota(jnp.int32, (KAUG_POS_LANE, tq), 0)
    cur = _div_pow2(q0 + lax.broadcasted_iota(jnp.int32, (KAUG_POS_LANE, tq), 1), SEL_BLOCK)
    forced = (blk == 0) | (blk == cur) | (blk == cur - 1)
    score = jnp.where(forced, SEL_FORCE, jnp.where(blk > cur, -SEL_FORCE, imp_t))
    score = jnp.where(blk < nblk, score, -jnp.inf)
    ahead = jnp.zeros((KAUG_POS_LANE, tq), F32)
    for j in range(nblk):
        sj = score[j:j + 1, :]
        wins_tie = jnp.where(blk > j, 1.0, 0.0)
        ahead = ahead + jnp.where(sj > score, 1.0, jnp.where(sj == score, wins_tie, 0.0))
    sel_t = jnp.where(ahead < float(min(N_SEL, nblk)), 1.0, 0.0)
    sel_t = jnp.concatenate([sel_t, jnp.zeros((LANES - KAUG_POS_LANE, tq), F32)], axis=0)
    sel = sel_t.T
    sel_cols = (jnp.concatenate([sel] * NSA_HPG, axis=0) - 1.0) * SEL_MASK_BIG
    q_sel = jnp.concatenate([qb, jnp.where(rlane < KAUG_POS_LANE, sel_cols, bias_cols).astype(BF16)], axis=1)

    pair = KEY_TILE
    below_diag = (lax.broadcasted_iota(jnp.int32, (rows, pair), 1)
                  <= jnp.bitwise_and(lax.broadcasted_iota(jnp.int32, (rows, pair), 0), tq - 1))
    causal = lambda sc: jnp.where(below_diag, sc, MASK_VALUE)
    in_window = lambda sc: jnp.where(below_diag, MASK_VALUE, sc)
    fresh = (jnp.full((rows, LANES), MASK_VALUE, F32), jnp.zeros((rows, LANES), F32),
             jnp.zeros((rows, LANES), F32))

    def key_span(j, ntile):
        k0 = j * pair if isinstance(j, int) else pl.multiple_of(j * pair, pair)
        return pl.ds(k0, ntile * pair)

    def scores(qa, k_ref, j, ntile, mask):
        span = key_span(j, ntile)
        kp = jnp.concatenate([k_ref[span, :].astype(BF16), kaug_ref[span, :]], axis=1)
        sc = lax.dot_general(qa, kp, (((1,), (1,)), ((), ())), preferred_element_type=F32)
        return sc if mask is None else mask(sc)

    def advance(states, tiles, v_refs, j, ntile):
        span = key_span(j, ntile)
        nchunk = ntile * pair // LANES
        chunks = [[sc[:, c * LANES:(c + 1) * LANES] for c in range(nchunk)] for sc in tiles]
        m_new = [jnp.maximum(st[0], jnp.max(functools.reduce(jnp.maximum, ch), axis=-1, keepdims=True))
                 for st, ch in zip(states, chunks)]
        p = [jnp.concatenate([jnp.exp2(c - mn) for c in ch], axis=1).astype(BF16) for ch, mn in zip(chunks, m_new)]
        alpha = [jnp.exp2(st[0] - mn) for st, mn in zip(states, m_new)]
        ones = jnp.ones((ntile * pair, LANES), BF16)
        pv = [jnp.dot(pb, jnp.concatenate([v_ref[span, :].astype(BF16), ones], axis=1),
                      preferred_element_type=F32) for v_ref, pb in zip(v_refs, p)]
        return [(mn, al * st[1] + r[:, LANES:], al * st[2] + r[:, :LANES])
                for st, r, al, mn in zip(states, pv, alpha, m_new)]

    npair = qi + 1
    win_pairs = WINDOW // pair + 1

    for ref, init in zip((m_ref, l_ref, acc_ref), fresh):
        ref[...] = init

    def far_step(j, ntile):
        state, = advance([(m_ref[...], l_ref[...], acc_ref[...])], [scores(q_sel, ks_ref, j, ntile, None)],
                         [vs_ref], j, ntile)
        for ref, val in zip((m_ref, l_ref, acc_ref), state):
            ref[...] = val

    nfar = jnp.maximum(npair - win_pairs, 0)
    odd = jnp.bitwise_and(nfar, 1)
    pl.when(odd == 1)(lambda: far_step(0, 1))

    def far_body(i, carry):
        far_step(odd + 2 * i, 2)
        return carry

    lax.fori_loop(0, jnp.right_shift(nfar, 1), far_body, 0)

    def near(count):
        steps = ([(npair - count, count - 1)] if count > 1 else []) + [(npair - 1, 1)]
        window_edge = (lambda sc: jnp.concatenate([in_window(sc[:, :pair]), sc[:, pair:]], axis=1))
        before_diag_mask = window_edge if count == win_pairs else None
        last = len(steps) - 1
        s_sel = [scores(q_sel, ks_ref, j, n, causal if i == last else None) for i, (j, n) in enumerate(steps)]
        s_win = [scores(q_win, kw_ref, j, n, causal if i == last else before_diag_mask)
                 for i, (j, n) in enumerate(steps)]
        st_sel, st_win = (m_ref[...], l_ref[...], acc_ref[...]), fresh
        for (j, n), ts, tw in zip(steps, s_sel, s_win):
            st_sel, st_win = advance([st_sel, st_win], [ts, tw], [vs_ref, vw_ref], j, n)
        acc_ref[...] = st_sel[2] / st_sel[1]
        acc2_ref[...] = st_win[2] / st_win[1]

    pl.when(npair >= win_pairs)(functools.partial(near, win_pairs))
    for count in range(1, win_pairs):
        pl.when(npair == count)(functools.partial(near, count))

    o_slc = acc_ref[...]
    o_win = acc2_ref[...]

    ngate = NSA_HPG * 3
    erow = lax.broadcasted_iota(jnp.int32, (LANES, LANES), 0)
    elane = lax.broadcasted_iota(jnp.int32, (LANES, LANES), 1)
    pick_gate = (erow == g * ngate + elane).astype(F32)
    gates = _hdot(_sigmoid(gl_ref[...]), pick_gate)
    for hp in range(NSA_HPG):
        rs = slice(hp * tq, (hp + 1) * tq)
        o = (gates[:, 3 * hp:3 * hp + 1] * o_cmp[rs] + gates[:, 3 * hp + 1:3 * hp + 2] * o_slc[rs]
             + gates[:, 3 * hp + 2:3 * hp + 3] * o_win[rs])
        o_ref[:, hp * HEAD_DIM:(hp + 1) * HEAD_DIM] = o.astype(o_ref.dtype)


def _nsa_attention(z, zs, k_cmp, v_cmp, batch, seq):
    tq = ATT_TILE
    hq = NSA_HPG * HEAD_DIM
    nblk = seq // CMP_STRIDE
    assert ATT_TILE == KEY_TILE and seq % KEY_TILE == 0 and seq // SEL_BLOCK <= KAUG_POS_LANE
    assert WINDOW % KEY_TILE == 0
    kv = lambda base: pl.BlockSpec((None, seq, HEAD_DIM), lambda b, g, i, base=base: (b, 0, base + g))
    cmp_spec = pl.BlockSpec((None, None, nblk, HEAD_DIM), lambda b, g, i: (b, g, 0, 0))
    return pl.pallas_call(
        functools.partial(_nsa_kernel, seq=seq),
        grid=(batch, NSA_GROUPS, seq // tq),
        in_specs=[pl.BlockSpec((None, tq, hq), lambda b, g, i: (b, i, g)),
                  cmp_spec, cmp_spec, kv(24), kv(28), kv(32), kv(36),
                  pl.BlockSpec((None, tq, LANES), lambda b, g, i: (b, i, 0)),
                  pl.BlockSpec((seq, LANES), lambda b, g, i: (0, 0)),
                  pl.BlockSpec((nblk, LANES), lambda b, g, i: (0, 0))],
        out_specs=pl.BlockSpec((None, tq, hq), lambda b, g, i: (b, i, g)),
        out_shape=jax.ShapeDtypeStruct((batch, seq, NSA_HEADS * HEAD_DIM), BF16),
        scratch_shapes=[pltpu.VMEM((NSA_HPG * tq, LANES), F32)] * 4,
        compiler_params=_params(("parallel", "parallel", "arbitrary")),
        name="nsa_attention",
    )(z, k_cmp, v_cmp, z, z, z, z, zs, _key_aug(np.arange(seq), True),
      _key_aug(np.arange(nblk) * CMP_STRIDE + CMP_LEN - 1, False))


def _key_aug(kpos, with_block):
    aug = np.zeros((len(kpos), LANES), np.float32)
    if with_block:
        aug[np.arange(len(kpos)), kpos // SEL_BLOCK] = 1.0
    for c in range(3):
        aug[:, KAUG_POS_LANE + 2 * c] = kpos - kpos % 256
        aug[:, KAUG_POS_LANE + 2 * c + 1] = kpos % 256
    return jnp.asarray(aug, BF16)


def _w_in_tail(w_t, layer, n_main):
    tail = w_t[layer, n_main:, :]
    return jnp.pad(tail, ((0, LANES - tail.shape[0]), (0, 0)))


def _pad_lanes(v):
    return jnp.pad(v.astype(F32), (0, LANES - v.shape[0])).reshape(1, LANES)


def _even_layer(h, g_in, g_out, w_in, j, lb, conv_w, a_log, dt_bias, hgrn_g, gdn_g, w_out_bf, batch, seq):
    n_main = 8 * EVEN_WIDTH
    z, zs = _proj_in(h, g_in, w_in, j, _w_in_tail(w_in, j, n_main), jnp.ones((1, n_main), F32))
    o = _even_mixer(z, zs, lb.reshape(1, EVEN_WIDTH), hgrn_g.reshape(1, HEAD_DIM).astype(F32),
                    conv_w.astype(F32), _pad_lanes(a_log), _pad_lanes(dt_bias),
                    gdn_g.reshape(1, HEAD_DIM).astype(F32), batch, seq)
    return _proj_out(o, w_out_bf, j, h, g_out)


def _odd_layer(h, g_in, g_out, w_in, j, cmp_pos, cmp_w1, cmp_w2, w_out_bf, batch, seq):
    n_q = NSA_HEADS * HEAD_DIM
    n_main = n_q + 6 * NSA_GROUPS * HEAD_DIM
    col_scale = jnp.where(jnp.arange(n_main) < n_q, HEAD_DIM ** -0.5 * LOG2_E, 1.0).astype(F32).reshape(1, n_main)
    z, zs = _proj_in(h, g_in, w_in, j, _w_in_tail(w_in, j, n_main), col_scale)
    z3 = z.reshape(batch, seq, n_main)
    k_cmp, v_cmp = _compress(z3, cmp_pos.astype(F32), cmp_w1.astype(BF16), cmp_w2.astype(BF16), batch, seq)
    o = _nsa_attention(z3, zs.reshape(batch, seq, LANES), k_cmp, v_cmp, batch, seq)
    return _proj_out(o.reshape(batch * seq, -1), w_out_bf, j, h, g_out)


def kernel(x, norm_g, ab_w_in, hgrn_lb_logits, gdn_conv_w, gdn_a_log, gdn_dt_bias, hgrn_norm_g, gdn_norm_g,
           ab_w_out, nsa_w_in, nsa_cmp_pos, nsa_cmp_w1, nsa_cmp_w2, nsa_w_out, ffn_w_gu, ffn_w_down):
    batch, seq, d = x.shape
    depth = norm_g.shape[0]
    lb_all = jnp.cumsum(jax.nn.softmax(hgrn_lb_logits.astype(F32), axis=0), axis=0)
    lb_all = lb_all - lb_all[:1]
    g = norm_g.astype(F32).reshape(depth, 4, 1, d)
    ab_w_out_bf, nsa_w_out_bf = ab_w_out.astype(BF16), nsa_w_out.astype(BF16)
    ab_w_in_t, nsa_w_in_t = jnp.swapaxes(ab_w_in, 1, 2), jnp.swapaxes(nsa_w_in, 1, 2)
    h = x.reshape(batch * seq, d)
    for layer in range(depth):
        j = layer // 2
        if layer % 2 == 0:
            h = _even_layer(h, g[layer, 0], g[layer, 1], ab_w_in_t, j, lb_all[j], gdn_conv_w[j],
                            gdn_a_log[j], gdn_dt_bias[j], hgrn_norm_g[j], gdn_norm_g[j], ab_w_out_bf, batch, seq)
        else:
            h = _odd_layer(h, g[layer, 0], g[layer, 1], nsa_w_in_t, j, nsa_cmp_pos[j],
                           nsa_cmp_w1[j], nsa_cmp_w2[j], nsa_w_out_bf, batch, seq)
        h = _ffn(h, g[layer, 2], ffn_w_gu, ffn_w_down, layer, g[layer, 3])
    return h.reshape(batch, seq, d)
```

```python
import functools

import numpy as np
import jax
import jax.numpy as jnp
from jax import lax
from jax.experimental import pallas as pl
from jax.experimental.pallas import tpu as pltpu

F32 = jnp.float32
BF16 = jnp.bfloat16
HIGHEST = lax.Precision.HIGHEST

NORM_EPS = 1e-6
MASK_VALUE = -1e30

LANES = 128
HEAD_DIM = 128
CHUNK = 64
HGRN_BAND = 16
CONV_K = 4
EVEN_HEADS = 8
EVEN_WIDTH = EVEN_HEADS * HEAD_DIM

NSA_GROUPS = 4
NSA_HPG = 4
NSA_HEADS = NSA_GROUPS * NSA_HPG
CMP_LEN = 32
CMP_STRIDE = 16
SEL_BLOCK = 64
N_SEL = 8
WINDOW = 512
SEL_FORCE = 1e4
ATT_TILE = 256
KEY_TILE = 256
KAUG_POS_LANE = 32
SEL_MASK_BIG = 2.0 ** 100
LOG2_E = float(np.log2(np.e))

VMEM_LIMIT = 56 * 1024 * 1024


def _params(semantics):
    return pltpu.CompilerParams(dimension_semantics=semantics, vmem_limit_bytes=VMEM_LIMIT)


def _rms(x, g):
    return x * lax.rsqrt(jnp.mean(x * x, axis=-1, keepdims=True) + NORM_EPS) * g


def _sigmoid(x):
    return 1.0 / (1.0 + jnp.exp(-x))


def _silu(x):
    return x * _sigmoid(x)


def _bdot(a, b):
    return jnp.dot(a.astype(BF16), b.astype(BF16), preferred_element_type=F32)


def _bdot_nt(a, b):
    return lax.dot_general(a.astype(BF16), b.astype(BF16), (((1,), (1,)), ((), ())),
                           preferred_element_type=F32)


def _hdot(a, b):
    return jnp.dot(a, b, precision=HIGHEST, preferred_element_type=F32)


def _split(x):
    hi = x.astype(BF16)
    return hi, (x - hi.astype(F32)).astype(BF16)


def _dot3(a, b):
    (a_hi, a_lo), (b_hi, b_lo) = a, b
    dot = lambda x, y: jnp.dot(x, y, preferred_element_type=F32)
    return dot(a_hi, b_hi) + (dot(a_lo, b_hi) + dot(a_hi, b_lo))


def _div_pow2(x, n):
    return jnp.right_shift(x, int(n).bit_length() - 1)


def _proj_in_kernel(h_ref, g_ref, w_ref, ws_ref, cs_ref, z_ref, zs_ref, xn_ref):
    @pl.when(pl.program_id(1) == 0)
    def _():
        xn = _rms(h_ref[...], g_ref[...]).astype(BF16)
        xn_ref[...] = xn
        zs_ref[...] = _bdot_nt(xn, ws_ref[...])

    z = _bdot_nt(xn_ref[...], w_ref[...])
    z_ref[...] = (z * cs_ref[...]).astype(z_ref.dtype)


def _proj_in(h, g, w_t, layer, ws_t, col_scale, tm=1024, tn=1024):
    m, d = h.shape
    n = col_scale.shape[1]
    assert m % tm == 0 and n % tn == 0
    return pl.pallas_call(
        _proj_in_kernel,
        grid=(m // tm, n // tn),
        in_specs=[
            pl.BlockSpec((tm, d), lambda i, j: (i, 0)),
            pl.BlockSpec((1, d), lambda i, j: (0, 0)),
            pl.BlockSpec((None, tn, d), lambda i, j: (layer, j, 0)),
            pl.BlockSpec((LANES, d), lambda i, j: (0, 0)),
            pl.BlockSpec((1, tn), lambda i, j: (0, j)),
        ],
        out_specs=[
            pl.BlockSpec((tm, tn), lambda i, j: (i, j)),
            pl.BlockSpec((tm, LANES), lambda i, j: (i, 0)),
        ],
        out_shape=[jax.ShapeDtypeStruct((m, n), BF16), jax.ShapeDtypeStruct((m, LANES), F32)],
        scratch_shapes=[pltpu.VMEM((tm, d), BF16)],
        compiler_params=_params(("parallel", "arbitrary")),
        name="proj_in",
    )(h, g, w_t, ws_t, col_scale)


def _proj_out_kernel(o_ref, w_ref, h_ref, g_ref, out_ref):
    m = jnp.dot(o_ref[...], w_ref[...], preferred_element_type=F32)
    out_ref[...] = h_ref[...] + _rms(m, g_ref[...])


def _proj_out(o, w, layer, h, g, tm=512):
    m, k = o.shape
    d = w.shape[2]
    return pl.pallas_call(
        _proj_out_kernel,
        grid=(m // tm,),
        in_specs=[
            pl.BlockSpec((tm, k), lambda i: (i, 0)),
            pl.BlockSpec((None, k, d), lambda i: (layer, 0, 0)),
            pl.BlockSpec((tm, d), lambda i: (i, 0)),
            pl.BlockSpec((1, d), lambda i: (0, 0)),
        ],
        out_specs=pl.BlockSpec((tm, d), lambda i: (i, 0)),
        out_shape=jax.ShapeDtypeStruct((m, d), F32),
        compiler_params=_params(("parallel",)),
        name="proj_out",
    )(o, w, h, g)


def _ffn_kernel(h_ref, gin_ref, wg_ref, wu_ref, wd_ref, gout_ref, out_ref, xn_ref):
    f = pl.program_id(1)

    @pl.when(f == 0)
    def _():
        xn_ref[...] = _rms(h_ref[...], gin_ref[...]).astype(BF16)
        out_ref[...] = jnp.zeros_like(out_ref)

    xn = xn_ref[...]
    gate = jnp.dot(xn, wg_ref[...].astype(BF16), preferred_element_type=F32)
    up = jnp.dot(xn, wu_ref[...].astype(BF16), preferred_element_type=F32)
    act = (_silu(gate) * up).astype(BF16)
    out_ref[...] += jnp.dot(act, wd_ref[...].astype(BF16), preferred_element_type=F32)

    @pl.when(f == pl.num_programs(1) - 1)
    def _():
        out_ref[...] = h_ref[...] + _rms(out_ref[...], gout_ref[...])


def _ffn(h, g_in, w_gu, w_down, layer, g_out, tm=1024, tf=256):
    m, d = h.shape
    d_ff = w_down.shape[1]
    nf = d_ff // tf
    assert m % tm == 0 and d_ff % tf == 0
    return pl.pallas_call(
        _ffn_kernel,
        grid=(m // tm, nf),
        in_specs=[
            pl.BlockSpec((tm, d), lambda i, f: (i, 0), pipeline_mode=pl.Buffered(1)),
            pl.BlockSpec((1, d), lambda i, f: (0, 0)),
            pl.BlockSpec((None, d, tf), lambda i, f: (layer, 0, f)),
            pl.BlockSpec((None, d, tf), lambda i, f: (layer, 0, nf + f)),
            pl.BlockSpec((None, tf, d), lambda i, f: (layer, f, 0)),
            pl.BlockSpec((1, d), lambda i, f: (0, 0)),
        ],
        out_specs=pl.BlockSpec((tm, d), lambda i, f: (i, 0)),
        out_shape=jax.ShapeDtypeStruct((m, d), F32),
        scratch_shapes=[pltpu.VMEM((tm, d), BF16)],
        compiler_params=_params(("parallel", "arbitrary")),
        name="ffn",
    )(h, g_in, w_gu, w_gu, w_down, g_out)


def _hgrn_chunk(q_ref, f_ref, i_ref, gate_ref, lb_ref, ng_ref, sel_ref, o_ref, st_ref, bpad_ref, kpad_ref):
    band = HGRN_BAND

    r64 = lax.broadcasted_iota(jnp.int32, (CHUNK, CHUNK), 0)
    c64 = lax.broadcasted_iota(jnp.int32, (CHUNK, CHUNK), 1)
    tril = (r64 >= c64).astype(F32)
    lb = lb_ref[...]
    f_all = lb + (1.0 - lb) * _sigmoid(f_ref[...].astype(F32))
    b_all = _hdot(tril, jnp.log(f_all) * LOG2_E)
    bpad_ref[band:band + CHUNK, :] = b_all
    kpad_ref[band:band + CHUNK, :] = 1.0 - f_all
    yield

    for hd in range(EVEN_HEADS):
        sl = slice(hd * HEAD_DIM, (hd + 1) * HEAD_DIM)
        q = q_ref[:, sl].astype(F32)
        b = b_all[:, sl]
        k = kpad_ref[band:band + CHUNK, sl]
        v = i_ref[:, sl].astype(F32)

        b_ext = bpad_ref[:, sl]
        k_ext = kpad_ref[:, sl]
        terms = []
        for d in range(band):
            bs = b if d == 0 else pltpu.roll(b_ext, d, 0)[band:, :]
            ks = k if d == 0 else pltpu.roll(k_ext, d, 0)[band:, :]
            terms.append((jnp.exp2(b - bs) * (q * ks)).astype(BF16))
        diag = jnp.dot(jnp.concatenate(terms, axis=1), sel_ref[...], preferred_element_type=F32)
        a = pltpu.roll(diag, 1, 1, stride=1, stride_axis=0)

        far = [jnp.zeros((band, CHUNK), F32)]
        for blk in range(1, CHUNK // band):
            r0 = blk * band
            br = b[r0:r0 + 1, :]
            qi = q[r0:r0 + band, :] * jnp.exp2(b[r0:r0 + band, :] - br)
            ki = k * jnp.exp2(jnp.minimum(br - b, 0.0))
            far.append(_bdot_nt(qi, ki))
        a_far = jnp.where(r64 - c64 >= band, jnp.concatenate(far, axis=0), 0.0)

        st = st_ref[hd]
        o = _bdot(a[:, :CHUNK] + a_far, v) + _bdot_nt(q * jnp.exp2(b), st)
        b_end = b[CHUNK - 1:CHUNK, :]
        kd = k * jnp.exp2(b_end - b)
        st_ref[hd] = jnp.exp2(b_end) * st + _bdot(v.T, kd)
        o_ref[:, sl] = (_rms(o, ng_ref[...]) * _silu(gate_ref[:, sl].astype(F32))).astype(o_ref.dtype)
        yield


def _unit_lower_inverses(l_mats):
    r = lax.broadcasted_iota(jnp.int32, (CHUNK, CHUNK), 0)
    c = lax.broadcasted_iota(jnp.int32, (CHUNK, CHUNK), 1)
    eye = (r == c).astype(F32)
    same16 = _div_pow2(r, 16) == _div_pow2(c, 16)
    same32 = _div_pow2(r, 32) == _div_pow2(c, 32)
    m = [_split(jnp.where(same16, -l, 0.0)) for l in l_mats]
    p = [eye + jnp.where(same16, -l, 0.0) for l in l_mats]
    for _ in range(3):
        m = [_split(_dot3(x, x)) for x in m]
        yield
        p = [y + _dot3(_split(y), x) for y, x in zip(p, m)]
        yield
    for off_diag in (same32 & jnp.logical_not(same16), jnp.logical_not(same32)):
        ps = [_split(y) for y in p]
        t = [_dot3(ys, _split(jnp.where(off_diag, l, 0.0))) for ys, l in zip(ps, l_mats)]
        yield
        p = [y - _dot3(_split(x), ys) for y, ys, x in zip(p, ps, t)]
        yield
    return p


def _gdn_chunk(xq_ref, xk_ref, xv_ref, gate_ref, zs_ref, cw_ref, alog_ref, dtb_ref, ng_ref,
               o_ref, s_ref, xbuf_ref):
    w = EVEN_WIDTH

    conv = []
    for a, x_ref in enumerate((xq_ref, xk_ref, xv_ref)):
        x = x_ref[...].astype(F32)
        xbuf_ref[a, 8:8 + CHUNK, :] = x
        x_ext = xbuf_ref[a]
        y = x * cw_ref[CONV_K - 1:CONV_K, a * w:(a + 1) * w]
        for j in range(CONV_K - 1):
            lag = CONV_K - 1 - j
            y = y + pltpu.roll(x_ext, lag, 0)[8:, :] * cw_ref[j:j + 1, a * w:(a + 1) * w]
        xbuf_ref[a, 0:8, :] = x[CHUNK - 8:, :]
        conv.append(_silu(y))
        yield
    cq, ck, cv = conv

    zs = zs_ref[...]
    log_decay = -jnp.exp(alog_ref[...]) * jax.nn.softplus(zs + dtb_ref[...])
    r64 = lax.broadcasted_iota(jnp.int32, (CHUNK, CHUNK), 0)
    c64 = lax.broadcasted_iota(jnp.int32, (CHUNK, CHUNK), 1)
    tril = r64 >= c64
    gcum = _hdot(tril.astype(F32), log_decay)
    gcum_t = gcum.T
    beta_all = _sigmoid(zs)

    heads = range(EVEN_HEADS)
    sls = [slice(hd * HEAD_DIM, (hd + 1) * HEAD_DIM) for hd in heads]
    l2 = lambda x: x * lax.rsqrt(jnp.sum(x * x, axis=-1, keepdims=True) + NORM_EPS)
    q = [l2(cq[:, sl]) * (HEAD_DIM ** -0.5) for sl in sls]
    k = [l2(ck[:, sl]) for sl in sls]
    v = [cv[:, sl] for sl in sls]
    yield
    g_col = [gcum[:, hd:hd + 1] for hd in heads]
    beta = [beta_all[:, EVEN_HEADS + hd:EVEN_HEADS + hd + 1] for hd in heads]
    decay = [jnp.exp(jnp.where(tril, g_col[hd] - gcum_t[hd:hd + 1, :], -jnp.inf)) for hd in heads]
    kb = [k[hd] * beta[hd] for hd in heads]
    l_mat = [jnp.where(r64 > c64, _bdot_nt(kb[hd], k[hd]) * decay[hd], 0.0) for hd in heads]
    yield
    qk = [_bdot_nt(q[hd], k[hd]) * decay[hd] for hd in heads]
    eg = [jnp.exp(g_col[hd]) for hd in heads]
    rhs = [jnp.concatenate([v[hd] * beta[hd], kb[hd] * eg[hd]], axis=-1) for hd in heads]
    inv = yield from _unit_lower_inverses(l_mat)
    sol = [_dot3(_split(inv[hd]), _split(rhs[hd])) for hd in heads]
    yield
    s = [s_ref[hd] for hd in heads]
    v_new = [sol[hd][:, :HEAD_DIM] - _bdot(sol[hd][:, HEAD_DIM:], s[hd]) for hd in heads]
    yield
    o = [_bdot(q[hd] * eg[hd], s[hd]) + _bdot(qk[hd], v_new[hd]) for hd in heads]
    yield
    for hd in heads:
        g_end = g_col[hd][CHUNK - 1:CHUNK, :]
        kd = k[hd] * jnp.exp(g_end - g_col[hd])
        s_ref[hd] = jnp.exp(g_end) * s[hd] + _bdot(kd.T, v_new[hd])
        o_ref[:, sls[hd]] = (_rms(o[hd], ng_ref[...])
                             * _silu(gate_ref[:, sls[hd]].astype(F32))).astype(o_ref.dtype)
        if hd % 2:
            yield


def _even_mixer_kernel(hq_ref, hf_ref, hi_ref, hgate_ref, xq_ref, xk_ref, xv_ref, xgate_ref, zs_ref,
                       lb_ref, hng_ref, cw_ref, alog_ref, dtb_ref, gng_ref, sel_ref, o_ref,
                       st_ref, bpad_ref, kpad_ref, s_ref, xbuf_ref):
    w = EVEN_WIDTH

    @pl.when(pl.program_id(1) == 0)
    def _():
        st_ref[...] = jnp.zeros_like(st_ref)
        s_ref[...] = jnp.zeros_like(s_ref)
        bpad_ref[0:HGRN_BAND, :] = jnp.zeros((HGRN_BAND, w), F32)
        kpad_ref[0:HGRN_BAND, :] = jnp.zeros((HGRN_BAND, w), F32)
        xbuf_ref[:, 0:8, :] = jnp.zeros((3, 8, w), F32)

    hgrn = _hgrn_chunk(hq_ref, hf_ref, hi_ref, hgate_ref, lb_ref, hng_ref, sel_ref, o_ref.at[:, 0:w],
                       st_ref, bpad_ref, kpad_ref)
    gdn = _gdn_chunk(xq_ref, xk_ref, xv_ref, xgate_ref, zs_ref, cw_ref, alog_ref, dtb_ref, gng_ref,
                     o_ref.at[:, w:2 * w], s_ref, xbuf_ref)
    pending = [gdn, gdn, hgrn]
    while pending:
        for gen in list(pending):
            if gen in pending and next(gen, StopIteration) is StopIteration:
                pending = [x for x in pending if x is not gen]


def _even_mixer(z, zs, lb, hgrn_g, conv_w, alog, dtb, gdn_g, batch, seq):
    nchunk = seq // CHUNK
    w = EVEN_WIDTH
    col = lambda c: pl.BlockSpec((CHUNK, w), lambda b, n, c=c: (b * nchunk + n, c))
    vec = lambda width: pl.BlockSpec((1, width), lambda b, n: (0, 0))
    state = pltpu.VMEM((EVEN_HEADS, HEAD_DIM, HEAD_DIM), F32)
    return pl.pallas_call(
        _even_mixer_kernel,
        grid=(batch, nchunk),
        in_specs=[col(c) for c in range(8)] + [
            pl.BlockSpec((CHUNK, LANES), lambda b, n: (b * nchunk + n, 0)),
            vec(w), vec(HEAD_DIM),
            pl.BlockSpec((CONV_K, 3 * w), lambda b, n: (0, 0)),
            vec(LANES), vec(LANES), vec(HEAD_DIM),
            pl.BlockSpec((HGRN_BAND * LANES, LANES), lambda b, n: (0, 0))],
        out_specs=pl.BlockSpec((CHUNK, 2 * w), lambda b, n: (b * nchunk + n, 0)),
        out_shape=jax.ShapeDtypeStruct((batch * seq, 2 * w), BF16),
        scratch_shapes=[state, pltpu.VMEM((HGRN_BAND + CHUNK, w), F32), pltpu.VMEM((HGRN_BAND + CHUNK, w), F32),
                        state, pltpu.VMEM((3, 8 + CHUNK, w), F32)],
        compiler_params=_params(("parallel", "arbitrary")),
        name="even_mixer",
    )(*([z] * 8), zs, lb, hgrn_g, conv_w, alog, dtb, gdn_g, _band_select())


def _band_select():
    sel = np.zeros((HGRN_BAND, LANES, LANES), np.float32)
    for d in range(HGRN_BAND):
        sel[d, :, LANES - 1 - d] = 1.0
    return jnp.asarray(sel.reshape(HGRN_BAND * LANES, LANES), BF16)


def _compress_kernel(kc_ref, vc_ref, pos_ref, w1_ref, w2_ref, ko_ref, vo_ref, x_ref):
    nblk = ko_ref.shape[0]
    row = lax.broadcasted_iota(jnp.int32, (nblk, HEAD_DIM), 0)
    half = CMP_STRIDE * HEAD_DIM
    for a, (src_ref, out_ref) in enumerate(((kc_ref, ko_ref), (vc_ref, vo_ref))):
        x_ref[...] = src_ref[...].astype(F32)
        p1 = jnp.zeros((nblk, HEAD_DIM), F32)
        p2 = jnp.zeros((nblk, HEAD_DIM), F32)
        for l in range(CMP_STRIDE):
            rows = x_ref[pl.ds(l, nblk, stride=CMP_STRIDE), :]
            p1 = p1 + _bdot(rows + pos_ref[a, l:l + 1, :], w1_ref[a, l * HEAD_DIM:(l + 1) * HEAD_DIM, :])
            p2 = p2 + _bdot(rows + pos_ref[a, CMP_STRIDE + l:CMP_STRIDE + l + 1, :],
                            w1_ref[a, half + l * HEAD_DIM:half + (l + 1) * HEAD_DIM, :])
        pre = p1 + pltpu.roll(p2, nblk - 1, 0)
        out = _bdot(_silu(pre), w2_ref[a])
        out_ref[...] = jnp.where(row < nblk - 1, out, 0.0)


def _compress(z, pos, w1, w2, batch, seq):
    nblk = seq // CMP_STRIDE
    src = lambda base: pl.BlockSpec((None, seq, HEAD_DIM), lambda b, g, base=base: (b, 0, base + g))
    out = pl.BlockSpec((None, None, nblk, HEAD_DIM), lambda b, g: (b, g, 0, 0))
    full = lambda a: pl.BlockSpec(a.shape, lambda b, g: (0,) * a.ndim)
    shape = jax.ShapeDtypeStruct((batch, NSA_GROUPS, nblk, HEAD_DIM), F32)
    return pl.pallas_call(
        _compress_kernel,
        grid=(batch, NSA_GROUPS),
        in_specs=[src(16), src(20), full(pos), full(w1), full(w2)],
        out_specs=[out, out],
        out_shape=[shape, shape],
        scratch_shapes=[pltpu.VMEM((seq, HEAD_DIM), F32)],
        compiler_params=_params(("parallel", "parallel")),
        name="nsa_compress",
    )(z, z, pos, w1, w2)


def _nsa_kernel(q_ref, kc_ref, vc_ref, ks_ref, vs_ref, kw_ref, vw_ref, gl_ref, kaug_ref, caug_ref, o_ref,
                m_ref, l_ref, acc_ref, acc2_ref, *, seq):
    g = pl.program_id(1)
    qi = pl.program_id(2)
    tq = ATT_TILE
    rows = NSA_HPG * tq
    q0 = qi * tq
    ncmp = seq // CMP_STRIDE
    nblk = seq // SEL_BLOCK

    qb = jnp.concatenate([q_ref[:, hp * HEAD_DIM:(hp + 1) * HEAD_DIM] for hp in range(NSA_HPG)], axis=0)

    rrow = lax.broadcasted_iota(jnp.int32, (rows, LANES), 0)
    rlane = lax.broadcasted_iota(jnp.int32, (rows, LANES), 1)
    hp_idx = _div_pow2(rrow, tq)
    tpos = q0 + (rrow - hp_idx * tq)
    head = (g * NSA_HPG + hp_idx).astype(F32)
    slope = jnp.exp((-8.0 * np.log(2.0) / NSA_HEADS) * (head + 1.0)) * LOG2_E

    s_hi = slope.astype(BF16).astype(F32)
    s_mid = (slope - s_hi).astype(BF16).astype(F32)
    s_lo = slope - s_hi - s_mid
    piece = jnp.where(rlane < KAUG_POS_LANE + 2, s_hi, jnp.where(rlane < KAUG_POS_LANE + 4, s_mid, s_lo))
    bias_cols = jnp.where((rlane >= KAUG_POS_LANE) & (rlane < KAUG_POS_LANE + 6), piece, 0.0)
    q_win = jnp.concatenate([qb, bias_cols.astype(BF16)], axis=1)

    cpos = lax.broadcasted_iota(jnp.int32, (rows, ncmp), 1) * CMP_STRIDE + (CMP_LEN - 1)
    valid_c = tpos[:, :ncmp] >= cpos
    kc_aug = jnp.concatenate([kc_ref[...].astype(BF16), caug_ref[...]], axis=1)
    s_c = lax.dot_general(q_win, kc_aug, (((1,), (1,)), ((), ())), preferred_element_type=F32)
    s_c = jnp.where(valid_c, s_c, MASK_VALUE)
    m_c = jnp.max(s_c, axis=-1, keepdims=True)
    e_c = jnp.where(valid_c, jnp.exp2(s_c - m_c), 0.0)
    den = jnp.sum(e_c, axis=-1, keepdims=True)
    p_c = e_c / jnp.where(den > 0.0, den, 1.0)
    o_cmp = _bdot(p_c, vc_ref[...])

    p_sum = p_c[0:tq] + p_c[tq:2 * tq] + p_c[2 * tq:3 * tq] + p_c[3 * tq:4 * tq]
    ob = lax.broadcasted_iota(jnp.int32, (KAUG_POS_LANE, ncmp), 0) * SEL_BLOCK
    oj = lax.broadcasted_iota(jnp.int32, (KAUG_POS_LANE, ncmp), 1) * CMP_STRIDE
    overlap_t = ((oj < ob + SEL_BLOCK) & (oj + CMP_LEN > ob)).astype(F32)
    imp_t = lax.dot_general(overlap_t, p_sum, (((1,), (1,)), ((), ())), precision=HIGHEST,
                            preferred_element_type=F32)
    blk = lax.broadcasted_iota(jnp.int32, (KAUG_POS_LANE, tq), 0)
    cur = _div_pow2(q0 + lax.broadcasted_iota(jnp.int32, (KAUG_POS_LANE, tq), 1), SEL_BLOCK)
    forced = (blk == 0) | (blk == cur) | (blk == cur - 1)
    score = jnp.where(forced, SEL_FORCE, jnp.where(blk > cur, -SEL_FORCE, imp_t))
    score = jnp.where(blk < nblk, score, -jnp.inf)
    ahead = jnp.zeros((KAUG_POS_LANE, tq), F32)
    for j in range(nblk):
        sj = score[j:j + 1, :]
        wins_tie = jnp.where(blk > j, 1.0, 0.0)
        ahead = ahead + jnp.where(sj > score, 1.0, jnp.where(sj == score, wins_tie, 0.0))
    sel_t = jnp.where(ahead < float(min(N_SEL, nblk)), 1.0, 0.0)
    sel_t = jnp.concatenate([sel_t, jnp.zeros((LANES - KAUG_POS_LANE, tq), F32)], axis=0)
    sel = sel_t.T
    sel_cols = (jnp.concatenate([sel] * NSA_HPG, axis=0) - 1.0) * SEL_MASK_BIG
    q_sel = jnp.concatenate([qb, jnp.where(rlane < KAUG_POS_LANE, sel_cols, bias_cols).astype(BF16)], axis=1)

    pair = KEY_TILE
    below_diag = (lax.broadcasted_iota(jnp.int32, (rows, pair), 1)
                  <= jnp.bitwise_and(lax.broadcasted_iota(jnp.int32, (rows, pair), 0), tq - 1))
    causal = lambda sc: jnp.where(below_diag, sc, MASK_VALUE)
    in_window = lambda sc: jnp.where(below_diag, MASK_VALUE, sc)
    fresh = (jnp.full((rows, LANES), MASK_VALUE, F32), jnp.zeros((rows, LANES), F32),
             jnp.zeros((rows, LANES), F32))

    def key_span(j, ntile):
        k0 = j * pair if isinstance(j, int) else pl.multiple_of(j * pair, pair)
        return pl.ds(k0, ntile * pair)

    def scores(qa, k_ref, j, ntile, mask):
        span = key_span(j, ntile)
        kp = jnp.concatenate([k_ref[span, :].astype(BF16), kaug_ref[span, :]], axis=1)
        sc = lax.dot_general(qa, kp, (((1,), (1,)), ((), ())), preferred_element_type=F32)
        return sc if mask is None else mask(sc)

    def advance(states, tiles, v_refs, j, ntile):
        span = key_span(j, ntile)
        nchunk = ntile * pair // LANES
        chunks = [[sc[:, c * LANES:(c + 1) * LANES] for c in range(nchunk)] for sc in tiles]
        m_new = [jnp.maximum(st[0], jnp.max(functools.reduce(jnp.maximum, ch), axis=-1, keepdims=True))
                 for st, ch in zip(states, chunks)]
        p = [jnp.concatenate([jnp.exp2(c - mn) for c in ch], axis=1).astype(BF16) for ch, mn in zip(chunks, m_new)]
        alpha = [jnp.exp2(st[0] - mn) for st, mn in zip(states, m_new)]
        ones = jnp.ones((ntile * pair, LANES), BF16)
        pv = [jnp.dot(pb, jnp.concatenate([v_ref[span, :].astype(BF16), ones], axis=1),
                      preferred_element_type=F32) for v_ref, pb in zip(v_refs, p)]
        return [(mn, al * st[1] + r[:, LANES:], al * st[2] + r[:, :LANES])
                for st, r, al, mn in zip(states, pv, alpha, m_new)]

    npair = qi + 1
    win_pairs = WINDOW // pair + 1

    for ref, init in zip((m_ref, l_ref, acc_ref), fresh):
        ref[...] = init

    def far_step(j, ntile):
        state, = advance([(m_ref[...], l_ref[...], acc_ref[...])], [scores(q_sel, ks_ref, j, ntile, None)],
                         [vs_ref], j, ntile)
        for ref, val in zip((m_ref, l_ref, acc_ref), state):
            ref[...] = val

    nfar = jnp.maximum(npair - win_pairs, 0)
    odd = jnp.bitwise_and(nfar, 1)
    pl.when(odd == 1)(lambda: far_step(0, 1))

    def far_body(i, carry):
        far_step(odd + 2 * i, 2)
        return carry

    lax.fori_loop(0, jnp.right_shift(nfar, 1), far_body, 0)

    def near(count):
        steps = ([(npair - count, count - 1)] if count > 1 else []) + [(npair - 1, 1)]
        window_edge = (lambda sc: jnp.concatenate([in_window(sc[:, :pair]), sc[:, pair:]], axis=1))
        before_diag_mask = window_edge if count == win_pairs else None
        last = len(steps) - 1
        s_sel = [scores(q_sel, ks_ref, j, n, causal if i == last else None) for i, (j, n) in enumerate(steps)]
        s_win = [scores(q_win, kw_ref, j, n, causal if i == last else before_diag_mask)
                 for i, (j, n) in enumerate(steps)]
        st_sel, st_win = (m_ref[...], l_ref[...], acc_ref[...]), fresh
        for (j, n), ts, tw in zip(steps, s_sel, s_win):
            st_sel, st_win = advance([st_sel, st_win], [ts, tw], [vs_ref, vw_ref], j, n)
        acc_ref[...] = st_sel[2] / st_sel[1]
        acc2_ref[...] = st_win[2] / st_win[1]

    pl.when(npair >= win_pairs)(functools.partial(near, win_pairs))
    for count in range(1, win_pairs):
        pl.when(npair == count)(functools.partial(near, count))

    o_slc = acc_ref[...]
    o_win = acc2_ref[...]

    ngate = NSA_HPG * 3
    erow = lax.broadcasted_iota(jnp.int32, (LANES, LANES), 0)
    elane = lax.broadcasted_iota(jnp.int32, (LANES, LANES), 1)
    pick_gate = (erow == g * ngate + elane).astype(F32)
    gates = _hdot(_sigmoid(gl_ref[...]), pick_gate)
    for hp in range(NSA_HPG):
        rs = slice(hp * tq, (hp + 1) * tq)
        o = (gates[:, 3 * hp:3 * hp + 1] * o_cmp[rs] + gates[:, 3 * hp + 1:3 * hp + 2] * o_slc[rs]
             + gates[:, 3 * hp + 2:3 * hp + 3] * o_win[rs])
        o_ref[:, hp * HEAD_DIM:(hp + 1) * HEAD_DIM] = o.astype(o_ref.dtype)


def _nsa_attention(z, zs, k_cmp, v_cmp, batch, seq):
    tq = ATT_TILE
    hq = NSA_HPG * HEAD_DIM
    nblk = seq // CMP_STRIDE
    assert ATT_TILE == KEY_TILE and seq % KEY_TILE == 0 and seq // SEL_BLOCK <= KAUG_POS_LANE
    assert WINDOW % KEY_TILE == 0
    kv = lambda base: pl.BlockSpec((None, seq, HEAD_DIM), lambda b, g, i, base=base: (b, 0, base + g))
    cmp_spec = pl.BlockSpec((None, None, nblk, HEAD_DIM), lambda b, g, i: (b, g, 0, 0))
    return pl.pallas_call(
        functools.partial(_nsa_kernel, seq=seq),
        grid=(batch, NSA_GROUPS, seq // tq),
        in_specs=[pl.BlockSpec((None, tq, hq), lambda b, g, i: (b, i, g)),
                  cmp_spec, cmp_spec, kv(24), kv(28), kv(32), kv(36),
                  pl.BlockSpec((None, tq, LANES), lambda b, g, i: (b, i, 0)),
                  pl.BlockSpec((seq, LANES), lambda b, g, i: (0, 0)),
                  pl.BlockSpec((nblk, LANES), lambda b, g, i: (0, 0))],
        out_specs=pl.BlockSpec((None, tq, hq), lambda b, g, i: (b, i, g)),
        out_shape=jax.ShapeDtypeStruct((batch, seq, NSA_HEADS * HEAD_DIM), BF16),
        scratch_shapes=[pltpu.VMEM((NSA_HPG * tq, LANES), F32)] * 4,
        compiler_params=_params(("parallel", "parallel", "arbitrary")),
        name="nsa_attention",
    )(z, k_cmp, v_cmp, z, z, z, z, zs, _key_aug(np.arange(seq), True),
      _key_aug(np.arange(nblk) * CMP_STRIDE + CMP_LEN - 1, False))


def _key_aug(kpos, with_block):
    aug = np.zeros((len(kpos), LANES), np.float32)
    if with_block:
        aug[np.arange(len(kpos)), kpos // SEL_BLOCK] = 1.0
    for c in range(3):
        aug[:, KAUG_POS_LANE + 2 * c] = kpos - kpos % 256
        aug[:, KAUG_POS_LANE + 2 * c + 1] = kpos % 256
    return jnp.asarray(aug, BF16)


def _w_in_tail(w_t, layer, n_main):
    tail = w_t[layer, n_main:, :]
    return jnp.pad(tail, ((0, LANES - tail.shape[0]), (0, 0)))


def _pad_lanes(v):
    return jnp.pad(v.astype(F32), (0, LANES - v.shape[0])).reshape(1, LANES)


def _even_layer(h, g_in, g_out, w_in, j, lb, conv_w, a_log, dt_bias, hgrn_g, gdn_g, w_out_bf, batch, seq):
    n_main = 8 * EVEN_WIDTH
    z, zs = _proj_in(h, g_in, w_in, j, _w_in_tail(w_in, j, n_main), jnp.ones((1, n_main), F32))
    o = _even_mixer(z, zs, lb.reshape(1, EVEN_WIDTH), hgrn_g.reshape(1, HEAD_DIM).astype(F32),
                    conv_w.astype(F32), _pad_lanes(a_log), _pad_lanes(dt_bias),
                    gdn_g.reshape(1, HEAD_DIM).astype(F32), batch, seq)
    return _proj_out(o, w_out_bf, j, h, g_out)


def _odd_layer(h, g_in, g_out, w_in, j, cmp_pos, cmp_w1, cmp_w2, w_out_bf, batch, seq):
    n_q = NSA_HEADS * HEAD_DIM
    n_main = n_q + 6 * NSA_GROUPS * HEAD_DIM
    col_scale = jnp.where(jnp.arange(n_main) < n_q, HEAD_DIM ** -0.5 * LOG2_E, 1.0).astype(F32).reshape(1, n_main)
    z, zs = _proj_in(h, g_in, w_in, j, _w_in_tail(w_in, j, n_main), col_scale)
    z3 = z.reshape(batch, seq, n_main)
    k_cmp, v_cmp = _compress(z3, cmp_pos.astype(F32), cmp_w1.astype(BF16), cmp_w2.astype(BF16), batch, seq)
    o = _nsa_attention(z3, zs.reshape(batch, seq, LANES), k_cmp, v_cmp, batch, seq)
    return _proj_out(o.reshape(batch * seq, -1), w_out_bf, j, h, g_out)


def kernel(x, norm_g, ab_w_in, hgrn_lb_logits, gdn_conv_w, gdn_a_log, gdn_dt_bias, hgrn_norm_g, gdn_norm_g,
           ab_w_out, nsa_w_in, nsa_cmp_pos, nsa_cmp_w1, nsa_cmp_w2, nsa_w_out, ffn_w_gu, ffn_w_down):
    batch, seq, d = x.shape
    depth = norm_g.shape[0]
    lb_all = jnp.cumsum(jax.nn.softmax(hgrn_lb_logits.astype(F32), axis=0), axis=0)
    lb_all = lb_all - lb_all[:1]
    g = norm_g.astype(F32).reshape(depth, 4, 1, d)
    ab_w_out_bf, nsa_w_out_bf = ab_w_out.astype(BF16), nsa_w_out.astype(BF16)
    ab_w_in_t, nsa_w_in_t = jnp.swapaxes(ab_w_in, 1, 2), jnp.swapaxes(nsa_w_in, 1, 2)
    h = x.reshape(batch * seq, d)
    for layer in range(depth):
        j = layer // 2
        if layer % 2 == 0:
            h = _even_layer(h, g[layer, 0], g[layer, 1], ab_w_in_t, j, lb_all[j], gdn_conv_w[j],
                            gdn_a_log[j], gdn_dt_bias[j], hgrn_norm_g[j], gdn_norm_g[j], ab_w_out_bf, batch, seq)
        else:
            h = _odd_layer(h, g[layer, 0], g[layer, 1], nsa_w_in_t, j, nsa_cmp_pos[j],
                           nsa_cmp_w1[j], nsa_cmp_w2[j], nsa_w_out_bf, batch, seq)
        h = _ffn(h, g[layer, 2], ffn_w_gu, ffn_w_down, layer, g[layer, 3])
    return h.reshape(batch, seq, d)
```

```python
import functools

import numpy as np
import jax
import jax.numpy as jnp
from jax import lax
from jax.experimental import pallas as pl
from jax.experimental.pallas import tpu as pltpu

F32 = jnp.float32
BF16 = jnp.bfloat16
HIGHEST = lax.Precision.HIGHEST

NORM_EPS = 1e-6
MASK_VALUE = -1e30

LANES = 128
HEAD_DIM = 128
CHUNK = 64
HGRN_BAND = 16
CONV_K = 4
EVEN_HEADS = 8
EVEN_WIDTH = EVEN_HEADS * HEAD_DIM

NSA_GROUPS = 4
NSA_HPG = 4
NSA_HEADS = NSA_GROUPS * NSA_HPG
CMP_LEN = 32
CMP_STRIDE = 16
SEL_BLOCK = 64
N_SEL = 8
WINDOW = 512
SEL_FORCE = 1e4
ATT_TILE = 256
KEY_TILE = 256
KAUG_POS_LANE = 32
SEL_MASK_BIG = 2.0 ** 100
LOG2_E = float(np.log2(np.e))

VMEM_LIMIT = 56 * 1024 * 1024


def _params(semantics):
    return pltpu.CompilerParams(dimension_semantics=semantics, vmem_limit_bytes=VMEM_LIMIT)


def _rms(x, g):
    return x * lax.rsqrt(jnp.mean(x * x, axis=-1, keepdims=True) + NORM_EPS) * g


def _sigmoid(x):
    return 1.0 / (1.0 + jnp.exp(-x))


def _silu(x):
    return x * _sigmoid(x)


def _bdot(a, b):
    return jnp.dot(a.astype(BF16), b.astype(BF16), preferred_element_type=F32)


def _bdot_nt(a, b):
    return lax.dot_general(a.astype(BF16), b.astype(BF16), (((1,), (1,)), ((), ())),
                           preferred_element_type=F32)


def _hdot(a, b):
    return jnp.dot(a, b, precision=HIGHEST, preferred_element_type=F32)


def _split(x):
    hi = x.astype(BF16)
    return hi, (x - hi.astype(F32)).astype(BF16)


def _dot3(a, b):
    (a_hi, a_lo), (b_hi, b_lo) = a, b
    dot = lambda x, y: jnp.dot(x, y, preferred_element_type=F32)
    return dot(a_hi, b_hi) + (dot(a_lo, b_hi) + dot(a_hi, b_lo))


def _div_pow2(x, n):
    return jnp.right_shift(x, int(n).bit_length() - 1)


def _proj_in_kernel(h_ref, g_ref, w_ref, ws_ref, cs_ref, z_ref, zs_ref, xn_ref):
    @pl.when(pl.program_id(1) == 0)
    def _():
        xn = _rms(h_ref[...], g_ref[...]).astype(BF16)
        xn_ref[...] = xn
        zs_ref[...] = _bdot_nt(xn, ws_ref[...])

    z = _bdot_nt(xn_ref[...], w_ref[...])
    z_ref[...] = (z * cs_ref[...]).astype(z_ref.dtype)


def _proj_in(h, g, w_t, layer, ws_t, col_scale, tm=1024, tn=1024):
    m, d = h.shape
    n = col_scale.shape[1]
    assert m % tm == 0 and n % tn == 0
    return pl.pallas_call(
        _proj_in_kernel,
        grid=(m // tm, n // tn),
        in_specs=[
            pl.BlockSpec((tm, d), lambda i, j: (i, 0)),
            pl.BlockSpec((1, d), lambda i, j: (0, 0)),
            pl.BlockSpec((None, tn, d), lambda i, j: (layer, j, 0)),
            pl.BlockSpec((LANES, d), lambda i, j: (0, 0)),
            pl.BlockSpec((1, tn), lambda i, j: (0, j)),
        ],
        out_specs=[
            pl.BlockSpec((tm, tn), lambda i, j: (i, j)),
            pl.BlockSpec((tm, LANES), lambda i, j: (i, 0)),
        ],
        out_shape=[jax.ShapeDtypeStruct((m, n), BF16), jax.ShapeDtypeStruct((m, LANES), F32)],
        scratch_shapes=[pltpu.VMEM((tm, d), BF16)],
        compiler_params=_params(("parallel", "arbitrary")),
        name="proj_in",
    )(h, g, w_t, ws_t, col_scale)


def _proj_out_kernel(o_ref, w_ref, h_ref, g_ref, out_ref):
    m = jnp.dot(o_ref[...], w_ref[...], preferred_element_type=F32)
    out_ref[...] = h_ref[...] + _rms(m, g_ref[...])


def _proj_out(o, w, layer, h, g, tm=512):
    m, k = o.shape
    d = w.shape[2]
    return pl.pallas_call(
        _proj_out_kernel,
        grid=(m // tm,),
        in_specs=[
            pl.BlockSpec((tm, k), lambda i: (i, 0)),
            pl.BlockSpec((None, k, d), lambda i: (layer, 0, 0)),
            pl.BlockSpec((tm, d), lambda i: (i, 0)),
            pl.BlockSpec((1, d), lambda i: (0, 0)),
        ],
        out_specs=pl.BlockSpec((tm, d), lambda i: (i, 0)),
        out_shape=jax.ShapeDtypeStruct((m, d), F32),
        compiler_params=_params(("parallel",)),
        name="proj_out",
    )(o, w, h, g)


def _ffn_kernel(h_ref, gin_ref, wg_ref, wu_ref, wd_ref, gout_ref, out_ref, xn_ref):
    f = pl.program_id(1)

    @pl.when(f == 0)
    def _():
        xn_ref[...] = _rms(h_ref[...], gin_ref[...]).astype(BF16)
        out_ref[...] = jnp.zeros_like(out_ref)

    xn = xn_ref[...]
    gate = jnp.dot(xn, wg_ref[...].astype(BF16), preferred_element_type=F32)
    up = jnp.dot(xn, wu_ref[...].astype(BF16), preferred_element_type=F32)
    act = (_silu(gate) * up).astype(BF16)
    out_ref[...] += jnp.dot(act, wd_ref[...].astype(BF16), preferred_element_type=F32)

    @pl.when(f == pl.num_programs(1) - 1)
    def _():
        out_ref[...] = h_ref[...] + _rms(out_ref[...], gout_ref[...])


def _ffn(h, g_in, w_gu, w_down, layer, g_out, tm=1024, tf=256):
    m, d = h.shape
    d_ff = w_down.shape[1]
    nf = d_ff // tf
    assert m % tm == 0 and d_ff % tf == 0
    return pl.pallas_call(
        _ffn_kernel,
        grid=(m // tm, nf),
        in_specs=[
            pl.BlockSpec((tm, d), lambda i, f: (i, 0), pipeline_mode=pl.Buffered(1)),
            pl.BlockSpec((1, d), lambda i, f: (0, 0)),
            pl.BlockSpec((None, d, tf), lambda i, f: (layer, 0, f)),
            pl.BlockSpec((None, d, tf), lambda i, f: (layer, 0, nf + f)),
            pl.BlockSpec((None, tf, d), lambda i, f: (layer, f, 0)),
            pl.BlockSpec((1, d), lambda i, f: (0, 0)),
        ],
        out_specs=pl.BlockSpec((tm, d), lambda i, f: (i, 0)),
        out_shape=jax.ShapeDtypeStruct((m, d), F32),
        scratch_shapes=[pltpu.VMEM((tm, d), BF16)],
        compiler_params=_params(("parallel", "arbitrary")),
        name="ffn",
    )(h, g_in, w_gu, w_gu, w_down, g_out)


def _hgrn_chunk(q_ref, f_ref, i_ref, gate_ref, lb_ref, ng_ref, sel_ref, o_ref, st_ref, bpad_ref, kpad_ref):
    band = HGRN_BAND

    r64 = lax.broadcasted_iota(jnp.int32, (CHUNK, CHUNK), 0)
    c64 = lax.broadcasted_iota(jnp.int32, (CHUNK, CHUNK), 1)
    tril = (r64 >= c64).astype(F32)
    lb = lb_ref[...]
    f_all = lb + (1.0 - lb) * _sigmoid(f_ref[...].astype(F32))
    b_all = _hdot(tril, jnp.log(f_all) * LOG2_E)
    bpad_ref[band:band + CHUNK, :] = b_all
    kpad_ref[band:band + CHUNK, :] = 1.0 - f_all
    yield

    for hd in range(EVEN_HEADS):
        sl = slice(hd * HEAD_DIM, (hd + 1) * HEAD_DIM)
        q = q_ref[:, sl].astype(F32)
        b = b_all[:, sl]
        k = kpad_ref[band:band + CHUNK, sl]
        v = i_ref[:, sl].astype(F32)

        b_ext = bpad_ref[:, sl]
        k_ext = kpad_ref[:, sl]
        terms = []
        for d in range(band):
            bs = b if d == 0 else pltpu.roll(b_ext, d, 0)[band:, :]
            ks = k if d == 0 else pltpu.roll(k_ext, d, 0)[band:, :]
            terms.append((jnp.exp2(b - bs) * (q * ks)).astype(BF16))
        diag = jnp.dot(jnp.concatenate(terms, axis=1), sel_ref[...], preferred_element_type=F32)
        a = pltpu.roll(diag, 1, 1, stride=1, stride_axis=0)

        far = [jnp.zeros((band, CHUNK), F32)]
        for blk in range(1, CHUNK // band):
            r0 = blk * band
            br = b[r0:r0 + 1, :]
            qi = q[r0:r0 + band, :] * jnp.exp2(b[r0:r0 + band, :] - br)
            ki = k * jnp.exp2(jnp.minimum(br - b, 0.0))
            far.append(_bdot_nt(qi, ki))
        a_far = jnp.where(r64 - c64 >= band, jnp.concatenate(far, axis=0), 0.0)

        st = st_ref[hd]
        o = _bdot(a[:, :CHUNK] + a_far, v) + _bdot_nt(q * jnp.exp2(b), st)
        b_end = b[CHUNK - 1:CHUNK, :]
        kd = k * jnp.exp2(b_end - b)
        st_ref[hd] = jnp.exp2(b_end) * st + _bdot(v.T, kd)
        o_ref[:, sl] = (_rms(o, ng_ref[...]) * _silu(gate_ref[:, sl].astype(F32))).astype(o_ref.dtype)
        yield


def _unit_lower_inverses(l_mats):
    r = lax.broadcasted_iota(jnp.int32, (CHUNK, CHUNK), 0)
    c = lax.broadcasted_iota(jnp.int32, (CHUNK, CHUNK), 1)
    eye = (r == c).astype(F32)
    same16 = _div_pow2(r, 16) == _div_pow2(c, 16)
    same32 = _div_pow2(r, 32) == _div_pow2(c, 32)
    m = [_split(jnp.where(same16, -l, 0.0)) for l in l_mats]
    p = [eye + jnp.where(same16, -l, 0.0) for l in l_mats]
    for _ in range(3):
        m = [_split(_dot3(x, x)) for x in m]
        yield
        p = [y + _dot3(_split(y), x) for y, x in zip(p, m)]
        yield
    for off_diag in (same32 & jnp.logical_not(same16), jnp.logical_not(same32)):
        ps = [_split(y) for y in p]
        t = [_dot3(ys, _split(jnp.where(off_diag, l, 0.0))) for ys, l in zip(ps, l_mats)]
        yield
        p = [y - _dot3(_split(x), ys) for y, ys, x in zip(p, ps, t)]
        yield
    return p


def _gdn_chunk(xq_ref, xk_ref, xv_ref, gate_ref, zs_ref, cw_ref, alog_ref, dtb_ref, ng_ref,
               o_ref, s_ref, xbuf_ref):
    w = EVEN_WIDTH

    conv = []
    for a, x_ref in enumerate((xq_ref, xk_ref, xv_ref)):
        x = x_ref[...].astype(F32)
        xbuf_ref[a, 8:8 + CHUNK, :] = x
        x_ext = xbuf_ref[a]
        y = x * cw_ref[CONV_K - 1:CONV_K, a * w:(a + 1) * w]
        for j in range(CONV_K - 1):
            lag = CONV_K - 1 - j
            y = y + pltpu.roll(x_ext, lag, 0)[8:, :] * cw_ref[j:j + 1, a * w:(a + 1) * w]
        xbuf_ref[a, 0:8, :] = x[CHUNK - 8:, :]
        conv.append(_silu(y))
        yield
    cq, ck, cv = conv

    zs = zs_ref[...]
    log_decay = -jnp.exp(alog_ref[...]) * jax.nn.softplus(zs + dtb_ref[...])
    r64 = lax.broadcasted_iota(jnp.int32, (CHUNK, CHUNK), 0)
    c64 = lax.broadcasted_iota(jnp.int32, (CHUNK, CHUNK), 1)
    tril = r64 >= c64
    gcum = _hdot(tril.astype(F32), log_decay)
    gcum_t = gcum.T
    beta_all = _sigmoid(zs)

    heads = range(EVEN_HEADS)
    sls = [slice(hd * HEAD_DIM, (hd + 1) * HEAD_DIM) for hd in heads]
    l2 = lambda x: x * lax.rsqrt(jnp.sum(x * x, axis=-1, keepdims=True) + NORM_EPS)
    q = [l2(cq[:, sl]) * (HEAD_DIM ** -0.5) for sl in sls]
    k = [l2(ck[:, sl]) for sl in sls]
    v = [cv[:, sl] for sl in sls]
    yield
    g_col = [gcum[:, hd:hd + 1] for hd in heads]
    beta = [beta_all[:, EVEN_HEADS + hd:EVEN_HEADS + hd + 1] for hd in heads]
    decay = [jnp.exp(jnp.where(tril, g_col[hd] - gcum_t[hd:hd + 1, :], -jnp.inf)) for hd in heads]
    kb = [k[hd] * beta[hd] for hd in heads]
    l_mat = [jnp.where(r64 > c64, _bdot_nt(kb[hd], k[hd]) * decay[hd], 0.0) for hd in heads]
    yield
    qk = [_bdot_nt(q[hd], k[hd]) * decay[hd] for hd in heads]
    eg = [jnp.exp(g_col[hd]) for hd in heads]
    rhs = [jnp.concatenate([v[hd] * beta[hd], kb[hd] * eg[hd]], axis=-1) for hd in heads]
    inv = yield from _unit_lower_inverses(l_mat)
    sol = [_dot3(_split(inv[hd]), _split(rhs[hd])) for hd in heads]
    yield
    s = [s_ref[hd] for hd in heads]
    v_new = [sol[hd][:, :HEAD_DIM] - _bdot(sol[hd][:, HEAD_DIM:], s[hd]) for hd in heads]
    yield
    o = [_bdot(q[hd] * eg[hd], s[hd]) + _bdot(qk[hd], v_new[hd]) for hd in heads]
    yield
    for hd in heads:
        g_end = g_col[hd][CHUNK - 1:CHUNK, :]
        kd = k[hd] * jnp.exp(g_end - g_col[hd])
        s_ref[hd] = jnp.exp(g_end) * s[hd] + _bdot(kd.T, v_new[hd])
        o_ref[:, sls[hd]] = (_rms(o[hd], ng_ref[...])
                             * _silu(gate_ref[:, sls[hd]].astype(F32))).astype(o_ref.dtype)
        if hd % 2:
            yield


def _even_mixer_kernel(hq_ref, hf_ref, hi_ref, hgate_ref, xq_ref, xk_ref, xv_ref, xgate_ref, zs_ref,
                       lb_ref, hng_ref, cw_ref, alog_ref, dtb_ref, gng_ref, sel_ref, o_ref,
                       st_ref, bpad_ref, kpad_ref, s_ref, xbuf_ref):
    w = EVEN_WIDTH

    @pl.when(pl.program_id(1) == 0)
    def _():
        st_ref[...] = jnp.zeros_like(st_ref)
        s_ref[...] = jnp.zeros_like(s_ref)
        bpad_ref[0:HGRN_BAND, :] = jnp.zeros((HGRN_BAND, w), F32)
        kpad_ref[0:HGRN_BAND, :] = jnp.zeros((HGRN_BAND, w), F32)
        xbuf_ref[:, 0:8, :] = jnp.zeros((3, 8, w), F32)

    hgrn = _hgrn_chunk(hq_ref, hf_ref, hi_ref, hgate_ref, lb_ref, hng_ref, sel_ref, o_ref.at[:, 0:w],
                       st_ref, bpad_ref, kpad_ref)
    gdn = _gdn_chunk(xq_ref, xk_ref, xv_ref, xgate_ref, zs_ref, cw_ref, alog_ref, dtb_ref, gng_ref,
                     o_ref.at[:, w:2 * w], s_ref, xbuf_ref)
    pending = [gdn, gdn, hgrn]
    while pending:
        for gen in list(pending):
            if gen in pending and next(gen, StopIteration) is StopIteration:
                pending = [x for x in pending if x is not gen]


def _even_mixer(z, zs, lb, hgrn_g, conv_w, alog, dtb, gdn_g, batch, seq):
    nchunk = seq // CHUNK
    w = EVEN_WIDTH
    col = lambda c: pl.BlockSpec((CHUNK, w), lambda b, n, c=c: (b * nchunk + n, c))
    vec = lambda width: pl.BlockSpec((1, width), lambda b, n: (0, 0))
    state = pltpu.VMEM((EVEN_HEADS, HEAD_DIM, HEAD_DIM), F32)
    return pl.pallas_call(
        _even_mixer_kernel,
        grid=(batch, nchunk),
        in_specs=[col(c) for c in range(8)] + [
            pl.BlockSpec((CHUNK, LANES), lambda b, n: (b * nchunk + n, 0)),
            vec(w), vec(HEAD_DIM),
            pl.BlockSpec((CONV_K, 3 * w), lambda b, n: (0, 0)),
            vec(LANES), vec(LANES), vec(HEAD_DIM),
            pl.BlockSpec((HGRN_BAND * LANES, LANES), lambda b, n: (0, 0))],
        out_specs=pl.BlockSpec((CHUNK, 2 * w), lambda b, n: (b * nchunk + n, 0)),
        out_shape=jax.ShapeDtypeStruct((batch * seq, 2 * w), BF16),
        scratch_shapes=[state, pltpu.VMEM((HGRN_BAND + CHUNK, w), F32), pltpu.VMEM((HGRN_BAND + CHUNK, w), F32),
                        state, pltpu.VMEM((3, 8 + CHUNK, w), F32)],
        compiler_params=_params(("parallel", "arbitrary")),
        name="even_mixer",
    )(*([z] * 8), zs, lb, hgrn_g, conv_w, alog, dtb, gdn_g, _band_select())


def _band_select():
    sel = np.zeros((HGRN_BAND, LANES, LANES), np.float32)
    for d in range(HGRN_BAND):
        sel[d, :, LANES - 1 - d] = 1.0
    return jnp.asarray(sel.reshape(HGRN_BAND * LANES, LANES), BF16)


def _compress_kernel(kc_ref, vc_ref, pos_ref, w1_ref, w2_ref, ko_ref, vo_ref, x_ref):
    nblk = ko_ref.shape[0]
    row = lax.broadcasted_iota(jnp.int32, (nblk, HEAD_DIM), 0)
    half = CMP_STRIDE * HEAD_DIM
    for a, (src_ref, out_ref) in enumerate(((kc_ref, ko_ref), (vc_ref, vo_ref))):
        x_ref[...] = src_ref[...].astype(F32)
        p1 = jnp.zeros((nblk, HEAD_DIM), F32)
        p2 = jnp.zeros((nblk, HEAD_DIM), F32)
        for l in range(CMP_STRIDE):
            rows = x_ref[pl.ds(l, nblk, stride=CMP_STRIDE), :]
            p1 = p1 + _bdot(rows + pos_ref[a, l:l + 1, :], w1_ref[a, l * HEAD_DIM:(l + 1) * HEAD_DIM, :])
            p2 = p2 + _bdot(rows + pos_ref[a, CMP_STRIDE + l:CMP_STRIDE + l + 1, :],
                            w1_ref[a, half + l * HEAD_DIM:half + (l + 1) * HEAD_DIM, :])
        pre = p1 + pltpu.roll(p2, nblk - 1, 0)
        out = _bdot(_silu(pre), w2_ref[a])
        out_ref[...] = jnp.where(row < nblk - 1, out, 0.0)


def _compress(z, pos, w1, w2, batch, seq):
    nblk = seq // CMP_STRIDE
    src = lambda base: pl.BlockSpec((None, seq, HEAD_DIM), lambda b, g, base=base: (b, 0, base + g))
    out = pl.BlockSpec((None, None, nblk, HEAD_DIM), lambda b, g: (b, g, 0, 0))
    full = lambda a: pl.BlockSpec(a.shape, lambda b, g: (0,) * a.ndim)
    shape = jax.ShapeDtypeStruct((batch, NSA_GROUPS, nblk, HEAD_DIM), F32)
    return pl.pallas_call(
        _compress_kernel,
        grid=(batch, NSA_GROUPS),
        in_specs=[src(16), src(20), full(pos), full(w1), full(w2)],
        out_specs=[out, out],
        out_shape=[shape, shape],
        scratch_shapes=[pltpu.VMEM((seq, HEAD_DIM), F32)],
        compiler_params=_params(("parallel", "parallel")),
        name="nsa_compress",
    )(z, z, pos, w1, w2)


def _nsa_kernel(q_ref, kc_ref, vc_ref, ks_ref, vs_ref, kw_ref, vw_ref, gl_ref, kaug_ref, caug_ref, o_ref,
                m_ref, l_ref, acc_ref, *, seq):
    g = pl.program_id(1)
    qi = pl.program_id(2)
    tq = ATT_TILE
    rows = NSA_HPG * tq
    q0 = qi * tq
    ncmp = seq // CMP_STRIDE
    nblk = seq // SEL_BLOCK

    qb = jnp.concatenate([q_ref[:, hp * HEAD_DIM:(hp + 1) * HEAD_DIM] for hp in range(NSA_HPG)], axis=0)

    rrow = lax.broadcasted_iota(jnp.int32, (rows, LANES), 0)
    rlane = lax.broadcasted_iota(jnp.int32, (rows, LANES), 1)
    hp_idx = _div_pow2(rrow, tq)
    tpos = q0 + (rrow - hp_idx * tq)
    head = (g * NSA_HPG + hp_idx).astype(F32)
    slope = jnp.exp((-8.0 * np.log(2.0) / NSA_HEADS) * (head + 1.0)) * LOG2_E

    s_hi = slope.astype(BF16).astype(F32)
    s_mid = (slope - s_hi).astype(BF16).astype(F32)
    s_lo = slope - s_hi - s_mid
    piece = jnp.where(rlane < KAUG_POS_LANE + 2, s_hi, jnp.where(rlane < KAUG_POS_LANE + 4, s_mid, s_lo))
    bias_cols = jnp.where((rlane >= KAUG_POS_LANE) & (rlane < KAUG_POS_LANE + 6), piece, 0.0)
    q_win = jnp.concatenate([qb, bias_cols.astype(BF16)], axis=1)

    pair = KEY_TILE
    edge = WINDOW // pair
    below_diag = (lax.broadcasted_iota(jnp.int32, (rows, pair), 1)
                  <= jnp.bitwise_and(lax.broadcasted_iota(jnp.int32, (rows, pair), 0), tq - 1))
    causal = lambda sc: jnp.where(below_diag, sc, MASK_VALUE)
    window_edge = lambda sc: jnp.concatenate([jnp.where(below_diag, MASK_VALUE, sc[:, :pair]), sc[:, pair:]],
                                             axis=1)
    fresh = (jnp.full((rows, LANES), MASK_VALUE, F32), jnp.zeros((rows, LANES), F32),
             jnp.zeros((rows, LANES), F32))

    def rows_of(ref, first, ntile):
        tiles = [ref[pl.ds(pl.multiple_of(jnp.maximum(first + t, 0) * pair, pair), pair), :] for t in range(ntile)]
        return tiles[0] if ntile == 1 else jnp.concatenate(tiles, axis=0)

    def scores(qa, k_ref, first, ntile):
        aug = kaug_ref[pl.ds(pl.multiple_of((first + edge) * pair, pair), ntile * pair), :]
        kp = jnp.concatenate([rows_of(k_ref, first, ntile).astype(BF16), aug], axis=1)
        return lax.dot_general(qa, kp, (((1,), (1,)), ((), ())), preferred_element_type=F32)

    def advance(state, sc, values):
        chunks = [sc[:, c * LANES:(c + 1) * LANES] for c in range(sc.shape[1] // LANES)]
        m_new = jnp.maximum(state[0], jnp.max(functools.reduce(jnp.maximum, chunks), axis=-1, keepdims=True))
        p = jnp.concatenate([jnp.exp2(c - m_new) for c in chunks], axis=1).astype(BF16)
        alpha = jnp.exp2(state[0] - m_new)
        ones = jnp.ones((values.shape[0], LANES), BF16)
        r = jnp.dot(p, jnp.concatenate([values.astype(BF16), ones], axis=1), preferred_element_type=F32)
        return m_new, alpha * state[1] + r[:, LANES:], alpha * state[2] + r[:, :LANES]

    s_win_edge = window_edge(scores(q_win, kw_ref, qi - edge, edge))
    s_win_diag = causal(scores(q_win, kw_ref, qi, 1))

    cpos = lax.broadcasted_iota(jnp.int32, (rows, ncmp), 1) * CMP_STRIDE + (CMP_LEN - 1)
    valid_c = tpos[:, :ncmp] >= cpos
    kc_aug = jnp.concatenate([kc_ref[...].astype(BF16), caug_ref[...]], axis=1)
    s_c = lax.dot_general(q_win, kc_aug, (((1,), (1,)), ((), ())), preferred_element_type=F32)
    s_c = jnp.where(valid_c, s_c, MASK_VALUE)
    m_c = jnp.max(s_c, axis=-1, keepdims=True)
    e_c = jnp.where(valid_c, jnp.exp2(s_c - m_c), 0.0)
    den = jnp.sum(e_c, axis=-1, keepdims=True)
    p_c = e_c / jnp.where(den > 0.0, den, 1.0)
    o_cmp = _bdot(p_c, vc_ref[...])
    st_win = advance(fresh, s_win_edge, rows_of(vw_ref, qi - edge, edge))

    p_sum = p_c[0:tq] + p_c[tq:2 * tq] + p_c[2 * tq:3 * tq] + p_c[3 * tq:4 * tq]
    ob = lax.broadcasted_iota(jnp.int32, (KAUG_POS_LANE, ncmp), 0) * SEL_BLOCK
    oj = lax.broadcasted_iota(jnp.int32, (KAUG_POS_LANE, ncmp), 1) * CMP_STRIDE
    overlap_t = ((oj < ob + SEL_BLOCK) & (oj + CMP_LEN > ob)).astype(F32)
    imp_t = lax.dot_general(overlap_t, p_sum, (((1,), (1,)), ((), ())), precision=HIGHEST,
                            preferred_element_type=F32)
    blk = lax.broadcasted_iota(jnp.int32, (KAUG_POS_LANE, tq), 0)
    cur = _div_pow2(q0 + lax.broadcasted_iota(jnp.int32, (KAUG_POS_LANE, tq), 1), SEL_BLOCK)
    forced = (blk == 0) | (blk == cur) | (blk == cur - 1)
    score = jnp.where(forced, SEL_FORCE, jnp.where(blk > cur, -SEL_FORCE, imp_t))
    score = jnp.where(blk < nblk, score, -jnp.inf)
    ahead = jnp.zeros((KAUG_POS_LANE, tq), F32)
    for j in range(nblk):
        sj = score[j:j + 1, :]
        wins_tie = jnp.where(blk > j, 1.0, 0.0)
        ahead = ahead + jnp.where(sj > score, 1.0, jnp.where(sj == score, wins_tie, 0.0))
    sel_t = jnp.where(ahead < float(min(N_SEL, nblk)), 1.0, 0.0)
    sel_t = jnp.concatenate([sel_t, jnp.zeros((LANES - KAUG_POS_LANE, tq), F32)], axis=0)
    sel = sel_t.T
    st_win = advance(st_win, s_win_diag, rows_of(vw_ref, qi, 1))
    o_win = st_win[2] / st_win[1]
    sel_cols = (jnp.concatenate([sel] * NSA_HPG, axis=0) - 1.0) * SEL_MASK_BIG
    q_sel = jnp.concatenate([qb, jnp.where(rlane < KAUG_POS_LANE, sel_cols, bias_cols).astype(BF16)], axis=1)

    for ref, init in zip((m_ref, l_ref, acc_ref), fresh):
        ref[...] = init

    def far_step(j, ntile):
        state = advance((m_ref[...], l_ref[...], acc_ref[...]), scores(q_sel, ks_ref, j, ntile),
                        rows_of(vs_ref, j, ntile))
        for ref, val in zip((m_ref, l_ref, acc_ref), state):
            ref[...] = val

    nfar = jnp.maximum(qi - edge, 0)
    odd = jnp.bitwise_and(nfar, 1)
    pl.when(odd == 1)(lambda: far_step(0, 1))

    def far_body(i, carry):
        far_step(odd + 2 * i, 2)
        return carry

    lax.fori_loop(0, jnp.right_shift(nfar, 1), far_body, 0)

    s_sel_edge = scores(q_sel, ks_ref, qi - edge, edge)
    s_sel_diag = causal(scores(q_sel, ks_ref, qi, 1))
    st_sel = advance((m_ref[...], l_ref[...], acc_ref[...]), s_sel_edge, rows_of(vs_ref, qi - edge, edge))
    st_sel = advance(st_sel, s_sel_diag, rows_of(vs_ref, qi, 1))
    o_slc = st_sel[2] / st_sel[1]

    ngate = NSA_HPG * 3
    erow = lax.broadcasted_iota(jnp.int32, (LANES, LANES), 0)
    elane = lax.broadcasted_iota(jnp.int32, (LANES, LANES), 1)
    pick_gate = (erow == g * ngate + elane).astype(F32)
    gates = _hdot(_sigmoid(gl_ref[...]), pick_gate)
    for hp in range(NSA_HPG):
        rs = slice(hp * tq, (hp + 1) * tq)
        o = (gates[:, 3 * hp:3 * hp + 1] * o_cmp[rs] + gates[:, 3 * hp + 1:3 * hp + 2] * o_slc[rs]
             + gates[:, 3 * hp + 2:3 * hp + 3] * o_win[rs])
        o_ref[:, hp * HEAD_DIM:(hp + 1) * HEAD_DIM] = o.astype(o_ref.dtype)


def _nsa_attention(z, zs, k_cmp, v_cmp, batch, seq):
    tq = ATT_TILE
    hq = NSA_HPG * HEAD_DIM
    nblk = seq // CMP_STRIDE
    assert ATT_TILE == KEY_TILE and seq % KEY_TILE == 0 and seq // SEL_BLOCK <= KAUG_POS_LANE
    assert WINDOW % KEY_TILE == 0
    kv = lambda base: pl.BlockSpec((None, seq, HEAD_DIM), lambda b, g, i, base=base: (b, 0, base + g))
    cmp_spec = pl.BlockSpec((None, None, nblk, HEAD_DIM), lambda b, g, i: (b, g, 0, 0))
    return pl.pallas_call(
        functools.partial(_nsa_kernel, seq=seq),
        grid=(batch, NSA_GROUPS, seq // tq),
        in_specs=[pl.BlockSpec((None, tq, hq), lambda b, g, i: (b, i, g)),
                  cmp_spec, cmp_spec, kv(24), kv(28), kv(32), kv(36),
                  pl.BlockSpec((None, tq, LANES), lambda b, g, i: (b, i, 0)),
                  pl.BlockSpec((seq + WINDOW, LANES), lambda b, g, i: (0, 0)),
                  pl.BlockSpec((nblk, LANES), lambda b, g, i: (0, 0))],
        out_specs=pl.BlockSpec((None, tq, hq), lambda b, g, i: (b, i, g)),
        out_shape=jax.ShapeDtypeStruct((batch, seq, NSA_HEADS * HEAD_DIM), BF16),
        scratch_shapes=[pltpu.VMEM((NSA_HPG * tq, LANES), F32)] * 3,
        compiler_params=_params(("parallel", "parallel", "arbitrary")),
        name="nsa_attention",
    )(z, k_cmp, v_cmp, z, z, z, z, zs, _key_aug(np.arange(-WINDOW, seq), True),
      _key_aug(np.arange(nblk) * CMP_STRIDE + CMP_LEN - 1, False))


def _key_aug(kpos, with_block):
    real = kpos >= 0
    aug = np.zeros((len(kpos), LANES), np.float32)
    if with_block:
        aug[np.nonzero(real)[0], kpos[real] // SEL_BLOCK] = 1.0
    for c in range(3):
        aug[:, KAUG_POS_LANE + 2 * c] = np.where(real, kpos - kpos % 256, -SEL_MASK_BIG)
        aug[:, KAUG_POS_LANE + 2 * c + 1] = np.where(real, kpos % 256, 0)
    return jnp.asarray(aug, BF16)


def _w_in_tail(w_t, layer, n_main):
    tail = w_t[layer, n_main:, :]
    return jnp.pad(tail, ((0, LANES - tail.shape[0]), (0, 0)))


def _pad_lanes(v):
    return jnp.pad(v.astype(F32), (0, LANES - v.shape[0])).reshape(1, LANES)


def _even_layer(h, g_in, g_out, w_in, j, lb, conv_w, a_log, dt_bias, hgrn_g, gdn_g, w_out_bf, batch, seq):
    n_main = 8 * EVEN_WIDTH
    z, zs = _proj_in(h, g_in, w_in, j, _w_in_tail(w_in, j, n_main), jnp.ones((1, n_main), F32))
    o = _even_mixer(z, zs, lb.reshape(1, EVEN_WIDTH), hgrn_g.reshape(1, HEAD_DIM).astype(F32),
                    conv_w.astype(F32), _pad_lanes(a_log), _pad_lanes(dt_bias),
                    gdn_g.reshape(1, HEAD_DIM).astype(F32), batch, seq)
    return _proj_out(o, w_out_bf, j, h, g_out)


def _odd_layer(h, g_in, g_out, w_in, j, cmp_pos, cmp_w1, cmp_w2, w_out_bf, batch, seq):
    n_q = NSA_HEADS * HEAD_DIM
    n_main = n_q + 6 * NSA_GROUPS * HEAD_DIM
    col_scale = jnp.where(jnp.arange(n_main) < n_q, HEAD_DIM ** -0.5 * LOG2_E, 1.0).astype(F32).reshape(1, n_main)
    z, zs = _proj_in(h, g_in, w_in, j, _w_in_tail(w_in, j, n_main), col_scale)
    z3 = z.reshape(batch, seq, n_main)
    k_cmp, v_cmp = _compress(z3, cmp_pos.astype(F32), cmp_w1.astype(BF16), cmp_w2.astype(BF16), batch, seq)
    o = _nsa_attention(z3, zs.reshape(batch, seq, LANES), k_cmp, v_cmp, batch, seq)
    return _proj_out(o.reshape(batch * seq, -1), w_out_bf, j, h, g_out)


def kernel(x, norm_g, ab_w_in, hgrn_lb_logits, gdn_conv_w, gdn_a_log, gdn_dt_bias, hgrn_norm_g, gdn_norm_g,
           ab_w_out, nsa_w_in, nsa_cmp_pos, nsa_cmp_w1, nsa_cmp_w2, nsa_w_out, ffn_w_gu, ffn_w_down):
    batch, seq, d = x.shape
    depth = norm_g.shape[0]
    lb_all = jnp.cumsum(jax.nn.softmax(hgrn_lb_logits.astype(F32), axis=0), axis=0)
    lb_all = lb_all - lb_all[:1]
    g = norm_g.astype(F32).reshape(depth, 4, 1, d)
    ab_w_out_bf, nsa_w_out_bf = ab_w_out.astype(BF16), nsa_w_out.astype(BF16)
    ab_w_in_t, nsa_w_in_t = jnp.swapaxes(ab_w_in, 1, 2), jnp.swapaxes(nsa_w_in, 1, 2)
    h = x.reshape(batch * seq, d)
    for layer in range(depth):
        j = layer // 2
        if layer % 2 == 0:
            h = _even_layer(h, g[layer, 0], g[layer, 1], ab_w_in_t, j, lb_all[j], gdn_conv_w[j],
                            gdn_a_log[j], gdn_dt_bias[j], hgrn_norm_g[j], gdn_norm_g[j], ab_w_out_bf, batch, seq)
        else:
            h = _odd_layer(h, g[layer, 0], g[layer, 1], nsa_w_in_t, j, nsa_cmp_pos[j],
                           nsa_cmp_w1[j], nsa_cmp_w2[j], nsa_w_out_bf, batch, seq)
        h = _ffn(h, g[layer, 2], ffn_w_gu, ffn_w_down, layer, g[layer, 3])
    return h.reshape(batch, seq, d)
```

```python
import functools

import numpy as np
import jax
import jax.numpy as jnp
from jax import lax
from jax.experimental import pallas as pl
from jax.experimental.pallas import tpu as pltpu

F32 = jnp.float32
BF16 = jnp.bfloat16
HIGHEST = lax.Precision.HIGHEST

NORM_EPS = 1e-6
MASK_VALUE = -1e30

LANES = 128
HEAD_DIM = 128
CHUNK = 64
HGRN_BAND = 16
CONV_K = 4
EVEN_HEADS = 8
EVEN_WIDTH = EVEN_HEADS * HEAD_DIM

NSA_GROUPS = 4
NSA_HPG = 4
NSA_HEADS = NSA_GROUPS * NSA_HPG
CMP_LEN = 32
CMP_STRIDE = 16
SEL_BLOCK = 64
N_SEL = 8
WINDOW = 512
SEL_FORCE = 1e4
ATT_TILE = 256
KEY_TILE = 256
KAUG_POS_LANE = 32
SEL_MASK_BIG = 2.0 ** 100
LOG2_E = float(np.log2(np.e))

VMEM_LIMIT = 56 * 1024 * 1024


def _params(semantics):
    return pltpu.CompilerParams(dimension_semantics=semantics, vmem_limit_bytes=VMEM_LIMIT)


def _rms(x, g):
    return x * lax.rsqrt(jnp.mean(x * x, axis=-1, keepdims=True) + NORM_EPS) * g


def _sigmoid(x):
    return 1.0 / (1.0 + jnp.exp(-x))


def _silu(x):
    return x * _sigmoid(x)


def _bdot(a, b):
    return jnp.dot(a.astype(BF16), b.astype(BF16), preferred_element_type=F32)


def _bdot_nt(a, b):
    return lax.dot_general(a.astype(BF16), b.astype(BF16), (((1,), (1,)), ((), ())),
                           preferred_element_type=F32)


def _hdot(a, b):
    return jnp.dot(a, b, precision=HIGHEST, preferred_element_type=F32)


def _split(x):
    hi = x.astype(BF16)
    return hi, (x - hi.astype(F32)).astype(BF16)


def _dot3(a, b):
    (a_hi, a_lo), (b_hi, b_lo) = a, b
    dot = lambda x, y: jnp.dot(x, y, preferred_element_type=F32)
    return dot(a_hi, b_hi) + (dot(a_lo, b_hi) + dot(a_hi, b_lo))


def _div_pow2(x, n):
    return jnp.right_shift(x, int(n).bit_length() - 1)


def _proj_in_kernel(h_ref, g_ref, w_ref, ws_ref, cs_ref, z_ref, zs_ref, xn_ref):
    @pl.when(pl.program_id(1) == 0)
    def _():
        xn = _rms(h_ref[...], g_ref[...]).astype(BF16)
        xn_ref[...] = xn
        zs_ref[...] = _bdot_nt(xn, ws_ref[...])

    z = _bdot_nt(xn_ref[...], w_ref[...])
    z_ref[...] = (z * cs_ref[...]).astype(z_ref.dtype)


def _proj_in(h, g, w_t, layer, ws_t, col_scale, tm=1024, tn=1024):
    m, d = h.shape
    n = col_scale.shape[1]
    assert m % tm == 0 and n % tn == 0
    return pl.pallas_call(
        _proj_in_kernel,
        grid=(m // tm, n // tn),
        in_specs=[
            pl.BlockSpec((tm, d), lambda i, j: (i, 0)),
            pl.BlockSpec((1, d), lambda i, j: (0, 0)),
            pl.BlockSpec((None, tn, d), lambda i, j: (layer, j, 0)),
            pl.BlockSpec((LANES, d), lambda i, j: (0, 0)),
            pl.BlockSpec((1, tn), lambda i, j: (0, j)),
        ],
        out_specs=[
            pl.BlockSpec((tm, tn), lambda i, j: (i, j)),
            pl.BlockSpec((tm, LANES), lambda i, j: (i, 0)),
        ],
        out_shape=[jax.ShapeDtypeStruct((m, n), BF16), jax.ShapeDtypeStruct((m, LANES), F32)],
        scratch_shapes=[pltpu.VMEM((tm, d), BF16)],
        compiler_params=_params(("parallel", "arbitrary")),
        name="proj_in",
    )(h, g, w_t, ws_t, col_scale)


def _proj_out_kernel(o_ref, w_ref, h_ref, g_ref, out_ref):
    m = jnp.dot(o_ref[...], w_ref[...], preferred_element_type=F32)
    out_ref[...] = h_ref[...] + _rms(m, g_ref[...])


def _proj_out(o, w, layer, h, g, tm=512):
    m, k = o.shape
    d = w.shape[2]
    return pl.pallas_call(
        _proj_out_kernel,
        grid=(m // tm,),
        in_specs=[
            pl.BlockSpec((tm, k), lambda i: (i, 0)),
            pl.BlockSpec((None, k, d), lambda i: (layer, 0, 0)),
            pl.BlockSpec((tm, d), lambda i: (i, 0)),
            pl.BlockSpec((1, d), lambda i: (0, 0)),
        ],
        out_specs=pl.BlockSpec((tm, d), lambda i: (i, 0)),
        out_shape=jax.ShapeDtypeStruct((m, d), F32),
        compiler_params=_params(("parallel",)),
        name="proj_out",
    )(o, w, h, g)


def _ffn_kernel(h_ref, gin_ref, wg_ref, wu_ref, wd_ref, gout_ref, out_ref, xn_ref):
    f = pl.program_id(1)

    @pl.when(f == 0)
    def _():
        xn_ref[...] = _rms(h_ref[...], gin_ref[...]).astype(BF16)
        out_ref[...] = jnp.zeros_like(out_ref)

    xn = xn_ref[...]
    gate = jnp.dot(xn, wg_ref[...].astype(BF16), preferred_element_type=F32)
    up = jnp.dot(xn, wu_ref[...].astype(BF16), preferred_element_type=F32)
    act = (_silu(gate) * up).astype(BF16)
    out_ref[...] += jnp.dot(act, wd_ref[...].astype(BF16), preferred_element_type=F32)

    @pl.when(f == pl.num_programs(1) - 1)
    def _():
        out_ref[...] = h_ref[...] + _rms(out_ref[...], gout_ref[...])


def _ffn(h, g_in, w_gu, w_down, layer, g_out, tm=1024, tf=256):
    m, d = h.shape
    d_ff = w_down.shape[1]
    nf = d_ff // tf
    assert m % tm == 0 and d_ff % tf == 0
    return pl.pallas_call(
        _ffn_kernel,
        grid=(m // tm, nf),
        in_specs=[
            pl.BlockSpec((tm, d), lambda i, f: (i, 0), pipeline_mode=pl.Buffered(1)),
            pl.BlockSpec((1, d), lambda i, f: (0, 0)),
            pl.BlockSpec((None, d, tf), lambda i, f: (layer, 0, f)),
            pl.BlockSpec((None, d, tf), lambda i, f: (layer, 0, nf + f)),
            pl.BlockSpec((None, tf, d), lambda i, f: (layer, f, 0)),
            pl.BlockSpec((1, d), lambda i, f: (0, 0)),
        ],
        out_specs=pl.BlockSpec((tm, d), lambda i, f: (i, 0)),
        out_shape=jax.ShapeDtypeStruct((m, d), F32),
        scratch_shapes=[pltpu.VMEM((tm, d), BF16)],
        compiler_params=_params(("parallel", "arbitrary")),
        name="ffn",
    )(h, g_in, w_gu, w_gu, w_down, g_out)


def _hgrn_chunk(q_ref, f_ref, i_ref, gate_ref, lb_ref, ng_ref, sel_ref, o_ref, st_ref, bpad_ref, kpad_ref):
    band = HGRN_BAND

    r64 = lax.broadcasted_iota(jnp.int32, (CHUNK, CHUNK), 0)
    c64 = lax.broadcasted_iota(jnp.int32, (CHUNK, CHUNK), 1)
    tril = (r64 >= c64).astype(F32)
    lb = lb_ref[...]
    f_all = lb + (1.0 - lb) * _sigmoid(f_ref[...].astype(F32))
    b_all = _hdot(tril, jnp.log(f_all) * LOG2_E)
    bpad_ref[band:band + CHUNK, :] = b_all
    kpad_ref[band:band + CHUNK, :] = 1.0 - f_all
    yield

    for hd in range(EVEN_HEADS):
        sl = slice(hd * HEAD_DIM, (hd + 1) * HEAD_DIM)
        q = q_ref[:, sl].astype(F32)
        b = b_all[:, sl]
        k = kpad_ref[band:band + CHUNK, sl]
        v = i_ref[:, sl].astype(F32)

        b_ext = bpad_ref[:, sl]
        k_ext = kpad_ref[:, sl]
        terms = []
        for d in range(band):
            bs = b if d == 0 else pltpu.roll(b_ext, d, 0)[band:, :]
            ks = k if d == 0 else pltpu.roll(k_ext, d, 0)[band:, :]
            terms.append((jnp.exp2(b - bs) * (q * ks)).astype(BF16))
        diag = jnp.dot(jnp.concatenate(terms, axis=1), sel_ref[...], preferred_element_type=F32)
        a = pltpu.roll(diag, 1, 1, stride=1, stride_axis=0)

        far = [jnp.zeros((band, CHUNK), F32)]
        for blk in range(1, CHUNK // band):
            r0 = blk * band
            br = b[r0:r0 + 1, :]
            qi = q[r0:r0 + band, :] * jnp.exp2(b[r0:r0 + band, :] - br)
            ki = k * jnp.exp2(jnp.minimum(br - b, 0.0))
            far.append(_bdot_nt(qi, ki))
        a_far = jnp.where(r64 - c64 >= band, jnp.concatenate(far, axis=0), 0.0)

        st = st_ref[hd]
        o = _bdot(a[:, :CHUNK] + a_far, v) + _bdot_nt(q * jnp.exp2(b), st)
        b_end = b[CHUNK - 1:CHUNK, :]
        kd = k * jnp.exp2(b_end - b)
        st_ref[hd] = jnp.exp2(b_end) * st + _bdot(v.T, kd)
        o_ref[:, sl] = (_rms(o, ng_ref[...]) * _silu(gate_ref[:, sl].astype(F32))).astype(o_ref.dtype)
        yield


def _unit_lower_inverses(l_mats):
    r = lax.broadcasted_iota(jnp.int32, (CHUNK, CHUNK), 0)
    c = lax.broadcasted_iota(jnp.int32, (CHUNK, CHUNK), 1)
    eye = (r == c).astype(F32)
    same16 = _div_pow2(r, 16) == _div_pow2(c, 16)
    same32 = _div_pow2(r, 32) == _div_pow2(c, 32)
    m = [_split(jnp.where(same16, -l, 0.0)) for l in l_mats]
    p = [eye + jnp.where(same16, -l, 0.0) for l in l_mats]
    for _ in range(3):
        m = [_split(_dot3(x, x)) for x in m]
        yield
        p = [y + _dot3(_split(y), x) for y, x in zip(p, m)]
        yield
    for off_diag in (same32 & jnp.logical_not(same16), jnp.logical_not(same32)):
        ps = [_split(y) for y in p]
        t = [_dot3(ys, _split(jnp.where(off_diag, l, 0.0))) for ys, l in zip(ps, l_mats)]
        yield
        p = [y - _dot3(_split(x), ys) for y, ys, x in zip(p, ps, t)]
        yield
    return p


def _gdn_chunk(xq_ref, xk_ref, xv_ref, gate_ref, zs_ref, cw_ref, alog_ref, dtb_ref, ng_ref,
               o_ref, s_ref, xbuf_ref):
    w = EVEN_WIDTH

    conv = []
    for a, x_ref in enumerate((xq_ref, xk_ref, xv_ref)):
        x = x_ref[...].astype(F32)
        xbuf_ref[a, 8:8 + CHUNK, :] = x
        x_ext = xbuf_ref[a]
        y = x * cw_ref[CONV_K - 1:CONV_K, a * w:(a + 1) * w]
        for j in range(CONV_K - 1):
            lag = CONV_K - 1 - j
            y = y + pltpu.roll(x_ext, lag, 0)[8:, :] * cw_ref[j:j + 1, a * w:(a + 1) * w]
        xbuf_ref[a, 0:8, :] = x[CHUNK - 8:, :]
        conv.append(_silu(y))
        yield
    cq, ck, cv = conv

    zs = zs_ref[...]
    log_decay = -jnp.exp(alog_ref[...]) * jax.nn.softplus(zs + dtb_ref[...])
    r64 = lax.broadcasted_iota(jnp.int32, (CHUNK, CHUNK), 0)
    c64 = lax.broadcasted_iota(jnp.int32, (CHUNK, CHUNK), 1)
    tril = r64 >= c64
    gcum = _hdot(tril.astype(F32), log_decay)
    gcum_t = gcum.T
    beta_all = _sigmoid(zs)

    heads = range(EVEN_HEADS)
    sls = [slice(hd * HEAD_DIM, (hd + 1) * HEAD_DIM) for hd in heads]
    l2 = lambda x: x * lax.rsqrt(jnp.sum(x * x, axis=-1, keepdims=True) + NORM_EPS)
    q = [l2(cq[:, sl]) * (HEAD_DIM ** -0.5) for sl in sls]
    k = [l2(ck[:, sl]) for sl in sls]
    v = [cv[:, sl] for sl in sls]
    yield
    g_col = [gcum[:, hd:hd + 1] for hd in heads]
    beta = [beta_all[:, EVEN_HEADS + hd:EVEN_HEADS + hd + 1] for hd in heads]
    decay = [jnp.exp(jnp.where(tril, g_col[hd] - gcum_t[hd:hd + 1, :], -jnp.inf)) for hd in heads]
    kb = [k[hd] * beta[hd] for hd in heads]
    l_mat = [jnp.where(r64 > c64, _bdot_nt(kb[hd], k[hd]) * decay[hd], 0.0) for hd in heads]
    yield
    qk = [_bdot_nt(q[hd], k[hd]) * decay[hd] for hd in heads]
    eg = [jnp.exp(g_col[hd]) for hd in heads]
    rhs = [jnp.concatenate([v[hd] * beta[hd], kb[hd] * eg[hd]], axis=-1) for hd in heads]
    inv = yield from _unit_lower_inverses(l_mat)
    sol = [_dot3(_split(inv[hd]), _split(rhs[hd])) for hd in heads]
    yield
    s = [s_ref[hd] for hd in heads]
    v_new = [sol[hd][:, :HEAD_DIM] - _bdot(sol[hd][:, HEAD_DIM:], s[hd]) for hd in heads]
    yield
    o = [_bdot(q[hd] * eg[hd], s[hd]) + _bdot(qk[hd], v_new[hd]) for hd in heads]
    yield
    for hd in heads:
        g_end = g_col[hd][CHUNK - 1:CHUNK, :]
        kd = k[hd] * jnp.exp(g_end - g_col[hd])
        s_ref[hd] = jnp.exp(g_end) * s[hd] + _bdot(kd.T, v_new[hd])
        o_ref[:, sls[hd]] = (_rms(o[hd], ng_ref[...])
                             * _silu(gate_ref[:, sls[hd]].astype(F32))).astype(o_ref.dtype)
        if hd % 2:
            yield


def _even_mixer_kernel(hq_ref, hf_ref, hi_ref, hgate_ref, xq_ref, xk_ref, xv_ref, xgate_ref, zs_ref,
                       lb_ref, hng_ref, cw_ref, alog_ref, dtb_ref, gng_ref, sel_ref, o_ref,
                       st_ref, bpad_ref, kpad_ref, s_ref, xbuf_ref):
    w = EVEN_WIDTH

    @pl.when(pl.program_id(1) == 0)
    def _():
        st_ref[...] = jnp.zeros_like(st_ref)
        s_ref[...] = jnp.zeros_like(s_ref)
        bpad_ref[0:HGRN_BAND, :] = jnp.zeros((HGRN_BAND, w), F32)
        kpad_ref[0:HGRN_BAND, :] = jnp.zeros((HGRN_BAND, w), F32)
        xbuf_ref[:, 0:8, :] = jnp.zeros((3, 8, w), F32)

    hgrn = _hgrn_chunk(hq_ref, hf_ref, hi_ref, hgate_ref, lb_ref, hng_ref, sel_ref, o_ref.at[:, 0:w],
                       st_ref, bpad_ref, kpad_ref)
    gdn = _gdn_chunk(xq_ref, xk_ref, xv_ref, xgate_ref, zs_ref, cw_ref, alog_ref, dtb_ref, gng_ref,
                     o_ref.at[:, w:2 * w], s_ref, xbuf_ref)
    pending = [gdn, gdn, hgrn]
    while pending:
        for gen in list(pending):
            if gen in pending and next(gen, StopIteration) is StopIteration:
                pending = [x for x in pending if x is not gen]


def _even_mixer(z, zs, lb, hgrn_g, conv_w, alog, dtb, gdn_g, batch, seq):
    nchunk = seq // CHUNK
    w = EVEN_WIDTH
    col = lambda c: pl.BlockSpec((CHUNK, w), lambda b, n, c=c: (b * nchunk + n, c))
    vec = lambda width: pl.BlockSpec((1, width), lambda b, n: (0, 0))
    state = pltpu.VMEM((EVEN_HEADS, HEAD_DIM, HEAD_DIM), F32)
    return pl.pallas_call(
        _even_mixer_kernel,
        grid=(batch, nchunk),
        in_specs=[col(c) for c in range(8)] + [
            pl.BlockSpec((CHUNK, LANES), lambda b, n: (b * nchunk + n, 0)),
            vec(w), vec(HEAD_DIM),
            pl.BlockSpec((CONV_K, 3 * w), lambda b, n: (0, 0)),
            vec(LANES), vec(LANES), vec(HEAD_DIM),
            pl.BlockSpec((HGRN_BAND * LANES, LANES), lambda b, n: (0, 0))],
        out_specs=pl.BlockSpec((CHUNK, 2 * w), lambda b, n: (b * nchunk + n, 0)),
        out_shape=jax.ShapeDtypeStruct((batch * seq, 2 * w), BF16),
        scratch_shapes=[state, pltpu.VMEM((HGRN_BAND + CHUNK, w), F32), pltpu.VMEM((HGRN_BAND + CHUNK, w), F32),
                        state, pltpu.VMEM((3, 8 + CHUNK, w), F32)],
        compiler_params=_params(("parallel", "arbitrary")),
        name="even_mixer",
    )(*([z] * 8), zs, lb, hgrn_g, conv_w, alog, dtb, gdn_g, _band_select())


def _band_select():
    sel = np.zeros((HGRN_BAND, LANES, LANES), np.float32)
    for d in range(HGRN_BAND):
        sel[d, :, LANES - 1 - d] = 1.0
    return jnp.asarray(sel.reshape(HGRN_BAND * LANES, LANES), BF16)


def _compress_kernel(kc_ref, vc_ref, pos_ref, w1_ref, w2_ref, ko_ref, vo_ref, x_ref):
    nblk = ko_ref.shape[0]
    row = lax.broadcasted_iota(jnp.int32, (nblk, HEAD_DIM), 0)
    half = CMP_STRIDE * HEAD_DIM
    for a, (src_ref, out_ref) in enumerate(((kc_ref, ko_ref), (vc_ref, vo_ref))):
        x_ref[...] = src_ref[...].astype(F32)
        p1 = jnp.zeros((nblk, HEAD_DIM), F32)
        p2 = jnp.zeros((nblk, HEAD_DIM), F32)
        for l in range(CMP_STRIDE):
            rows = x_ref[pl.ds(l, nblk, stride=CMP_STRIDE), :]
            p1 = p1 + _bdot(rows + pos_ref[a, l:l + 1, :], w1_ref[a, l * HEAD_DIM:(l + 1) * HEAD_DIM, :])
            p2 = p2 + _bdot(rows + pos_ref[a, CMP_STRIDE + l:CMP_STRIDE + l + 1, :],
                            w1_ref[a, half + l * HEAD_DIM:half + (l + 1) * HEAD_DIM, :])
        pre = p1 + pltpu.roll(p2, nblk - 1, 0)
        out = _bdot(_silu(pre), w2_ref[a])
        out_ref[...] = jnp.where(row < nblk - 1, out, 0.0)


def _compress(z, pos, w1, w2, batch, seq):
    nblk = seq // CMP_STRIDE
    src = lambda base: pl.BlockSpec((None, seq, HEAD_DIM), lambda b, g, base=base: (b, 0, base + g))
    out = pl.BlockSpec((None, None, nblk, HEAD_DIM), lambda b, g: (b, g, 0, 0))
    full = lambda a: pl.BlockSpec(a.shape, lambda b, g: (0,) * a.ndim)
    shape = jax.ShapeDtypeStruct((batch, NSA_GROUPS, nblk, HEAD_DIM), F32)
    return pl.pallas_call(
        _compress_kernel,
        grid=(batch, NSA_GROUPS),
        in_specs=[src(16), src(20), full(pos), full(w1), full(w2)],
        out_specs=[out, out],
        out_shape=[shape, shape],
        scratch_shapes=[pltpu.VMEM((seq, HEAD_DIM), F32)],
        compiler_params=_params(("parallel", "parallel")),
        name="nsa_compress",
    )(z, z, pos, w1, w2)


def _nsa_kernel(q_ref, kc_ref, vc_ref, ks_ref, vs_ref, kw_ref, vw_ref, gl_ref, kaug_ref, caug_ref, o_ref,
                m_ref, l_ref, acc_ref, *, seq):
    g = pl.program_id(1)
    qi = pl.program_id(2)
    tq = ATT_TILE
    rows = NSA_HPG * tq
    q0 = qi * tq
    ncmp = seq // CMP_STRIDE
    nblk = seq // SEL_BLOCK

    qb = jnp.concatenate([q_ref[:, hp * HEAD_DIM:(hp + 1) * HEAD_DIM] for hp in range(NSA_HPG)], axis=0)

    rrow = lax.broadcasted_iota(jnp.int32, (rows, LANES), 0)
    rlane = lax.broadcasted_iota(jnp.int32, (rows, LANES), 1)
    hp_idx = _div_pow2(rrow, tq)
    tpos = q0 + (rrow - hp_idx * tq)
    head = (g * NSA_HPG + hp_idx).astype(F32)
    slope = jnp.exp((-8.0 * np.log(2.0) / NSA_HEADS) * (head + 1.0)) * LOG2_E

    s_hi = slope.astype(BF16).astype(F32)
    s_mid = (slope - s_hi).astype(BF16).astype(F32)
    s_lo = slope - s_hi - s_mid
    piece = jnp.where(rlane < KAUG_POS_LANE + 2, s_hi, jnp.where(rlane < KAUG_POS_LANE + 4, s_mid, s_lo))
    bias_cols = jnp.where((rlane >= KAUG_POS_LANE) & (rlane < KAUG_POS_LANE + 6), piece, 0.0)
    q_win = jnp.concatenate([qb, bias_cols.astype(BF16)], axis=1)

    pair = KEY_TILE
    edge = WINDOW // pair
    below_diag = (lax.broadcasted_iota(jnp.int32, (rows, pair), 1)
                  <= jnp.bitwise_and(lax.broadcasted_iota(jnp.int32, (rows, pair), 0), tq - 1))
    causal = lambda sc: jnp.where(below_diag, sc, MASK_VALUE)
    window_edge = lambda sc: jnp.concatenate([jnp.where(below_diag, MASK_VALUE, sc[:, :pair]), sc[:, pair:]],
                                             axis=1)
    fresh = (jnp.full((rows, LANES), MASK_VALUE, F32), jnp.zeros((rows, LANES), F32),
             jnp.zeros((rows, LANES), F32))

    def rows_of(ref, first, ntile):
        tiles = [ref[pl.ds(pl.multiple_of(jnp.maximum(first + t, 0) * pair, pair), pair), :] for t in range(ntile)]
        return tiles[0] if ntile == 1 else jnp.concatenate(tiles, axis=0)

    def scores(qa, k_ref, first, ntile):
        aug = kaug_ref[pl.ds(pl.multiple_of((first + edge) * pair, pair), ntile * pair), :]
        kp = jnp.concatenate([rows_of(k_ref, first, ntile).astype(BF16), aug], axis=1)
        return lax.dot_general(qa, kp, (((1,), (1,)), ((), ())), preferred_element_type=F32)

    def advance(state, sc, values):
        chunks = [sc[:, c * LANES:(c + 1) * LANES] for c in range(sc.shape[1] // LANES)]
        m_new = jnp.maximum(state[0], jnp.max(functools.reduce(jnp.maximum, chunks), axis=-1, keepdims=True))
        p = jnp.concatenate([jnp.exp2(c - m_new) for c in chunks], axis=1).astype(BF16)
        alpha = jnp.exp2(state[0] - m_new)
        ones = jnp.ones((values.shape[0], LANES), BF16)
        r = jnp.dot(p, jnp.concatenate([values.astype(BF16), ones], axis=1), preferred_element_type=F32)
        return m_new, alpha * state[1] + r[:, LANES:], alpha * state[2] + r[:, :LANES]

    s_win_edge = window_edge(scores(q_win, kw_ref, qi - edge, edge))
    s_win_diag = causal(scores(q_win, kw_ref, qi, 1))

    cpos = lax.broadcasted_iota(jnp.int32, (rows, ncmp), 1) * CMP_STRIDE + (CMP_LEN - 1)
    valid_c = tpos[:, :ncmp] >= cpos
    kc_aug = jnp.concatenate([kc_ref[...].astype(BF16), caug_ref[...]], axis=1)
    s_c = lax.dot_general(q_win, kc_aug, (((1,), (1,)), ((), ())), preferred_element_type=F32)
    s_c = jnp.where(valid_c, s_c, MASK_VALUE)
    m_c = jnp.max(s_c, axis=-1, keepdims=True)
    e_c = jnp.where(valid_c, jnp.exp2(s_c - m_c), 0.0)
    den = jnp.sum(e_c, axis=-1, keepdims=True)
    p_c = e_c / jnp.where(den > 0.0, den, 1.0)
    o_cmp = _bdot(p_c, vc_ref[...])
    st_win = advance(fresh, s_win_edge, rows_of(vw_ref, qi - edge, edge))

    p_sum = p_c[0:tq] + p_c[tq:2 * tq] + p_c[2 * tq:3 * tq] + p_c[3 * tq:4 * tq]
    ob = lax.broadcasted_iota(jnp.int32, (KAUG_POS_LANE, ncmp), 0) * SEL_BLOCK
    oj = lax.broadcasted_iota(jnp.int32, (KAUG_POS_LANE, ncmp), 1) * CMP_STRIDE
    overlap_t = ((oj < ob + SEL_BLOCK) & (oj + CMP_LEN > ob)).astype(F32)
    imp_t = lax.dot_general(overlap_t, p_sum, (((1,), (1,)), ((), ())), precision=HIGHEST,
                            preferred_element_type=F32)
    blk = lax.broadcasted_iota(jnp.int32, (KAUG_POS_LANE, tq), 0)
    cur = _div_pow2(q0 + lax.broadcasted_iota(jnp.int32, (KAUG_POS_LANE, tq), 1), SEL_BLOCK)
    forced = (blk == 0) | (blk == cur) | (blk == cur - 1)
    score = jnp.where(forced, SEL_FORCE, jnp.where(blk > cur, -SEL_FORCE, imp_t))
    score = jnp.where(blk < nblk, score, -jnp.inf)
    ahead = jnp.zeros((KAUG_POS_LANE, tq), F32)
    for j in range(nblk):
        sj = score[j:j + 1, :]
        wins_tie = jnp.where(blk > j, 1.0, 0.0)
        ahead = ahead + jnp.where(sj > score, 1.0, jnp.where(sj == score, wins_tie, 0.0))
    sel_t = jnp.where(ahead < float(min(N_SEL, nblk)), 1.0, 0.0)
    sel_t = jnp.concatenate([sel_t, jnp.zeros((LANES - KAUG_POS_LANE, tq), F32)], axis=0)
    sel = sel_t.T
    st_win = advance(st_win, s_win_diag, rows_of(vw_ref, qi, 1))
    o_win = st_win[2] / st_win[1]

    ngate = NSA_HPG * 3
    erow = lax.broadcasted_iota(jnp.int32, (LANES, LANES), 0)
    elane = lax.broadcasted_iota(jnp.int32, (LANES, LANES), 1)
    pick_gate = (erow == g * ngate + elane).astype(F32)
    gates = _hdot(_sigmoid(gl_ref[...]), pick_gate)
    gate = lambda hp, branch: jnp.broadcast_to(gates[:, 3 * hp + branch:3 * hp + branch + 1], (tq, HEAD_DIM))
    gated = [gate(hp, 0) * o_cmp[hp * tq:(hp + 1) * tq] + gate(hp, 2) * o_win[hp * tq:(hp + 1) * tq]
             for hp in range(NSA_HPG)]
    slc_gate = [gate(hp, 1) for hp in range(NSA_HPG)]
    sel_cols = (jnp.concatenate([sel] * NSA_HPG, axis=0) - 1.0) * SEL_MASK_BIG
    q_sel = jnp.concatenate([qb, jnp.where(rlane < KAUG_POS_LANE, sel_cols, bias_cols).astype(BF16)], axis=1)

    for ref, init in zip((m_ref, l_ref, acc_ref), fresh):
        ref[...] = init

    def far_step(j, ntile):
        state = advance((m_ref[...], l_ref[...], acc_ref[...]), scores(q_sel, ks_ref, j, ntile),
                        rows_of(vs_ref, j, ntile))
        for ref, val in zip((m_ref, l_ref, acc_ref), state):
            ref[...] = val

    nfar = jnp.maximum(qi - edge, 0)
    odd = jnp.bitwise_and(nfar, 1)
    pl.when(odd == 1)(lambda: far_step(0, 1))

    def far_body(i, carry):
        far_step(odd + 2 * i, 2)
        return carry

    lax.fori_loop(0, jnp.right_shift(nfar, 1), far_body, 0)

    s_sel_edge = scores(q_sel, ks_ref, qi - edge, edge)
    s_sel_diag = causal(scores(q_sel, ks_ref, qi, 1))
    st_sel = advance((m_ref[...], l_ref[...], acc_ref[...]), s_sel_edge, rows_of(vs_ref, qi - edge, edge))
    st_sel = advance(st_sel, s_sel_diag, rows_of(vs_ref, qi, 1))
    o_slc = st_sel[2] / st_sel[1]

    for hp in range(NSA_HPG):
        o = gated[hp] + slc_gate[hp] * o_slc[hp * tq:(hp + 1) * tq]
        o_ref[:, hp * HEAD_DIM:(hp + 1) * HEAD_DIM] = o.astype(o_ref.dtype)


def _nsa_attention(z, zs, k_cmp, v_cmp, batch, seq):
    tq = ATT_TILE
    hq = NSA_HPG * HEAD_DIM
    nblk = seq // CMP_STRIDE
    assert ATT_TILE == KEY_TILE and seq % KEY_TILE == 0 and seq // SEL_BLOCK <= KAUG_POS_LANE
    assert WINDOW % KEY_TILE == 0
    kv = lambda base: pl.BlockSpec((None, seq, HEAD_DIM), lambda b, g, i, base=base: (b, 0, base + g))
    cmp_spec = pl.BlockSpec((None, None, nblk, HEAD_DIM), lambda b, g, i: (b, g, 0, 0))
    return pl.pallas_call(
        functools.partial(_nsa_kernel, seq=seq),
        grid=(batch, NSA_GROUPS, seq // tq),
        in_specs=[pl.BlockSpec((None, tq, hq), lambda b, g, i: (b, i, g)),
                  cmp_spec, cmp_spec, kv(24), kv(28), kv(32), kv(36),
                  pl.BlockSpec((None, tq, LANES), lambda b, g, i: (b, i, 0)),
                  pl.BlockSpec((seq + WINDOW, LANES), lambda b, g, i: (0, 0)),
                  pl.BlockSpec((nblk, LANES), lambda b, g, i: (0, 0))],
        out_specs=pl.BlockSpec((None, tq, hq), lambda b, g, i: (b, i, g)),
        out_shape=jax.ShapeDtypeStruct((batch, seq, NSA_HEADS * HEAD_DIM), BF16),
        scratch_shapes=[pltpu.VMEM((NSA_HPG * tq, LANES), F32)] * 3,
        compiler_params=_params(("parallel", "parallel", "arbitrary")),
        name="nsa_attention",
    )(z, k_cmp, v_cmp, z, z, z, z, zs, _key_aug(np.arange(-WINDOW, seq), True),
      _key_aug(np.arange(nblk) * CMP_STRIDE + CMP_LEN - 1, False))


def _key_aug(kpos, with_block):
    real = kpos >= 0
    aug = np.zeros((len(kpos), LANES), np.float32)
    if with_block:
        aug[np.nonzero(real)[0], kpos[real] // SEL_BLOCK] = 1.0
    for c in range(3):
        aug[:, KAUG_POS_LANE + 2 * c] = np.where(real, kpos - kpos % 256, -SEL_MASK_BIG)
        aug[:, KAUG_POS_LANE + 2 * c + 1] = np.where(real, kpos % 256, 0)
    return jnp.asarray(aug, BF16)


def _w_in_tail(w_t, layer, n_main):
    tail = w_t[layer, n_main:, :]
    return jnp.pad(tail, ((0, LANES - tail.shape[0]), (0, 0)))


def _pad_lanes(v):
    return jnp.pad(v.astype(F32), (0, LANES - v.shape[0])).reshape(1, LANES)


def _even_layer(h, g_in, g_out, w_in, j, lb, conv_w, a_log, dt_bias, hgrn_g, gdn_g, w_out_bf, batch, seq):
    n_main = 8 * EVEN_WIDTH
    z, zs = _proj_in(h, g_in, w_in, j, _w_in_tail(w_in, j, n_main), jnp.ones((1, n_main), F32))
    o = _even_mixer(z, zs, lb.reshape(1, EVEN_WIDTH), hgrn_g.reshape(1, HEAD_DIM).astype(F32),
                    conv_w.astype(F32), _pad_lanes(a_log), _pad_lanes(dt_bias),
                    gdn_g.reshape(1, HEAD_DIM).astype(F32), batch, seq)
    return _proj_out(o, w_out_bf, j, h, g_out)


def _odd_layer(h, g_in, g_out, w_in, j, cmp_pos, cmp_w1, cmp_w2, w_out_bf, batch, seq):
    n_q = NSA_HEADS * HEAD_DIM
    n_main = n_q + 6 * NSA_GROUPS * HEAD_DIM
    col_scale = jnp.where(jnp.arange(n_main) < n_q, HEAD_DIM ** -0.5 * LOG2_E, 1.0).astype(F32).reshape(1, n_main)
    z, zs = _proj_in(h, g_in, w_in, j, _w_in_tail(w_in, j, n_main), col_scale)
    z3 = z.reshape(batch, seq, n_main)
    k_cmp, v_cmp = _compress(z3, cmp_pos.astype(F32), cmp_w1.astype(BF16), cmp_w2.astype(BF16), batch, seq)
    o = _nsa_attention(z3, zs.reshape(batch, seq, LANES), k_cmp, v_cmp, batch, seq)
    return _proj_out(o.reshape(batch * seq, -1), w_out_bf, j, h, g_out)


def kernel(x, norm_g, ab_w_in, hgrn_lb_logits, gdn_conv_w, gdn_a_log, gdn_dt_bias, hgrn_norm_g, gdn_norm_g,
           ab_w_out, nsa_w_in, nsa_cmp_pos, nsa_cmp_w1, nsa_cmp_w2, nsa_w_out, ffn_w_gu, ffn_w_down):
    batch, seq, d = x.shape
    depth = norm_g.shape[0]
    lb_all = jnp.cumsum(jax.nn.softmax(hgrn_lb_logits.astype(F32), axis=0), axis=0)
    lb_all = lb_all - lb_all[:1]
    g = norm_g.astype(F32).reshape(depth, 4, 1, d)
    ab_w_out_bf, nsa_w_out_bf = ab_w_out.astype(BF16), nsa_w_out.astype(BF16)
    ab_w_in_t, nsa_w_in_t = jnp.swapaxes(ab_w_in, 1, 2), jnp.swapaxes(nsa_w_in, 1, 2)
    h = x.reshape(batch * seq, d)
    for layer in range(depth):
        j = layer // 2
        if layer % 2 == 0:
            h = _even_layer(h, g[layer, 0], g[layer, 1], ab_w_in_t, j, lb_all[j], gdn_conv_w[j],
                            gdn_a_log[j], gdn_dt_bias[j], hgrn_norm_g[j], gdn_norm_g[j], ab_w_out_bf, batch, seq)
        else:
            h = _odd_layer(h, g[layer, 0], g[layer, 1], nsa_w_in_t, j, nsa_cmp_pos[j],
                           nsa_cmp_w1[j], nsa_cmp_w2[j], nsa_w_out_bf, batch, seq)
        h = _ffn(h, g[layer, 2], ffn_w_gu, ffn_w_down, layer, g[layer, 3])
    return h.reshape(batch, seq, d)
```

```python
import functools

import numpy as np
import jax
import jax.numpy as jnp
from jax import lax
from jax.experimental import pallas as pl
from jax.experimental.pallas import tpu as pltpu

F32 = jnp.float32
BF16 = jnp.bfloat16
HIGHEST = lax.Precision.HIGHEST

NORM_EPS = 1e-6
MASK_VALUE = -1e30

LANES = 128
HEAD_DIM = 128
CHUNK = 64
HGRN_BAND = 16
CONV_K = 4
EVEN_HEADS = 8
EVEN_WIDTH = EVEN_HEADS * HEAD_DIM

NSA_GROUPS = 4
NSA_HPG = 4
NSA_HEADS = NSA_GROUPS * NSA_HPG
CMP_LEN = 32
CMP_STRIDE = 16
SEL_BLOCK = 64
N_SEL = 8
WINDOW = 512
SEL_FORCE = 1e4
ATT_TILE = 256
KEY_TILE = 256
KAUG_POS_LANE = 32
SEL_MASK_BIG = 2.0 ** 100
LOG2_E = float(np.log2(np.e))

VMEM_LIMIT = 58 * 1024 * 1024


def _params(semantics):
    return pltpu.CompilerParams(dimension_semantics=semantics, vmem_limit_bytes=VMEM_LIMIT)


def _rms(x, g):
    return x * lax.rsqrt(jnp.mean(x * x, axis=-1, keepdims=True) + NORM_EPS) * g


def _sigmoid(x):
    return 1.0 / (1.0 + jnp.exp(-x))


def _silu(x):
    return x * _sigmoid(x)


def _bdot(a, b):
    return jnp.dot(a.astype(BF16), b.astype(BF16), preferred_element_type=F32)


def _bdot_nt(a, b):
    return lax.dot_general(a.astype(BF16), b.astype(BF16), (((1,), (1,)), ((), ())),
                           preferred_element_type=F32)


def _hdot(a, b):
    return jnp.dot(a, b, precision=HIGHEST, preferred_element_type=F32)


def _split(x):
    hi = x.astype(BF16)
    return hi, (x - hi.astype(F32)).astype(BF16)


def _dot3(a, b):
    (a_hi, a_lo), (b_hi, b_lo) = a, b
    dot = lambda x, y: jnp.dot(x, y, preferred_element_type=F32)
    return dot(a_hi, b_hi) + (dot(a_lo, b_hi) + dot(a_hi, b_lo))


def _div_pow2(x, n):
    return jnp.right_shift(x, int(n).bit_length() - 1)


def _proj_in_kernel(h_ref, g_ref, w_ref, ws_ref, cs_ref, z_ref, zs_ref, xn_ref):
    @pl.when(pl.program_id(1) == 0)
    def _():
        xn = _rms(h_ref[...], g_ref[...]).astype(BF16)
        xn_ref[...] = xn
        zs_ref[...] = _bdot_nt(xn, ws_ref[...])

    z = _bdot_nt(xn_ref[...], w_ref[...])
    z_ref[...] = (z * cs_ref[...]).astype(z_ref.dtype)


def _proj_in(h, g, w_t, layer, ws_t, col_scale, tm=1024, tn=1024):
    m, d = h.shape
    n = col_scale.shape[1]
    assert m % tm == 0 and n % tn == 0
    return pl.pallas_call(
        _proj_in_kernel,
        grid=(m // tm, n // tn),
        in_specs=[
            pl.BlockSpec((tm, d), lambda i, j: (i, 0)),
            pl.BlockSpec((1, d), lambda i, j: (0, 0)),
            pl.BlockSpec((None, tn, d), lambda i, j: (layer, j, 0)),
            pl.BlockSpec((LANES, d), lambda i, j: (0, 0)),
            pl.BlockSpec((1, tn), lambda i, j: (0, j)),
        ],
        out_specs=[
            pl.BlockSpec((tm, tn), lambda i, j: (i, j)),
            pl.BlockSpec((tm, LANES), lambda i, j: (i, 0)),
        ],
        out_shape=[jax.ShapeDtypeStruct((m, n), BF16), jax.ShapeDtypeStruct((m, LANES), F32)],
        scratch_shapes=[pltpu.VMEM((tm, d), BF16)],
        compiler_params=_params(("parallel", "arbitrary")),
        name="proj_in",
    )(h, g, w_t, ws_t, col_scale)


def _proj_out_kernel(o_ref, w_ref, h_ref, g_ref, out_ref):
    m = jnp.dot(o_ref[...], w_ref[...], preferred_element_type=F32)
    out_ref[...] = h_ref[...] + _rms(m, g_ref[...])


def _proj_out(o, w, layer, h, g, tm=512):
    m, k = o.shape
    d = w.shape[2]
    return pl.pallas_call(
        _proj_out_kernel,
        grid=(m // tm,),
        in_specs=[
            pl.BlockSpec((tm, k), lambda i: (i, 0)),
            pl.BlockSpec((None, k, d), lambda i: (layer, 0, 0)),
            pl.BlockSpec((tm, d), lambda i: (i, 0)),
            pl.BlockSpec((1, d), lambda i: (0, 0)),
        ],
        out_specs=pl.BlockSpec((tm, d), lambda i: (i, 0)),
        out_shape=jax.ShapeDtypeStruct((m, d), F32),
        compiler_params=_params(("parallel",)),
        name="proj_out",
    )(o, w, h, g)


def _ffn_kernel(h_ref, gin_ref, wg_ref, wu_ref, wd_ref, gout_ref, out_ref, xn_ref):
    f = pl.program_id(1)

    @pl.when(f == 0)
    def _():
        xn_ref[...] = _rms(h_ref[...], gin_ref[...]).astype(BF16)
        out_ref[...] = jnp.zeros_like(out_ref)

    xn = xn_ref[...]
    gate = jnp.dot(xn, wg_ref[...].astype(BF16), preferred_element_type=F32)
    up = jnp.dot(xn, wu_ref[...].astype(BF16), preferred_element_type=F32)
    act = (_silu(gate) * up).astype(BF16)
    out_ref[...] += jnp.dot(act, wd_ref[...].astype(BF16), preferred_element_type=F32)

    @pl.when(f == pl.num_programs(1) - 1)
    def _():
        out_ref[...] = h_ref[...] + _rms(out_ref[...], gout_ref[...])


def _ffn(h, g_in, w_gu, w_down, layer, g_out, tm=1024, tf=256):
    m, d = h.shape
    d_ff = w_down.shape[1]
    nf = d_ff // tf
    assert m % tm == 0 and d_ff % tf == 0
    return pl.pallas_call(
        _ffn_kernel,
        grid=(m // tm, nf),
        in_specs=[
            pl.BlockSpec((tm, d), lambda i, f: (i, 0)),
            pl.BlockSpec((1, d), lambda i, f: (0, 0)),
            pl.BlockSpec((None, d, tf), lambda i, f: (layer, 0, f)),
            pl.BlockSpec((None, d, tf), lambda i, f: (layer, 0, nf + f)),
            pl.BlockSpec((None, tf, d), lambda i, f: (layer, f, 0)),
            pl.BlockSpec((1, d), lambda i, f: (0, 0)),
        ],
        out_specs=pl.BlockSpec((tm, d), lambda i, f: (i, 0)),
        out_shape=jax.ShapeDtypeStruct((m, d), F32),
        scratch_shapes=[pltpu.VMEM((tm, d), BF16)],
        compiler_params=_params(("parallel", "arbitrary")),
        name="ffn",
    )(h, g_in, w_gu, w_gu, w_down, g_out)


def _hgrn_chunk(q_ref, f_ref, i_ref, gate_ref, lb_ref, ng_ref, sel_ref, o_ref, st_ref, bpad_ref, kpad_ref):
    band = HGRN_BAND

    r64 = lax.broadcasted_iota(jnp.int32, (CHUNK, CHUNK), 0)
    c64 = lax.broadcasted_iota(jnp.int32, (CHUNK, CHUNK), 1)
    tril = (r64 >= c64).astype(F32)
    lb = lb_ref[...]
    f_all = lb + (1.0 - lb) * _sigmoid(f_ref[...].astype(F32))
    b_all = _hdot(tril, jnp.log(f_all) * LOG2_E)
    bpad_ref[band:band + CHUNK, :] = b_all
    kpad_ref[band:band + CHUNK, :] = 1.0 - f_all
    yield

    for hd in range(EVEN_HEADS):
        sl = slice(hd * HEAD_DIM, (hd + 1) * HEAD_DIM)
        q = q_ref[:, sl].astype(F32)
        b = b_all[:, sl]
        k = kpad_ref[band:band + CHUNK, sl]
        v = i_ref[:, sl].astype(F32)

        b_ext = bpad_ref[:, sl]
        k_ext = kpad_ref[:, sl]
        terms = []
        for d in range(band):
            bs = b if d == 0 else pltpu.roll(b_ext, d, 0)[band:, :]
            ks = k if d == 0 else pltpu.roll(k_ext, d, 0)[band:, :]
            terms.append((jnp.exp2(b - bs) * (q * ks)).astype(BF16))
        diag = jnp.dot(jnp.concatenate(terms, axis=1), sel_ref[...], preferred_element_type=F32)
        a = pltpu.roll(diag, 1, 1, stride=1, stride_axis=0)

        far = [jnp.zeros((band, CHUNK), F32)]
        for blk in range(1, CHUNK // band):
            r0 = blk * band
            br = b[r0:r0 + 1, :]
            qi = q[r0:r0 + band, :] * jnp.exp2(b[r0:r0 + band, :] - br)
            ki = k * jnp.exp2(jnp.minimum(br - b, 0.0))
            far.append(_bdot_nt(qi, ki))
        a_far = jnp.where(r64 - c64 >= band, jnp.concatenate(far, axis=0), 0.0)

        st = st_ref[hd]
        o = _bdot(a[:, :CHUNK] + a_far, v) + _bdot_nt(q * jnp.exp2(b), st)
        b_end = b[CHUNK - 1:CHUNK, :]
        kd = k * jnp.exp2(b_end - b)
        st_ref[hd] = jnp.exp2(b_end) * st + _bdot(v.T, kd)
        o_ref[:, sl] = (_rms(o, ng_ref[...]) * _silu(gate_ref[:, sl].astype(F32))).astype(o_ref.dtype)
        yield


def _unit_lower_inverses(l_mats):
    r = lax.broadcasted_iota(jnp.int32, (CHUNK, CHUNK), 0)
    c = lax.broadcasted_iota(jnp.int32, (CHUNK, CHUNK), 1)
    eye = (r == c).astype(F32)
    same16 = _div_pow2(r, 16) == _div_pow2(c, 16)
    same32 = _div_pow2(r, 32) == _div_pow2(c, 32)
    m = [_split(jnp.where(same16, -l, 0.0)) for l in l_mats]
    p = [eye + jnp.where(same16, -l, 0.0) for l in l_mats]
    for _ in range(3):
        m = [_split(_dot3(x, x)) for x in m]
        yield
        p = [y + _dot3(_split(y), x) for y, x in zip(p, m)]
        yield
    for off_diag in (same32 & jnp.logical_not(same16), jnp.logical_not(same32)):
        ps = [_split(y) for y in p]
        t = [_dot3(ys, _split(jnp.where(off_diag, l, 0.0))) for ys, l in zip(ps, l_mats)]
        yield
        p = [y - _dot3(_split(x), ys) for y, ys, x in zip(p, ps, t)]
        yield
    return p


def _gdn_chunk(xq_ref, xk_ref, xv_ref, gate_ref, zs_ref, cw_ref, alog_ref, dtb_ref, ng_ref,
               o_ref, s_ref, xbuf_ref):
    w = EVEN_WIDTH

    conv = []
    for a, x_ref in enumerate((xq_ref, xk_ref, xv_ref)):
        x = x_ref[...].astype(F32)
        xbuf_ref[a, 8:8 + CHUNK, :] = x
        x_ext = xbuf_ref[a]
        y = x * cw_ref[CONV_K - 1:CONV_K, a * w:(a + 1) * w]
        for j in range(CONV_K - 1):
            lag = CONV_K - 1 - j
            y = y + pltpu.roll(x_ext, lag, 0)[8:, :] * cw_ref[j:j + 1, a * w:(a + 1) * w]
        xbuf_ref[a, 0:8, :] = x[CHUNK - 8:, :]
        conv.append(_silu(y))
        yield
    cq, ck, cv = conv

    zs = zs_ref[...]
    log_decay = -jnp.exp(alog_ref[...]) * jax.nn.softplus(zs + dtb_ref[...])
    r64 = lax.broadcasted_iota(jnp.int32, (CHUNK, CHUNK), 0)
    c64 = lax.broadcasted_iota(jnp.int32, (CHUNK, CHUNK), 1)
    tril = r64 >= c64
    gcum = _hdot(tril.astype(F32), log_decay)
    gcum_t = gcum.T
    beta_all = _sigmoid(zs)

    heads = range(EVEN_HEADS)
    sls = [slice(hd * HEAD_DIM, (hd + 1) * HEAD_DIM) for hd in heads]
    l2 = lambda x: x * lax.rsqrt(jnp.sum(x * x, axis=-1, keepdims=True) + NORM_EPS)
    q = [l2(cq[:, sl]) * (HEAD_DIM ** -0.5) for sl in sls]
    k = [l2(ck[:, sl]) for sl in sls]
    v = [cv[:, sl] for sl in sls]
    yield
    g_col = [gcum[:, hd:hd + 1] for hd in heads]
    beta = [beta_all[:, EVEN_HEADS + hd:EVEN_HEADS + hd + 1] for hd in heads]
    decay = [jnp.exp(jnp.where(tril, g_col[hd] - gcum_t[hd:hd + 1, :], -jnp.inf)) for hd in heads]
    kb = [k[hd] * beta[hd] for hd in heads]
    l_mat = [jnp.where(r64 > c64, _bdot_nt(kb[hd], k[hd]) * decay[hd], 0.0) for hd in heads]
    yield
    qk = [_bdot_nt(q[hd], k[hd]) * decay[hd] for hd in heads]
    eg = [jnp.exp(g_col[hd]) for hd in heads]
    rhs = [jnp.concatenate([v[hd] * beta[hd], kb[hd] * eg[hd]], axis=-1) for hd in heads]
    inv = yield from _unit_lower_inverses(l_mat)
    sol = [_dot3(_split(inv[hd]), _split(rhs[hd])) for hd in heads]
    yield
    s = [s_ref[hd] for hd in heads]
    v_new = [sol[hd][:, :HEAD_DIM] - _bdot(sol[hd][:, HEAD_DIM:], s[hd]) for hd in heads]
    yield
    o = [_bdot(q[hd] * eg[hd], s[hd]) + _bdot(qk[hd], v_new[hd]) for hd in heads]
    yield
    for hd in heads:
        g_end = g_col[hd][CHUNK - 1:CHUNK, :]
        kd = k[hd] * jnp.exp(g_end - g_col[hd])
        s_ref[hd] = jnp.exp(g_end) * s[hd] + _bdot(kd.T, v_new[hd])
        o_ref[:, sls[hd]] = (_rms(o[hd], ng_ref[...])
                             * _silu(gate_ref[:, sls[hd]].astype(F32))).astype(o_ref.dtype)
        if hd % 2:
            yield


def _even_mixer_kernel(hq_ref, hf_ref, hi_ref, hgate_ref, xq_ref, xk_ref, xv_ref, xgate_ref, zs_ref,
                       lb_ref, hng_ref, cw_ref, alog_ref, dtb_ref, gng_ref, sel_ref, o_ref,
                       st_ref, bpad_ref, kpad_ref, s_ref, xbuf_ref):
    w = EVEN_WIDTH

    @pl.when(pl.program_id(1) == 0)
    def _():
        st_ref[...] = jnp.zeros_like(st_ref)
        s_ref[...] = jnp.zeros_like(s_ref)
        bpad_ref[0:HGRN_BAND, :] = jnp.zeros((HGRN_BAND, w), F32)
        kpad_ref[0:HGRN_BAND, :] = jnp.zeros((HGRN_BAND, w), F32)
        xbuf_ref[:, 0:8, :] = jnp.zeros((3, 8, w), F32)

    hgrn = _hgrn_chunk(hq_ref, hf_ref, hi_ref, hgate_ref, lb_ref, hng_ref, sel_ref, o_ref.at[:, 0:w],
                       st_ref, bpad_ref, kpad_ref)
    gdn = _gdn_chunk(xq_ref, xk_ref, xv_ref, xgate_ref, zs_ref, cw_ref, alog_ref, dtb_ref, gng_ref,
                     o_ref.at[:, w:2 * w], s_ref, xbuf_ref)
    pending = [gdn, gdn, hgrn]
    while pending:
        for gen in list(pending):
            if gen in pending and next(gen, StopIteration) is StopIteration:
                pending = [x for x in pending if x is not gen]


def _even_mixer(z, zs, lb, hgrn_g, conv_w, alog, dtb, gdn_g, batch, seq):
    nchunk = seq // CHUNK
    w = EVEN_WIDTH
    col = lambda c: pl.BlockSpec((CHUNK, w), lambda b, n, c=c: (b * nchunk + n, c))
    vec = lambda width: pl.BlockSpec((1, width), lambda b, n: (0, 0))
    state = pltpu.VMEM((EVEN_HEADS, HEAD_DIM, HEAD_DIM), F32)
    return pl.pallas_call(
        _even_mixer_kernel,
        grid=(batch, nchunk),
        in_specs=[col(c) for c in range(8)] + [
            pl.BlockSpec((CHUNK, LANES), lambda b, n: (b * nchunk + n, 0)),
            vec(w), vec(HEAD_DIM),
            pl.BlockSpec((CONV_K, 3 * w), lambda b, n: (0, 0)),
            vec(LANES), vec(LANES), vec(HEAD_DIM),
            pl.BlockSpec((HGRN_BAND * LANES, LANES), lambda b, n: (0, 0))],
        out_specs=pl.BlockSpec((CHUNK, 2 * w), lambda b, n: (b * nchunk + n, 0)),
        out_shape=jax.ShapeDtypeStruct((batch * seq, 2 * w), BF16),
        scratch_shapes=[state, pltpu.VMEM((HGRN_BAND + CHUNK, w), F32), pltpu.VMEM((HGRN_BAND + CHUNK, w), F32),
                        state, pltpu.VMEM((3, 8 + CHUNK, w), F32)],
        compiler_params=_params(("parallel", "arbitrary")),
        name="even_mixer",
    )(*([z] * 8), zs, lb, hgrn_g, conv_w, alog, dtb, gdn_g, _band_select())


def _band_select():
    sel = np.zeros((HGRN_BAND, LANES, LANES), np.float32)
    for d in range(HGRN_BAND):
        sel[d, :, LANES - 1 - d] = 1.0
    return jnp.asarray(sel.reshape(HGRN_BAND * LANES, LANES), BF16)


def _compress_kernel(kc_ref, vc_ref, pos_ref, w1_ref, w2_ref, ko_ref, vo_ref, x_ref):
    nblk = ko_ref.shape[0]
    row = lax.broadcasted_iota(jnp.int32, (nblk, HEAD_DIM), 0)
    half = CMP_STRIDE * HEAD_DIM
    for a, (src_ref, out_ref) in enumerate(((kc_ref, ko_ref), (vc_ref, vo_ref))):
        x_ref[...] = src_ref[...].astype(F32)
        p1 = jnp.zeros((nblk, HEAD_DIM), F32)
        p2 = jnp.zeros((nblk, HEAD_DIM), F32)
        for l in range(CMP_STRIDE):
            rows = x_ref[pl.ds(l, nblk, stride=CMP_STRIDE), :]
            p1 = p1 + _bdot(rows + pos_ref[a, l:l + 1, :], w1_ref[a, l * HEAD_DIM:(l + 1) * HEAD_DIM, :])
            p2 = p2 + _bdot(rows + pos_ref[a, CMP_STRIDE + l:CMP_STRIDE + l + 1, :],
                            w1_ref[a, half + l * HEAD_DIM:half + (l + 1) * HEAD_DIM, :])
        pre = p1 + pltpu.roll(p2, nblk - 1, 0)
        out = _bdot(_silu(pre), w2_ref[a])
        out_ref[...] = jnp.where(row < nblk - 1, out, 0.0)


def _compress(z, pos, w1, w2, batch, seq):
    nblk = seq // CMP_STRIDE
    src = lambda base: pl.BlockSpec((None, seq, HEAD_DIM), lambda b, g, base=base: (b, 0, base + g))
    out = pl.BlockSpec((None, None, nblk, HEAD_DIM), lambda b, g: (b, g, 0, 0))
    full = lambda a: pl.BlockSpec(a.shape, lambda b, g: (0,) * a.ndim)
    shape = jax.ShapeDtypeStruct((batch, NSA_GROUPS, nblk, HEAD_DIM), F32)
    return pl.pallas_call(
        _compress_kernel,
        grid=(batch, NSA_GROUPS),
        in_specs=[src(16), src(20), full(pos), full(w1), full(w2)],
        out_specs=[out, out],
        out_shape=[shape, shape],
        scratch_shapes=[pltpu.VMEM((seq, HEAD_DIM), F32)],
        compiler_params=_params(("parallel", "parallel")),
        name="nsa_compress",
    )(z, z, pos, w1, w2)


def _nsa_kernel(q_ref, kc_ref, vc_ref, ks_ref, vs_ref, kw_ref, vw_ref, gl_ref, kaug_ref, caug_ref, o_ref,
                m_ref, l_ref, acc_ref, *, seq):
    g = pl.program_id(1)
    qi = pl.program_id(2)
    tq = ATT_TILE
    rows = NSA_HPG * tq
    q0 = qi * tq
    ncmp = seq // CMP_STRIDE
    nblk = seq // SEL_BLOCK

    qb = jnp.concatenate([q_ref[:, hp * HEAD_DIM:(hp + 1) * HEAD_DIM] for hp in range(NSA_HPG)], axis=0)

    rrow = lax.broadcasted_iota(jnp.int32, (rows, LANES), 0)
    rlane = lax.broadcasted_iota(jnp.int32, (rows, LANES), 1)
    hp_idx = _div_pow2(rrow, tq)
    tpos = q0 + (rrow - hp_idx * tq)
    head = (g * NSA_HPG + hp_idx).astype(F32)
    slope = jnp.exp((-8.0 * np.log(2.0) / NSA_HEADS) * (head + 1.0)) * LOG2_E

    s_hi = slope.astype(BF16).astype(F32)
    s_mid = (slope - s_hi).astype(BF16).astype(F32)
    s_lo = slope - s_hi - s_mid
    piece = jnp.where(rlane < KAUG_POS_LANE + 2, s_hi, jnp.where(rlane < KAUG_POS_LANE + 4, s_mid, s_lo))
    bias_cols = jnp.where((rlane >= KAUG_POS_LANE) & (rlane < KAUG_POS_LANE + 6), piece, 0.0)
    q_win = jnp.concatenate([qb, bias_cols.astype(BF16)], axis=1)

    pair = KEY_TILE
    edge = WINDOW // pair
    below_diag = (lax.broadcasted_iota(jnp.int32, (rows, pair), 1)
                  <= jnp.bitwise_and(lax.broadcasted_iota(jnp.int32, (rows, pair), 0), tq - 1))
    causal = lambda sc: jnp.where(below_diag, sc, MASK_VALUE)
    window_edge = lambda sc: jnp.concatenate([jnp.where(below_diag, MASK_VALUE, sc[:, :pair]), sc[:, pair:]],
                                             axis=1)
    fresh = (jnp.full((rows, LANES), MASK_VALUE, F32), jnp.zeros((rows, LANES), F32),
             jnp.zeros((rows, LANES), F32))

    def rows_of(ref, first, ntile):
        tiles = [ref[pl.ds(pl.multiple_of(jnp.maximum(first + t, 0) * pair, pair), pair), :] for t in range(ntile)]
        return tiles[0] if ntile == 1 else jnp.concatenate(tiles, axis=0)

    def scores(qa, k_ref, first, ntile):
        aug = kaug_ref[pl.ds(pl.multiple_of((first + edge) * pair, pair), ntile * pair), :]
        kp = jnp.concatenate([rows_of(k_ref, first, ntile).astype(BF16), aug], axis=1)
        return lax.dot_general(qa, kp, (((1,), (1,)), ((), ())), preferred_element_type=F32)

    def advance(state, sc, values):
        chunks = [sc[:, c * LANES:(c + 1) * LANES] for c in range(sc.shape[1] // LANES)]
        m_new = jnp.maximum(state[0], jnp.max(functools.reduce(jnp.maximum, chunks), axis=-1, keepdims=True))
        p = jnp.concatenate([jnp.exp2(c - m_new) for c in chunks], axis=1).astype(BF16)
        alpha = jnp.exp2(state[0] - m_new)
        ones = jnp.ones((values.shape[0], LANES), BF16)
        r = jnp.dot(p, jnp.concatenate([values.astype(BF16), ones], axis=1), preferred_element_type=F32)
        return m_new, alpha * state[1] + r[:, LANES:], alpha * state[2] + r[:, :LANES]

    s_win_edge = window_edge(scores(q_win, kw_ref, qi - edge, edge))
    s_win_diag = causal(scores(q_win, kw_ref, qi, 1))

    cpos = lax.broadcasted_iota(jnp.int32, (rows, ncmp), 1) * CMP_STRIDE + (CMP_LEN - 1)
    valid_c = tpos[:, :ncmp] >= cpos
    kc_aug = jnp.concatenate([kc_ref[...].astype(BF16), caug_ref[...]], axis=1)
    s_c = lax.dot_general(q_win, kc_aug, (((1,), (1,)), ((), ())), preferred_element_type=F32)
    s_c = jnp.where(valid_c, s_c, MASK_VALUE)
    m_c = jnp.max(s_c, axis=-1, keepdims=True)
    e_c = jnp.where(valid_c, jnp.exp2(s_c - m_c), 0.0)
    den = jnp.sum(e_c, axis=-1, keepdims=True)
    p_c = e_c / jnp.where(den > 0.0, den, 1.0)
    o_cmp = _bdot(p_c, vc_ref[...])
    st_win = advance(fresh, s_win_edge, rows_of(vw_ref, qi - edge, edge))

    p_sum = p_c[0:tq] + p_c[tq:2 * tq] + p_c[2 * tq:3 * tq] + p_c[3 * tq:4 * tq]
    ob = lax.broadcasted_iota(jnp.int32, (KAUG_POS_LANE, ncmp), 0) * SEL_BLOCK
    oj = lax.broadcasted_iota(jnp.int32, (KAUG_POS_LANE, ncmp), 1) * CMP_STRIDE
    overlap_t = ((oj < ob + SEL_BLOCK) & (oj + CMP_LEN > ob)).astype(F32)
    imp_t = lax.dot_general(overlap_t, p_sum, (((1,), (1,)), ((), ())), precision=HIGHEST,
                            preferred_element_type=F32)
    blk = lax.broadcasted_iota(jnp.int32, (KAUG_POS_LANE, tq), 0)
    cur = _div_pow2(q0 + lax.broadcasted_iota(jnp.int32, (KAUG_POS_LANE, tq), 1), SEL_BLOCK)
    forced = (blk == 0) | (blk == cur) | (blk == cur - 1)
    score = jnp.where(forced, SEL_FORCE, jnp.where(blk > cur, -SEL_FORCE, imp_t))
    score = jnp.where(blk < nblk, score, -jnp.inf)
    ahead = jnp.zeros((KAUG_POS_LANE, tq), F32)
    for j in range(nblk):
        sj = score[j:j + 1, :]
        wins_tie = jnp.where(blk > j, 1.0, 0.0)
        ahead = ahead + jnp.where(sj > score, 1.0, jnp.where(sj == score, wins_tie, 0.0))
    sel_t = jnp.where(ahead < float(min(N_SEL, nblk)), 1.0, 0.0)
    sel_t = jnp.concatenate([sel_t, jnp.zeros((LANES - KAUG_POS_LANE, tq), F32)], axis=0)
    sel = sel_t.T
    st_win = advance(st_win, s_win_diag, rows_of(vw_ref, qi, 1))
    o_win = st_win[2] / st_win[1]

    ngate = NSA_HPG * 3
    erow = lax.broadcasted_iota(jnp.int32, (LANES, LANES), 0)
    elane = lax.broadcasted_iota(jnp.int32, (LANES, LANES), 1)
    pick_gate = (erow == g * ngate + elane).astype(F32)
    gates = _hdot(_sigmoid(gl_ref[...]), pick_gate)
    gate = lambda hp, branch: jnp.broadcast_to(gates[:, 3 * hp + branch:3 * hp + branch + 1], (tq, HEAD_DIM))
    gated = [gate(hp, 0) * o_cmp[hp * tq:(hp + 1) * tq] + gate(hp, 2) * o_win[hp * tq:(hp + 1) * tq]
             for hp in range(NSA_HPG)]
    slc_gate = [gate(hp, 1) for hp in range(NSA_HPG)]
    sel_cols = (jnp.concatenate([sel] * NSA_HPG, axis=0) - 1.0) * SEL_MASK_BIG
    q_sel = jnp.concatenate([qb, jnp.where(rlane < KAUG_POS_LANE, sel_cols, bias_cols).astype(BF16)], axis=1)

    for ref, init in zip((m_ref, l_ref, acc_ref), fresh):
        ref[...] = init

    def far_step(j, ntile):
        state = advance((m_ref[...], l_ref[...], acc_ref[...]), scores(q_sel, ks_ref, j, ntile),
                        rows_of(vs_ref, j, ntile))
        for ref, val in zip((m_ref, l_ref, acc_ref), state):
            ref[...] = val

    nfar = jnp.maximum(qi - edge, 0)
    odd = jnp.bitwise_and(nfar, 1)
    pl.when(odd == 1)(lambda: far_step(0, 1))

    def far_body(i, carry):
        far_step(odd + 2 * i, 2)
        return carry

    lax.fori_loop(0, jnp.right_shift(nfar, 1), far_body, 0)

    s_sel_edge = scores(q_sel, ks_ref, qi - edge, edge)
    s_sel_diag = causal(scores(q_sel, ks_ref, qi, 1))
    st_sel = advance((m_ref[...], l_ref[...], acc_ref[...]), s_sel_edge, rows_of(vs_ref, qi - edge, edge))
    st_sel = advance(st_sel, s_sel_diag, rows_of(vs_ref, qi, 1))
    o_slc = st_sel[2] / st_sel[1]

    for hp in range(NSA_HPG):
        o = gated[hp] + slc_gate[hp] * o_slc[hp * tq:(hp + 1) * tq]
        o_ref[:, hp * HEAD_DIM:(hp + 1) * HEAD_DIM] = o.astype(o_ref.dtype)


def _nsa_attention(z, zs, k_cmp, v_cmp, batch, seq):
    tq = ATT_TILE
    hq = NSA_HPG * HEAD_DIM
    nblk = seq // CMP_STRIDE
    assert ATT_TILE == KEY_TILE and seq % KEY_TILE == 0 and seq // SEL_BLOCK <= KAUG_POS_LANE
    assert WINDOW % KEY_TILE == 0
    kv = lambda base: pl.BlockSpec((None, seq, HEAD_DIM), lambda b, g, i, base=base: (b, 0, base + g))
    cmp_spec = pl.BlockSpec((None, None, nblk, HEAD_DIM), lambda b, g, i: (b, g, 0, 0))
    return pl.pallas_call(
        functools.partial(_nsa_kernel, seq=seq),
        grid=(batch, NSA_GROUPS, seq // tq),
        in_specs=[pl.BlockSpec((None, tq, hq), lambda b, g, i: (b, i, g)),
                  cmp_spec, cmp_spec, kv(24), kv(28), kv(32), kv(36),
                  pl.BlockSpec((None, tq, LANES), lambda b, g, i: (b, i, 0)),
                  pl.BlockSpec((seq + WINDOW, LANES), lambda b, g, i: (0, 0)),
                  pl.BlockSpec((nblk, LANES), lambda b, g, i: (0, 0))],
        out_specs=pl.BlockSpec((None, tq, hq), lambda b, g, i: (b, i, g)),
        out_shape=jax.ShapeDtypeStruct((batch, seq, NSA_HEADS * HEAD_DIM), BF16),
        scratch_shapes=[pltpu.VMEM((NSA_HPG * tq, LANES), F32)] * 3,
        compiler_params=_params(("parallel", "parallel", "arbitrary")),
        name="nsa_attention",
    )(z, k_cmp, v_cmp, z, z, z, z, zs, _key_aug(np.arange(-WINDOW, seq), True),
      _key_aug(np.arange(nblk) * CMP_STRIDE + CMP_LEN - 1, False))


def _key_aug(kpos, with_block):
    real = kpos >= 0
    aug = np.zeros((len(kpos), LANES), np.float32)
    if with_block:
        aug[np.nonzero(real)[0], kpos[real] // SEL_BLOCK] = 1.0
    for c in range(3):
        aug[:, KAUG_POS_LANE + 2 * c] = np.where(real, kpos - kpos % 256, -SEL_MASK_BIG)
        aug[:, KAUG_POS_LANE + 2 * c + 1] = np.where(real, kpos % 256, 0)
    return jnp.asarray(aug, BF16)


def _w_in_tail(w_t, layer, n_main):
    tail = w_t[layer, n_main:, :]
    return jnp.pad(tail, ((0, LANES - tail.shape[0]), (0, 0)))


def _pad_lanes(v):
    return jnp.pad(v.astype(F32), (0, LANES - v.shape[0])).reshape(1, LANES)


def _even_layer(h, g_in, g_out, w_in, j, lb, conv_w, a_log, dt_bias, hgrn_g, gdn_g, w_out_bf, batch, seq):
    n_main = 8 * EVEN_WIDTH
    z, zs = _proj_in(h, g_in, w_in, j, _w_in_tail(w_in, j, n_main), jnp.ones((1, n_main), F32))
    o = _even_mixer(z, zs, lb.reshape(1, EVEN_WIDTH), hgrn_g.reshape(1, HEAD_DIM).astype(F32),
                    conv_w.astype(F32), _pad_lanes(a_log), _pad_lanes(dt_bias),
                    gdn_g.reshape(1, HEAD_DIM).astype(F32), batch, seq)
    return _proj_out(o, w_out_bf, j, h, g_out)


def _odd_layer(h, g_in, g_out, w_in, j, cmp_pos, cmp_w1, cmp_w2, w_out_bf, batch, seq):
    n_q = NSA_HEADS * HEAD_DIM
    n_main = n_q + 6 * NSA_GROUPS * HEAD_DIM
    col_scale = jnp.where(jnp.arange(n_main) < n_q, HEAD_DIM ** -0.5 * LOG2_E, 1.0).astype(F32).reshape(1, n_main)
    z, zs = _proj_in(h, g_in, w_in, j, _w_in_tail(w_in, j, n_main), col_scale)
    z3 = z.reshape(batch, seq, n_main)
    k_cmp, v_cmp = _compress(z3, cmp_pos.astype(F32), cmp_w1.astype(BF16), cmp_w2.astype(BF16), batch, seq)
    o = _nsa_attention(z3, zs.reshape(batch, seq, LANES), k_cmp, v_cmp, batch, seq)
    return _proj_out(o.reshape(batch * seq, -1), w_out_bf, j, h, g_out)


def kernel(x, norm_g, ab_w_in, hgrn_lb_logits, gdn_conv_w, gdn_a_log, gdn_dt_bias, hgrn_norm_g, gdn_norm_g,
           ab_w_out, nsa_w_in, nsa_cmp_pos, nsa_cmp_w1, nsa_cmp_w2, nsa_w_out, ffn_w_gu, ffn_w_down):
    batch, seq, d = x.shape
    depth = norm_g.shape[0]
    lb_all = jnp.cumsum(jax.nn.softmax(hgrn_lb_logits.astype(F32), axis=0), axis=0)
    lb_all = lb_all - lb_all[:1]
    g = norm_g.astype(F32).reshape(depth, 4, 1, d)
    ab_w_out_bf, nsa_w_out_bf = ab_w_out.astype(BF16), nsa_w_out.astype(BF16)
    ab_w_in_t, nsa_w_in_t = jnp.swapaxes(ab_w_in, 1, 2), jnp.swapaxes(nsa_w_in, 1, 2)
    h = x.reshape(batch * seq, d)
    for layer in range(depth):
        j = layer // 2
        if layer % 2 == 0:
            h = _even_layer(h, g[layer, 0], g[layer, 1], ab_w_in_t, j, lb_all[j], gdn_conv_w[j],
                            gdn_a_log[j], gdn_dt_bias[j], hgrn_norm_g[j], gdn_norm_g[j], ab_w_out_bf, batch, seq)
        else:
            h = _odd_layer(h, g[layer, 0], g[layer, 1], nsa_w_in_t, j, nsa_cmp_pos[j],
                           nsa_cmp_w1[j], nsa_cmp_w2[j], nsa_w_out_bf, batch, seq)
        h = _ffn(h, g[layer, 2], ffn_w_gu, ffn_w_down, layer, g[layer, 3])
    return h.reshape(batch, seq, d)
```
